```python
import math
import jax
import jax.numpy as jnp
from jax import lax
import numpy as np

D_MODEL = 1024
BATCH = 8
SEQ = 4096
DEPTH = 1

GRID_W = 64
CTX_LEN = 256
EPS = 1e-6

RET_HEADS = 4
RET_QK_DIM = 128
RET_V_DIM = 256
RET_QK_WIDTH = RET_HEADS * RET_QK_DIM
RET_V_WIDTH = RET_HEADS * RET_V_DIM
RET_CHUNK = 128
ROPE_BASE = 10000.0

HY_WIDTH = 512
HY_SHORT = 3
HY_POS_DIM = 33
HY_BANDS = (HY_POS_DIM - 1) // 2
HY_FFN = 64
HY_DECAY_TARGET = 1e-2
HY_FAST_PCT = 0.3
HY_SLOW_PCT = 1.5

N_GROUPS = 4
EXPERTS_PER_GROUP = 8
N_EXPERTS = N_GROUPS * EXPERTS_PER_GROUP
TOP_K = 2
EXPERT_HIDDEN = 512
MOE_BLOCK = 128

IN_WIDTH = 2 * RET_QK_WIDTH + 2 * RET_V_WIDTH + 3 * HY_WIDTH + 2 * D_MODEL
IN_SPLITS = (RET_QK_WIDTH, 2 * RET_QK_WIDTH, 2 * RET_QK_WIDTH + RET_V_WIDTH, 2 * RET_QK_WIDTH + 2 * RET_V_WIDTH, 2 * RET_QK_WIDTH + 2 * RET_V_WIDTH + 3 * HY_WIDTH, 2 * RET_QK_WIDTH + 2 * RET_V_WIDTH + 3 * HY_WIDTH + D_MODEL)

kernel_name = 'hybrid_retention_hyena_hmoe_dit'


def rms_norm(x, gain=None):
    x32 = x.astype(jnp.float32)
    y = x32 * lax.rsqrt(jnp.mean(jnp.square(x32), axis=-1, keepdims=True) + EPS)
    if gain is not None:
        y = y * gain.astype(jnp.float32)
    return y.astype(x.dtype)


def modulate(h, shift, scale):
    return h * (1.0 + scale) + shift


def grid_rope_angles(n_tokens):
    rows = n_tokens // GRID_W
    r, col = jnp.meshgrid(jnp.arange(rows, dtype=jnp.float32), jnp.arange(GRID_W, dtype=jnp.float32), indexing='ij')
    n_freq = RET_QK_DIM // 4
    inv_freq = ROPE_BASE ** (-jnp.arange(n_freq, dtype=jnp.float32) / n_freq)
    return r.reshape(-1)[:, None] * inv_freq, col.reshape(-1)[:, None] * inv_freq


def rope_1d(x, ang):
    n_freq = ang.shape[-1]
    x1, x2 = x[..., :n_freq], x[..., n_freq:]
    cos, sin = jnp.cos(ang), jnp.sin(ang)
    return jnp.concatenate([x1 * cos - x2 * sin, x1 * sin + x2 * cos], axis=-1)


def rope_grid(x, ang_row, ang_col):
    half = RET_QK_DIM // 2
    return jnp.concatenate([rope_1d(x[..., :half], ang_row), rope_1d(x[..., half:], ang_col)], axis=-1)


def split_heads(t, n_heads):
    b, n, _ = t.shape
    return t.reshape(b, n, n_heads, -1).transpose(0, 2, 1, 3)


def retention_scan(q, k, v, log_g, s0, strict):
    b, h, n, _ = q.shape
    dv = v.shape[-1]
    n_chunks = n // RET_CHUNK
    pos = jnp.arange(RET_CHUNK, dtype=jnp.float32)
    diff = pos[:, None] - pos[None, :]
    mask = (diff > 0) if strict else (diff >= 0)
    lg = log_g[:, None, None]
    decay_in = jnp.where(mask[None], jnp.exp(lg * jnp.maximum(diff, 0.0)[None]), 0.0)
    decay_q = jnp.exp(lg * (pos + 1.0)[None, :, None])
    decay_k = jnp.exp(lg * (RET_CHUNK - 1.0 - pos)[None, :, None])
    decay_chunk = jnp.exp(lg * RET_CHUNK)

    def chunks(t):
        return t.reshape(b, h, n_chunks, RET_CHUNK, t.shape[-1]).transpose(2, 0, 1, 3, 4)

    def step(state, qkv):
        qc, kc, vc = qkv
        scores = jnp.einsum('bhid,bhjd->bhij', qc, kc) * decay_in
        out = jnp.einsum('bhij,bhjv->bhiv', scores, vc) + jnp.einsum('bhid,bhdv->bhiv', qc, state) * decay_q
        state = state * decay_chunk + jnp.einsum('bhjd,bhjv->bhdv', kc * decay_k, vc)
        return state, out

    s_final, o = lax.scan(step, s0, (chunks(q), chunks(k), chunks(v)))
    return o.transpose(1, 2, 0, 3, 4).reshape(b, h, n, dv), s_final


def retention_bidir(q, k, v, log_g, s0_fwd, s0_bwd):
    o_f, s_f = retention_scan(q, k, v, log_g[0], s0_fwd, False)
    o_b, s_b = retention_scan(jnp.flip(q, 2), jnp.flip(k, 2), jnp.flip(v, 2), log_g[1], s0_bwd, True)
    return o_f + jnp.flip(o_b, 2), s_f, s_b


def retention_readout(o, g):
    b, h, n, dv = o.shape
    o = rms_norm(o).transpose(0, 2, 1, 3).reshape(b, n, h * dv)
    return jax.nn.silu(g) * o.astype(g.dtype)


def short_conv(u, w, bias):
    n = u.shape[1]
    pad = HY_SHORT // 2
    up = jnp.pad(u, ((0, 0), (pad, pad), (0, 0)))
    return sum(up[:, j:j + n] * w[j] for j in range(HY_SHORT)) + bias


def hyena_filters(n, w1, b1, freq, w2, b2, w3):
    f32 = jnp.float32
    t = jnp.arange(n, dtype=f32) / n
    bands = jnp.linspace(1e-4, HY_BANDS - 1, HY_BANDS, dtype=f32)
    phase = 2.0 * math.pi * t[:, None] * bands[None, :]
    feats = jnp.concatenate([t[:, None], jnp.cos(phase), -jnp.sin(phase)], axis=-1)
    freq = freq.astype(f32)
    hid = jnp.sin(freq * (feats @ w1.astype(f32) + b1.astype(f32)))
    hid = jnp.sin(freq * (hid @ w2.astype(f32) + b2.astype(f32)))
    filt = hid @ w3.astype(f32)
    slow = abs(math.log(HY_DECAY_TARGET)) / HY_SLOW_PCT
    fast = abs(math.log(HY_DECAY_TARGET)) / HY_FAST_PCT
    deltas = jnp.tile(jnp.linspace(slow, fast, HY_WIDTH, dtype=f32), 2)
    filt = filt * jnp.exp(-t[:, None] * deltas[None, :])
    filt = filt / jnp.sum(jnp.abs(filt), axis=0, keepdims=True)
    return filt[:, :HY_WIDTH], filt[:, HY_WIDTH:]


def long_conv_bidir(z, h_fwd, h_bwd):
    n = z.shape[1]
    nfft = 2 * n
    zf = jnp.fft.rfft(z.astype(jnp.float32), n=nfft, axis=1)
    hf = jnp.fft.rfft(h_fwd, n=nfft, axis=0) + jnp.conj(jnp.fft.rfft(h_bwd, n=nfft, axis=0))
    return jnp.fft.irfft(zf * hf[None], n=nfft, axis=1)[:, :n]


def hyena_branch(u, conv_w, conv_b, ffn_w1, ffn_b1, ffn_freq, ffn_w2, ffn_b2, ffn_w3, skip):
    x0, x1, v = jnp.split(short_conv(u, conv_w, conv_b), 3, axis=-1)
    z = x1 * v
    h_fwd, h_bwd = hyena_filters(u.shape[1], ffn_w1, ffn_b1, ffn_freq, ffn_w2, ffn_b2, ffn_w3)
    y = long_conv_bidir(z, h_fwd, h_bwd).astype(u.dtype) + z * skip
    return x0 * y


def merge_branches(ret_gated, hy_out, gate_ret, gate_hy, ret_w_o, hy_w_o, w_out):
    mixed = jax.nn.sigmoid(gate_ret) * (ret_gated @ ret_w_o) + jax.nn.sigmoid(gate_hy) * (hy_out @ hy_w_o)
    return mixed @ w_out


def moe_ffn(u, rg_w, rg_b, re_w, re_b, w1, w3, w2):
    n_tok = u.shape[0]
    f32 = jnp.float32
    group_p = jax.nn.softmax((u @ rg_w).astype(f32) + rg_b.astype(f32), axis=-1)
    p_star, g_star = lax.top_k(group_p, 1)
    expert_logits = ((u @ re_w).astype(f32) + re_b.astype(f32)).reshape(n_tok, N_GROUPS, EXPERTS_PER_GROUP)
    in_group = jnp.take_along_axis(expert_logits, g_star[:, :, None], axis=1)[:, 0]
    w_top, e_top = lax.top_k(jax.nn.softmax(in_group, axis=-1), TOP_K)
    weights = p_star * w_top / jnp.sum(w_top, axis=-1, keepdims=True)
    expert_id = g_star * EXPERTS_PER_GROUP + e_top
    n_assign = n_tok * TOP_K
    flat_e = expert_id.reshape(n_assign)
    flat_tok = jnp.arange(n_assign, dtype=jnp.int32) // TOP_K
    flat_w = weights.reshape(n_assign)
    order = jnp.argsort(flat_e)
    se, stok, sw = flat_e[order], flat_tok[order], flat_w[order]
    counts = jnp.bincount(flat_e, length=N_EXPERTS)
    starts = jnp.cumsum(counts) - counts
    padded = (counts + MOE_BLOCK - 1) // MOE_BLOCK * MOE_BLOCK
    padded_end = jnp.cumsum(padded)
    padded_start = padded_end - padded
    dest = padded_start[se] + jnp.arange(n_assign, dtype=jnp.int32) - starts[se]
    n_blocks = -(-(n_assign + N_EXPERTS * (MOE_BLOCK - 1)) // MOE_BLOCK)
    n_rows = n_blocks * MOE_BLOCK
    row_tok = jnp.zeros((n_rows,), jnp.int32).at[dest].set(stok)
    block_expert = jnp.minimum(jnp.searchsorted(padded_end, jnp.arange(n_blocks, dtype=jnp.int32) * MOE_BLOCK, side='right'), N_EXPERTS - 1)
    xb = u[row_tok].reshape(n_blocks, MOE_BLOCK, u.shape[1])

    def expert_block(args):
        xblk, e = args
        return (jax.nn.silu(xblk @ w1[e]) * (xblk @ w3[e])) @ w2[e]

    yb = lax.map(expert_block, (xb, block_expert)).reshape(n_rows, u.shape[1])
    y = yb[dest] * sw[:, None].astype(u.dtype)
    return jax.ops.segment_sum(y, stok, num_segments=n_tok)


def setup_inputs(seed: int = 0) -> dict:
    key = jax.random.key(seed)
    ks = jax.random.split(key, 32)
    f32 = jnp.float32

    def nrm(k, shape, scale):
        return scale * jax.random.normal(k, shape, f32)

    base_logit = np.log(2.0 ** (5 + np.arange(RET_HEADS)) - 1.0).astype(np.float32)
    return {
        'x': nrm(ks[0], (BATCH, SEQ, D_MODEL), 1.0),
        'c': nrm(ks[1], (BATCH, D_MODEL), 1.0),
        'ctx': nrm(ks[2], (BATCH, CTX_LEN, D_MODEL), 1.0),
        'c_ctx': nrm(ks[3], (D_MODEL,), 1.0),
        'ada_w': nrm(ks[4], (DEPTH, D_MODEL, 6 * D_MODEL), 0.5 * D_MODEL ** -0.5),
        'ada_b': nrm(ks[5], (DEPTH, 6 * D_MODEL), 0.02),
        'norm1_g': 1.0 + nrm(ks[6], (DEPTH, D_MODEL), 0.05),
        'norm2_g': 1.0 + nrm(ks[7], (DEPTH, D_MODEL), 0.05),
        'w_in': nrm(ks[8], (DEPTH, D_MODEL, IN_WIDTH), D_MODEL ** -0.5),
        'b_in': nrm(ks[9], (DEPTH, IN_WIDTH), 0.02),
        'ret_decay_logit': jnp.asarray(base_logit)[None, None, :] + nrm(ks[10], (DEPTH, 2, RET_HEADS), 0.1),
        'ret_w_o': nrm(ks[11], (DEPTH, RET_V_WIDTH, D_MODEL), RET_V_WIDTH ** -0.5),
        'hy_conv_w': nrm(ks[12], (DEPTH, HY_SHORT, 3 * HY_WIDTH), HY_SHORT ** -0.5),
        'hy_conv_b': nrm(ks[13], (DEPTH, 3 * HY_WIDTH), 0.02),
        'hy_ffn_w1': nrm(ks[14], (DEPTH, HY_POS_DIM, HY_FFN), HY_POS_DIM ** -0.5),
        'hy_ffn_b1': nrm(ks[15], (DEPTH, HY_FFN), 0.02),
        'hy_ffn_freq': 1.0 + nrm(ks[16], (DEPTH, HY_FFN), 0.05),
        'hy_ffn_w2': nrm(ks[17], (DEPTH, HY_FFN, HY_FFN), HY_FFN ** -0.5),
        'hy_ffn_b2': nrm(ks[18], (DEPTH, HY_FFN), 0.02),
        'hy_ffn_w3': nrm(ks[19], (DEPTH, HY_FFN, 2 * HY_WIDTH), HY_FFN ** -0.5),
        'hy_skip': nrm(ks[20], (DEPTH, HY_WIDTH), 0.5),
        'hy_w_o': nrm(ks[21], (DEPTH, HY_WIDTH, D_MODEL), HY_WIDTH ** -0.5),
        'w_out': nrm(ks[22], (DEPTH, D_MODEL, D_MODEL), D_MODEL ** -0.5),
        'router_group_w': nrm(ks[23], (DEPTH, D_MODEL, N_GROUPS), D_MODEL ** -0.5),
        'router_group_b': nrm(ks[24], (DEPTH, N_GROUPS), 0.01),
        'router_expert_w': nrm(ks[25], (DEPTH, D_MODEL, N_EXPERTS), D_MODEL ** -0.5),
        'router_expert_b': nrm(ks[26], (DEPTH, N_EXPERTS), 0.01),
        'expert_w1': nrm(ks[27], (DEPTH, N_EXPERTS, D_MODEL, EXPERT_HIDDEN), D_MODEL ** -0.5),
        'expert_w3': nrm(ks[28], (DEPTH, N_EXPERTS, D_MODEL, EXPERT_HIDDEN), D_MODEL ** -0.5),
        'expert_w2': nrm(ks[29], (DEPTH, N_EXPERTS, EXPERT_HIDDEN, D_MODEL), EXPERT_HIDDEN ** -0.5),
        'final_norm_g': 1.0 + nrm(ks[30], (D_MODEL,), 0.05),
    }


def reference(x, c, ctx, c_ctx, ada_w, ada_b, norm1_g, norm2_g, w_in, b_in, ret_decay_logit, ret_w_o, hy_conv_w, hy_conv_b, hy_ffn_w1, hy_ffn_b1, hy_ffn_freq, hy_ffn_w2, hy_ffn_b2, hy_ffn_w3, hy_skip, hy_w_o, w_out, router_group_w, router_group_b, router_expert_w, router_expert_b, expert_w1, expert_w3, expert_w2, final_norm_g):
    f32 = jnp.float32
    batch, n_tokens = x.shape[0], x.shape[1]
    ang_row, ang_col = grid_rope_angles(n_tokens)
    silu_c = jax.nn.silu(c)
    silu_cc = jax.nn.silu(c_ctx)
    k_scale = RET_QK_DIM ** -0.5
    for layer in range(DEPTH):
        last = layer == DEPTH - 1
        sh1, sc1, gt1, sh2, sc2, gt2 = jnp.split((silu_c @ ada_w[layer] + ada_b[layer])[:, None, :], 6, axis=-1)
        csh1, csc1, cgt1, csh2, csc2, cgt2 = jnp.split(silu_cc @ ada_w[layer] + ada_b[layer], 6, axis=-1)
        h = modulate(rms_norm(x, norm1_g[layer]), sh1, sc1)
        hc = modulate(rms_norm(ctx, norm1_g[layer]), csh1, csc1)
        q, k, v, g, hy_u, gate_ret, gate_hy = jnp.split(h @ w_in[layer] + b_in[layer], IN_SPLITS, axis=-1)
        qc, kc, vc, gc, hyc_u, gate_ret_c, gate_hy_c = jnp.split(hc @ w_in[layer] + b_in[layer], IN_SPLITS, axis=-1)
        log_g = jax.nn.log_sigmoid(ret_decay_logit[layer].astype(f32))
        zero_state = jnp.zeros((batch, RET_HEADS, RET_QK_DIM, RET_V_DIM), f32)
        o_ctx, s_fwd, s_bwd = retention_bidir(split_heads(qc, RET_HEADS).astype(f32), split_heads(kc, RET_HEADS).astype(f32) * k_scale, split_heads(vc, RET_HEADS).astype(f32), log_g, zero_state, zero_state)
        q_h = rope_grid(split_heads(q, RET_HEADS).astype(f32), ang_row, ang_col)
        k_h = rope_grid(split_heads(k, RET_HEADS).astype(f32), ang_row, ang_col) * k_scale
        o_lat, _, _ = retention_bidir(q_h, k_h, split_heads(v, RET_HEADS).astype(f32), log_g, s_fwd, s_bwd)
        hy_args = (hy_conv_w[layer], hy_conv_b[layer], hy_ffn_w1[layer], hy_ffn_b1[layer], hy_ffn_freq[layer], hy_ffn_w2[layer], hy_ffn_b2[layer], hy_ffn_w3[layer], hy_skip[layer])
        out_args = (ret_w_o[layer], hy_w_o[layer], w_out[layer])
        moe_args = (router_group_w[layer], router_group_b[layer], router_expert_w[layer], router_expert_b[layer], expert_w1[layer], expert_w3[layer], expert_w2[layer])
        mix = merge_branches(retention_readout(o_lat, g), hyena_branch(hy_u, *hy_args), gate_ret, gate_hy, *out_args)
        x = x + gt1 * mix
        h2 = modulate(rms_norm(x, norm2_g[layer]), sh2, sc2)
        if last:
            x = x + gt2 * moe_ffn(h2.reshape(-1, D_MODEL), *moe_args).reshape(x.shape)
        else:
            mix_c = merge_branches(retention_readout(o_ctx, gc), hyena_branch(hyc_u, *hy_args), gate_ret_c, gate_hy_c, *out_args)
            ctx = ctx + cgt1 * mix_c
            h2c = modulate(rms_norm(ctx, norm2_g[layer]), csh2, csc2)
            y = moe_ffn(jnp.concatenate([h2.reshape(-1, D_MODEL), h2c.reshape(-1, D_MODEL)], axis=0), *moe_args)
            n_lat = batch * n_tokens
            x = x + gt2 * y[:n_lat].reshape(x.shape)
            ctx = ctx + cgt2 * y[n_lat:].reshape(ctx.shape)
    return rms_norm(x, final_norm_g)
```

```python
import functools
import math

import jax
import jax.numpy as jnp
from jax import lax
from jax.experimental import pallas as pl
from jax.experimental.pallas import tpu as pltpu

F32 = jnp.float32
BF16 = jnp.bfloat16

D_MODEL = 1024
EPS = 1e-6
GRID_W = 64
ROPE_BASE = 10000.0

RET_HEADS = 4
RET_QK_DIM = 128
RET_V_DIM = 256
RET_CHUNK = 128
QK_W = RET_HEADS * RET_QK_DIM
V_W = RET_HEADS * RET_V_DIM

HY_W = 512
HY_POS_DIM = 33
HY_BANDS = (HY_POS_DIM - 1) // 2
HY_FFN = 64
FFN_LANES = 128
HY_DECAY_TARGET = 1e-2
HY_FAST_PCT = 0.3
HY_SLOW_PCT = 1.5

N_GROUPS = 4
EXPERTS_PER_GROUP = 8
N_EXPERTS = N_GROUPS * EXPERTS_PER_GROUP
EXPERT_HIDDEN = 512
ROUTE_LANES = 128
EXPERT_LANE0 = N_GROUPS
MOE_BLK = 256

IN_W = 2 * QK_W + 2 * V_W + 3 * HY_W + 2 * D_MODEL
COL_Q, COL_K, COL_V, COL_G = 0, QK_W, 2 * QK_W, 2 * QK_W + V_W
COL_HY = 2 * QK_W + 2 * V_W
COL_GR = COL_HY + 3 * HY_W
COL_GH = COL_GR + D_MODEL

VMEM_LIMIT = 56 * 1024 * 1024
NEG = -1e30


def _cparams(*sem):
    return pltpu.CompilerParams(dimension_semantics=sem, vmem_limit_bytes=VMEM_LIMIT)


def _sigmoid(x):
    return 1.0 / (1.0 + jnp.exp(-x))


def _dot(a, b):
    return jnp.dot(a, b, preferred_element_type=F32)


def _dot_t0(a, b):
    return lax.dot_general(a, b, (((0,), (0,)), ((), ())), preferred_element_type=F32)


def _dot_nt(a, b):
    return lax.dot_general(a, b, (((1,), (1,)), ((), ())), preferred_element_type=F32)


def _split_bf16(a):
    hi = a.astype(BF16)
    lo = (a - hi.astype(F32)).astype(BF16)
    return hi, lo


def _dot3(a, b):
    ah, al = _split_bf16(a)
    bh, bl = _split_bf16(b)
    return _dot(ah, bh) + _dot(al, bh) + _dot(ah, bl)


def _adaln_kernel(c_ref, w_ref, b_ref, o_ref):
    c = c_ref[...]
    o_ref[...] = _dot3(c * _sigmoid(c), w_ref[...]) + b_ref[...]


def _adaln(cc, w, b):
    rows, d = cc.shape
    n = w.shape[1]
    tn = 1536
    return pl.pallas_call(
        _adaln_kernel,
        grid=(n // tn,),
        in_specs=[pl.BlockSpec((rows, d), lambda j: (0, 0)),
                  pl.BlockSpec((d, tn), lambda j: (0, j)),
                  pl.BlockSpec((1, tn), lambda j: (0, j))],
        out_specs=pl.BlockSpec((rows, tn), lambda j: (0, j)),
        out_shape=jax.ShapeDtypeStruct((rows, n), F32),
        compiler_params=_cparams("arbitrary"),
        name="adaln",
    )(cc, w, b)


def _inproj_kernel(x_ref, g_ref, sh_ref, sc_ref, w_ref, b_ref, o_ref, h_scr):
    @pl.when(pl.program_id(2) == 0)
    def _():
        x = x_ref[0]
        y = x * lax.rsqrt(jnp.mean(x * x, axis=-1, keepdims=True) + EPS) * g_ref[...]
        h_scr[...] = (y * (1.0 + sc_ref[0]) + sh_ref[0]).astype(BF16)

    o_ref[0] = (_dot(h_scr[...], w_ref[...].astype(BF16)) + b_ref[...]).astype(o_ref.dtype)


def _inproj(x, gain, shift, scale, w, b, tm, tn):
    bsz, n, d = x.shape
    nw = w.shape[1]
    return pl.pallas_call(
        _inproj_kernel,
        grid=(bsz, n // tm, nw // tn),
        in_specs=[pl.BlockSpec((1, tm, d), lambda bi, i, j: (bi, i, 0)),
                  pl.BlockSpec((1, d), lambda bi, i, j: (0, 0)),
                  pl.BlockSpec((1, 1, d), lambda bi, i, j: (bi, 0, 0)),
                  pl.BlockSpec((1, 1, d), lambda bi, i, j: (bi, 0, 0)),
                  pl.BlockSpec((d, tn), lambda bi, i, j: (0, j)),
                  pl.BlockSpec((1, tn), lambda bi, i, j: (0, j))],
        out_specs=pl.BlockSpec((1, tm, tn), lambda bi, i, j: (bi, i, j)),
        out_shape=jax.ShapeDtypeStruct((bsz, n, nw), BF16),
        scratch_shapes=[pltpu.VMEM((tm, d), BF16)],
        compiler_params=_cparams("arbitrary", "arbitrary", "arbitrary"),
        name="inproj",
    )(x, gain, shift, scale, w, b)


def _log_sigmoid(x):
    return jnp.minimum(x, 0.0) - jnp.log(1.0 + jnp.exp(-jnp.abs(x)))


def _rope_partner(x):
    lane = lax.broadcasted_iota(jnp.int32, x.shape, 1)
    return jnp.where((lane % 64) < 32, pltpu.roll(x, 96, axis=1), pltpu.roll(x, 32, axis=1))


def _retention_kernel(q_ref, k_ref, v_ref, g_ref, kc_ref, vc_ref, cos_ref, sin_ref, lgt_ref,
                      o_ref, qr_scr, kr_scr, oacc_scr, sf_scr, sb_scr, *, n_tok, n_ctx):
    c_len = RET_CHUNK
    n_chunks = n_tok // c_len
    head = pl.program_id(1)
    k_scale = RET_QK_DIM ** -0.5

    lg_f = _log_sigmoid(lgt_ref[pl.ds(head, 1), :])
    lg_b = _log_sigmoid(lgt_ref[pl.ds(RET_HEADS + head, 1), :])
    lgf_k, lgb_k = lg_f[:, :RET_QK_DIM], lg_b[:, :RET_QK_DIM]

    ii = lax.broadcasted_iota(jnp.int32, (c_len, c_len), 0)
    jj = lax.broadcasted_iota(jnp.int32, (c_len, c_len), 1)
    dif = (ii - jj).astype(F32)
    decay_in = jnp.where(ii >= jj, jnp.exp(lgf_k * jnp.maximum(dif, 0.0)), 0.0) + \
        jnp.where(jj > ii, jnp.exp(lgb_k * jnp.maximum(-dif, 0.0)), 0.0)
    pos_k = ii.astype(F32)
    pos_v = lax.broadcasted_iota(jnp.int32, (c_len, RET_V_DIM), 0).astype(F32)
    dq_f = jnp.exp(lg_f * (pos_v + 1.0))
    dq_b = jnp.exp(lg_b * (c_len - pos_v))
    dk_f = jnp.exp(lgf_k * (c_len - 1.0 - pos_k))
    dk_b = jnp.exp(lgb_k * pos_k)
    dchunk_f = jnp.exp(lg_f * float(c_len))
    dchunk_b = jnp.exp(lg_b * float(c_len))

    pos_c = lax.broadcasted_iota(jnp.int32, (n_ctx, RET_QK_DIM), 0).astype(F32)
    kc = kc_ref[0].astype(F32) * k_scale
    vc = vc_ref[0]
    sf_scr[...] = _dot_t0((kc * jnp.exp(lgf_k * (n_ctx - 1.0 - pos_c))).astype(BF16), vc)
    sb_scr[...] = _dot_t0((kc * jnp.exp(lgb_k * pos_c)).astype(BF16), vc)

    def fwd(c, carry):
        r0 = pl.multiple_of(c * c_len, c_len)
        rows = pl.ds(r0, c_len)
        cos = cos_ref[rows, :]
        sin = sin_ref[rows, :]
        q = q_ref[0, rows, :].astype(F32)
        k = k_ref[0, rows, :].astype(F32)
        qr = q * cos + _rope_partner(q) * sin
        kr = (k * cos + _rope_partner(k) * sin) * k_scale
        qb = qr.astype(BF16)
        kb = kr.astype(BF16)
        qr_scr[rows, :] = qb
        kr_scr[rows, :] = kr
        v = v_ref[0, rows, :]
        scores = _dot_nt(qb, kb) * decay_in
        o = _dot(scores.astype(BF16), v) + _dot(qb, sf_scr[...].astype(BF16)) * dq_f
        oacc_scr[rows, :] = o
        sf_scr[...] = sf_scr[...] * dchunk_f + _dot_t0((kr * dk_f).astype(BF16), v)
        return carry

    lax.fori_loop(0, n_chunks, fwd, 0)

    def bwd(t, carry):
        c = n_chunks - 1 - t
        r0 = pl.multiple_of(c * c_len, c_len)
        rows = pl.ds(r0, c_len)
        v = v_ref[0, rows, :]
        o = oacc_scr[rows, :] + _dot(qr_scr[rows, :], sb_scr[...].astype(BF16)) * dq_b
        o = o * lax.rsqrt(jnp.mean(o * o, axis=-1, keepdims=True) + EPS)
        g = g_ref[0, rows, :].astype(F32)
        o_ref[0, rows, :] = (g * _sigmoid(g) * o).astype(o_ref.dtype)
        sb_scr[...] = sb_scr[...] * dchunk_b + _dot_t0((kr_scr[rows, :] * dk_b).astype(BF16), v)
        return carry

    lax.fori_loop(0, n_chunks, bwd, 0)


def _retention(proj, proj_c, cos_t, sin_t, lgt):
    bsz, n, _ = proj.shape
    n_ctx = proj_c.shape[1]
    kq, kv = RET_QK_DIM, RET_V_DIM
    kern = functools.partial(_retention_kernel, n_tok=n, n_ctx=n_ctx)
    return pl.pallas_call(
        kern,
        grid=(bsz, RET_HEADS),
        in_specs=[pl.BlockSpec((1, n, kq), lambda b, h: (b, 0, COL_Q // kq + h)),
                  pl.BlockSpec((1, n, kq), lambda b, h: (b, 0, COL_K // kq + h)),
                  pl.BlockSpec((1, n, kv), lambda b, h: (b, 0, COL_V // kv + h)),
                  pl.BlockSpec((1, n, kv), lambda b, h: (b, 0, COL_G // kv + h)),
                  pl.BlockSpec((1, n_ctx, kq), lambda b, h: (b, 0, h)),
                  pl.BlockSpec((1, n_ctx, kv), lambda b, h: (b, 0, QK_W // kv + h)),
                  pl.BlockSpec((n, kq), lambda b, h: (0, 0)),
                  pl.BlockSpec((n, kq), lambda b, h: (0, 0)),
                  pl.BlockSpec((2 * RET_HEADS, kv), lambda b, h: (0, 0))],
        out_specs=pl.BlockSpec((1, n, kv), lambda b, h: (b, 0, h)),
        out_shape=jax.ShapeDtypeStruct((bsz, n, V_W), BF16),
        scratch_shapes=[pltpu.VMEM((n, kq), BF16), pltpu.VMEM((n, kq), F32),
                        pltpu.VMEM((n, kv), F32), pltpu.VMEM((kq, kv), F32),
                        pltpu.VMEM((kq, kv), F32)],
        compiler_params=_cparams("arbitrary", "arbitrary"),
        name="retention",
    )(proj, proj, proj, proj, proj_c, proj_c, cos_t, sin_t, lgt)


def _filter_kernel(feats_ref, w1_ref, b1_ref, fr_ref, w2_ref, b2_ref, w3f_ref, w3b_ref,
                   df_ref, db_ref, fs_ref, fd_ref, *, n_tok):
    fr = fr_ref[...]
    hid = jnp.sin(fr * (_dot3(feats_ref[...], w1_ref[...]) + b1_ref[...]))
    hid = jnp.sin(fr * (_dot3(hid, w2_ref[...]) + b2_ref[...]))
    t = lax.broadcasted_iota(jnp.int32, (n_tok, df_ref.shape[1]), 0).astype(F32) / n_tok

    def one(w3_ref, d_ref):
        f = _dot3(hid, w3_ref[...]) * jnp.exp(-t * d_ref[...])
        return f / jnp.sum(jnp.abs(f), axis=0, keepdims=True)

    hf = one(w3f_ref, df_ref)
    hb = one(w3b_ref, db_ref)
    fs_ref[...] = (hf + hb).astype(fs_ref.dtype)
    fd_ref[...] = (hf - hb).astype(fd_ref.dtype)


def _hyena_filters(n, feats, w1, b1, freq, w2, b2, w3, deltas):
    tc = 256
    nf = feats.shape[1]
    kern = functools.partial(_filter_kernel, n_tok=n)
    full = lambda shape: pl.BlockSpec(shape, lambda j: (0, 0))
    return pl.pallas_call(
        kern,
        grid=(HY_W // tc,),
        in_specs=[full((n, nf)), full((nf, FFN_LANES)), full((1, FFN_LANES)), full((1, FFN_LANES)),
                  full((FFN_LANES, FFN_LANES)), full((1, FFN_LANES)),
                  pl.BlockSpec((FFN_LANES, tc), lambda j: (0, j)),
                  pl.BlockSpec((FFN_LANES, tc), lambda j: (0, HY_W // tc + j)),
                  pl.BlockSpec((1, tc), lambda j: (0, j)),
                  pl.BlockSpec((1, tc), lambda j: (0, HY_W // tc + j))],
        out_specs=[pl.BlockSpec((n, tc), lambda j: (0, j)),
                   pl.BlockSpec((n, tc), lambda j: (0, j))],
        out_shape=[jax.ShapeDtypeStruct((n, HY_W), BF16)] * 2,
        compiler_params=_cparams("arbitrary"),
        name="hyena_filters",
    )(feats, w1, b1, freq, w2, b2, w3, w3, deltas, deltas)


def _hyena_pre_kernel(u0_ref, u1_ref, u2_ref, w_ref, b_ref, z_ref, x0_ref, *, n_tok, rows):
    n_steps = n_tok // rows
    tc = z_ref.shape[1]
    halo = 16
    rid = lax.broadcasted_iota(jnp.int32, (rows, tc), 0)

    def conv(u_ref, part, r0, has_prev, has_next):
        x = u_ref[0, pl.ds(r0, rows), :].astype(F32)
        prev_g = u_ref[0, pl.ds(pl.multiple_of(jnp.maximum(r0 - halo, 0), halo), halo), :].astype(F32)
        next_g = u_ref[0, pl.ds(pl.multiple_of(jnp.minimum(r0 + rows, n_tok - halo), halo), halo), :].astype(F32)
        prev_row = jnp.where(has_prev, prev_g[halo - 1:halo, :], 0.0)
        next_row = jnp.where(has_next, next_g[0:1, :], 0.0)
        up = jnp.where(rid == 0, prev_row, pltpu.roll(x, 1, axis=0))
        dn = jnp.where(rid == rows - 1, next_row, pltpu.roll(x, rows - 1, axis=0))
        w = w_ref[part]
        return up * w[0:1, :] + x * w[1:2, :] + dn * w[2:3, :] + b_ref[part]

    def body(s, carry):
        r0 = pl.multiple_of(s * rows, rows)
        has_prev = s > 0
        has_next = s < n_steps - 1
        x0 = conv(u0_ref, 0, r0, has_prev, has_next)
        x1 = conv(u1_ref, 1, r0, has_prev, has_next)
        vv = conv(u2_ref, 2, r0, has_prev, has_next)
        z_ref[pl.ds(r0, rows), :] = (x1 * vv).astype(z_ref.dtype)
        x0_ref[pl.ds(r0, rows), :] = x0.astype(x0_ref.dtype)
        return carry

    lax.fori_loop(0, n_steps, body, 0)


def _hyena_pre(proj, conv_w, conv_b):
    bsz, n, _ = proj.shape
    tc = 256
    nj = HY_W // tc
    rows = min(512, n)
    base = COL_HY // tc
    kern = functools.partial(_hyena_pre_kernel, n_tok=n, rows=rows)
    u_spec = lambda part: pl.BlockSpec((1, n, tc), lambda b, j: (b, 0, base + part * nj + j))
    wp = jnp.zeros((3, 8, HY_W), F32).at[:, :3, :].set(conv_w.reshape(3, 3, HY_W).transpose(1, 0, 2))
    bp = conv_b.reshape(3, 1, HY_W)
    return pl.pallas_call(
        kern,
        grid=(bsz, nj),
        in_specs=[u_spec(0), u_spec(1), u_spec(2),
                  pl.BlockSpec((3, 8, tc), lambda b, j: (0, 0, j)),
                  pl.BlockSpec((3, 1, tc), lambda b, j: (0, 0, j))],
        out_specs=[pl.BlockSpec((n, tc), lambda b, j: (0, b * nj + j)),
                   pl.BlockSpec((n, tc), lambda b, j: (0, b * nj + j))],
        out_shape=[jax.ShapeDtypeStruct((n, bsz * HY_W), BF16)] * 2,
        compiler_params=_cparams("arbitrary", "arbitrary"),
        name="hyena_pre",
    )(proj, proj, proj, wp, bp)


def _dft_tables(n):
    kb = 64
    period = 4 * n
    t = jnp.arange(n, dtype=jnp.int32)[None, :]
    k1 = jnp.arange(n // kb, dtype=jnp.int32)[:, None]
    k0 = jnp.arange(kb, dtype=jnp.int32)[:, None]
    ang_a = ((2 * kb * k1 * t) % period).astype(F32) * (2.0 * math.pi / period)
    ang_b = (((2 * k0 + 1) * t) % period).astype(F32) * (2.0 * math.pi / period)
    ca, sa = jnp.cos(ang_a)[:, None, :], jnp.sin(ang_a)[:, None, :]
    cb, sb = jnp.cos(ang_b)[None, :, :], jnp.sin(ang_b)[None, :, :]
    cos_t = (ca * cb - sa * sb).reshape(n, n).astype(BF16)
    sin_t = (sa * cb + ca * sb).reshape(n, n).astype(BF16)
    return cos_t, sin_t


def _dft_filter_kernel(c_ref, s_ref, fs_ref, fd_ref, a_ref, b_ref, acc_a, acc_b):
    kk = pl.program_id(1)

    @pl.when(kk == 0)
    def _():
        acc_a[...] = jnp.zeros_like(acc_a)
        acc_b[...] = jnp.zeros_like(acc_b)

    acc_a[...] += _dot(c_ref[...], fs_ref[...])
    acc_b[...] += _dot(s_ref[...], fd_ref[...])

    @pl.when(kk == pl.num_programs(1) - 1)
    def _():
        a_ref[...] = acc_a[...]
        b_ref[...] = acc_b[...]


def _dft_filters(cos_t, sin_t, fs, fd, tf, tk):
    n = cos_t.shape[0]
    w = fs.shape[1]
    return pl.pallas_call(
        _dft_filter_kernel,
        grid=(n // tf, n // tk),
        in_specs=[pl.BlockSpec((tf, tk), lambda i, kk: (i, kk)),
                  pl.BlockSpec((tf, tk), lambda i, kk: (i, kk)),
                  pl.BlockSpec((tk, w), lambda i, kk: (kk, 0)),
                  pl.BlockSpec((tk, w), lambda i, kk: (kk, 0))],
        out_specs=[pl.BlockSpec((tf, w), lambda i, kk: (i, 0)),
                   pl.BlockSpec((tf, w), lambda i, kk: (i, 0))],
        out_shape=[jax.ShapeDtypeStruct((n, w), F32)] * 2,
        scratch_shapes=[pltpu.VMEM((tf, w), F32), pltpu.VMEM((tf, w), F32)],
        compiler_params=_cparams("arbitrary", "arbitrary"),
        name="dft_filters",
    )(cos_t, sin_t, fs, fd)


def _dft_fwd_kernel(c_ref, s_ref, z_ref, a_ref, b_ref, u_ref, v_ref, acc_p, acc_q):
    kk = pl.program_id(2)

    @pl.when(kk == 0)
    def _():
        acc_p[...] = jnp.zeros_like(acc_p)
        acc_q[...] = jnp.zeros_like(acc_q)

    z = z_ref[...]
    acc_p[...] += _dot(c_ref[...], z)
    acc_q[...] += _dot(s_ref[...], z)

    @pl.when(kk == pl.num_programs(2) - 1)
    def _():
        a = a_ref[...]
        b = b_ref[...]
        w = a.shape[1]
        for s in range(u_ref.shape[1] // w):
            cols = slice(s * w, (s + 1) * w)
            p = acc_p[:, cols]
            q = acc_q[:, cols]
            u_ref[:, cols] = (p * a - q * b).astype(u_ref.dtype)
            v_ref[:, cols] = (p * b + q * a).astype(v_ref.dtype)


def _dft_fwd(cos_t, sin_t, z, spec_a, spec_b, tf, tn, tk):
    n = cos_t.shape[0]
    ncol = z.shape[1]
    w = spec_a.shape[1]
    return pl.pallas_call(
        _dft_fwd_kernel,
        grid=(n // tf, ncol // tn, n // tk),
        in_specs=[pl.BlockSpec((tf, tk), lambda i, j, kk: (i, kk)),
                  pl.BlockSpec((tf, tk), lambda i, j, kk: (i, kk)),
                  pl.BlockSpec((tk, tn), lambda i, j, kk: (kk, j)),
                  pl.BlockSpec((tf, w), lambda i, j, kk: (i, 0)),
                  pl.BlockSpec((tf, w), lambda i, j, kk: (i, 0))],
        out_specs=[pl.BlockSpec((tf, tn), lambda i, j, kk: (i, j)),
                   pl.BlockSpec((tf, tn), lambda i, j, kk: (i, j))],
        out_shape=[jax.ShapeDtypeStruct((n, ncol), BF16)] * 2,
        scratch_shapes=[pltpu.VMEM((tf, tn), F32), pltpu.VMEM((tf, tn), F32)],
        compiler_params=_cparams("arbitrary", "arbitrary", "arbitrary"),
        name="dft_fwd",
    )(cos_t, sin_t, z, spec_a, spec_b)


def _dft_inv_kernel(ct_ref, st_ref, u_ref, v_ref, z_ref, x0_ref, skip_ref, o_ref, acc, *, inv_n):
    kk = pl.program_id(2)

    @pl.when(kk == 0)
    def _():
        acc[...] = jnp.zeros_like(acc)

    acc[...] += _dot(ct_ref[...], u_ref[...]) + _dot(st_ref[...], v_ref[...])

    @pl.when(kk == pl.num_programs(2) - 1)
    def _():
        y = acc[...] * inv_n + z_ref[...].astype(F32) * skip_ref[...]
        o_ref[...] = (x0_ref[...].astype(F32) * y).astype(o_ref.dtype)


def _dft_inv(cos_tt, sin_tt, u, v, z, x0, skip_t, tt, tn, tk):
    n = cos_tt.shape[0]
    ncol = u.shape[1]
    kern = functools.partial(_dft_inv_kernel, inv_n=1.0 / n)
    return pl.pallas_call(
        kern,
        grid=(n // tt, ncol // tn, n // tk),
        in_specs=[pl.BlockSpec((tt, tk), lambda i, j, kk: (i, kk)),
                  pl.BlockSpec((tt, tk), lambda i, j, kk: (i, kk)),
                  pl.BlockSpec((tk, tn), lambda i, j, kk: (kk, j)),
                  pl.BlockSpec((tk, tn), lambda i, j, kk: (kk, j)),
                  pl.BlockSpec((tt, tn), lambda i, j, kk: (i, j)),
                  pl.BlockSpec((tt, tn), lambda i, j, kk: (i, j)),
                  pl.BlockSpec((1, tn), lambda i, j, kk: (0, j))],
        out_specs=pl.BlockSpec((tt, tn), lambda i, j, kk: (i, j)),
        out_shape=jax.ShapeDtypeStruct((n, ncol), BF16),
        scratch_shapes=[pltpu.VMEM((tt, tn), F32)],
        compiler_params=_cparams("arbitrary", "arbitrary", "arbitrary"),
        name="dft_inv",
    )(cos_tt, sin_tt, u, v, z, x0, skip_t)


def _merge_kernel(ret_ref, hy_ref, gr0_ref, gr1_ref, gh0_ref, gh1_ref, x_ref, gt1_ref, sh2_ref,
                  sc2_ref, g2_ref, wro_ref, who_ref, wout_ref, rw_ref, rb_ref,
                  x1_ref, h2_ref, route_ref, cnt_ref, base_scr):
    first = jnp.logical_and(pl.program_id(0) == 0, pl.program_id(1) == 0)

    @pl.when(first)
    def _():
        base_scr[...] = jnp.zeros_like(base_scr)

    tm = x_ref.shape[1]
    gate_r = jnp.concatenate([gr0_ref[0], gr1_ref[0]], axis=1).astype(F32)
    gate_h = jnp.concatenate([gh0_ref[0], gh1_ref[0]], axis=1).astype(F32)
    mixed = _sigmoid(gate_r) * _dot(ret_ref[0], wro_ref[...].astype(BF16)) + \
        _sigmoid(gate_h) * _dot(hy_ref[...], who_ref[...].astype(BF16))
    x1 = x_ref[0] + gt1_ref[0] * _dot(mixed.astype(BF16), wout_ref[...].astype(BF16))
    x1_ref[0] = x1
    h2 = x1 * lax.rsqrt(jnp.mean(x1 * x1, axis=-1, keepdims=True) + EPS) * g2_ref[...]
    h2 = h2 * (1.0 + sc2_ref[0]) + sh2_ref[0]
    h2_ref[0] = h2

    logits = _dot3(h2, rw_ref[...]) + rb_ref[...]
    lane = lax.broadcasted_iota(jnp.int32, logits.shape, 1)
    lane_f = lane.astype(F32)
    big = float(ROUTE_LANES)

    def first_lane(mask):
        return jnp.min(jnp.where(mask, lane_f, big), axis=1, keepdims=True)

    is_group = lane < N_GROUPS
    gl = jnp.where(is_group, logits, NEG)
    ge = jnp.where(is_group, jnp.exp(gl - jnp.max(gl, axis=1, keepdims=True)), 0.0)
    group_p = ge / jnp.sum(ge, axis=1, keepdims=True)
    p_star = jnp.max(group_p, axis=1, keepdims=True)
    g_star = first_lane(jnp.logical_and(is_group, group_p == p_star))
    lo = EXPERT_LANE0 + g_star * EXPERTS_PER_GROUP
    in_group = jnp.logical_and(lane_f >= lo, lane_f < lo + EXPERTS_PER_GROUP)
    el = jnp.where(in_group, logits, NEG)
    ee = jnp.where(in_group, jnp.exp(el - jnp.max(el, axis=1, keepdims=True)), 0.0)
    sp = jnp.where(in_group, ee / jnp.sum(ee, axis=1, keepdims=True), -1.0)
    w_a = jnp.max(sp, axis=1, keepdims=True)
    l_a = first_lane(sp == w_a)
    sp2 = jnp.where(lane_f == l_a, -1.0, sp)
    w_b = jnp.max(sp2, axis=1, keepdims=True)
    l_b = first_lane(sp2 == w_b)
    wsum = w_a + w_b
    wt_a = p_star * w_a / wsum
    wt_b = p_star * w_b / wsum

    hit_a = lane_f == l_a
    hit_b = lane_f == l_b
    onehot = jnp.where(jnp.logical_or(hit_a, hit_b), 1.0, 0.0)
    ri = lax.broadcasted_iota(jnp.int32, (tm, tm), 0)
    ci = lax.broadcasted_iota(jnp.int32, (tm, tm), 1)
    tri = jnp.where(ri > ci, 1.0, 0.0).astype(BF16)
    before = _dot(tri, onehot.astype(BF16)) + base_scr[0:1, :]
    rank_a = jnp.sum(jnp.where(hit_a, before, 0.0), axis=1, keepdims=True)
    rank_b = jnp.sum(jnp.where(hit_b, before, 0.0), axis=1, keepdims=True)
    base_scr[0:1, :] = base_scr[0:1, :] + jnp.sum(onehot, axis=0, keepdims=True)

    vals = (l_a - EXPERT_LANE0, l_b - EXPERT_LANE0, wt_a, wt_b, rank_a, rank_b)
    route = jnp.zeros(logits.shape, F32)
    for idx, val in enumerate(vals):
        route = jnp.where(lane == idx, val, route)
    route_ref[0] = route
    cnt_ref[...] = base_scr[...]


def _merge(ret, hy, proj, x, gt1, sh2, sc2, g2, w_ro, w_ho, w_out, rw, rb, tm):
    bsz, n, d = x.shape
    nt = n // tm
    hw = HY_W
    gspec = lambda col: pl.BlockSpec((1, tm, hw), lambda b, i: (b, i, col // hw))
    vec = pl.BlockSpec((1, 1, d), lambda b, i: (b, 0, 0))
    full = lambda shape: pl.BlockSpec(shape, lambda b, i: (0, 0))
    tok = pl.BlockSpec((1, tm, d), lambda b, i: (b, i, 0))
    return pl.pallas_call(
        _merge_kernel,
        grid=(bsz, nt),
        in_specs=[tok,
                  pl.BlockSpec((tm, hw), lambda b, i: (i, b)),
                  gspec(COL_GR), gspec(COL_GR + hw), gspec(COL_GH), gspec(COL_GH + hw),
                  tok, vec, vec, vec, full((1, d)),
                  full((V_W, d)), full((hw, d)), full((d, d)),
                  full((d, ROUTE_LANES)), full((1, ROUTE_LANES))],
        out_specs=[tok, tok,
                   pl.BlockSpec((1, tm, ROUTE_LANES), lambda b, i: (b, i, 0)),
                   pl.BlockSpec((8, ROUTE_LANES), lambda b, i: (0, 0))],
        out_shape=[jax.ShapeDtypeStruct((bsz, n, d), F32),
                   jax.ShapeDtypeStruct((bsz, n, d), F32),
                   jax.ShapeDtypeStruct((bsz, n, ROUTE_LANES), F32),
                   jax.ShapeDtypeStruct((8, ROUTE_LANES), F32)],
        scratch_shapes=[pltpu.VMEM((8, ROUTE_LANES), F32)],
        compiler_params=_cparams("arbitrary", "arbitrary"),
        name="merge_route",
    )(ret, hy, proj, proj, proj, proj, x, gt1, sh2, sc2, g2, w_ro, w_ho, w_out, rw, rb)


def _dispatch_kernel(da_ref, db_ref, pend_ref, h_ref, xb_ref, zero_scr, sem, zsem, *, tm):
    i = pl.program_id(0)

    def row_copy(r, dst_row):
        return pltpu.make_async_copy(h_ref.at[pl.ds(r, 1)], xb_ref.at[pl.ds(dst_row, 1)], sem)

    @pl.when(i == 0)
    def _():
        zero_scr[...] = jnp.zeros_like(zero_scr)

        def zcopy(e):
            return pltpu.make_async_copy(
                zero_scr, xb_ref.at[pl.ds(pl.multiple_of(pend_ref[e] - MOE_BLK, MOE_BLK), MOE_BLK)], zsem)

        def nonempty(e):
            prev = jnp.where(e > 0, pend_ref[jnp.maximum(e - 1, 0)], 0)
            return pend_ref[e] > prev

        for e in range(N_EXPERTS):
            @pl.when(nonempty(e))
            def _():
                zcopy(e).start()
        for e in range(N_EXPERTS):
            @pl.when(nonempty(e))
            def _():
                zcopy(e).wait()

        def tail_copy(blk):
            return pltpu.make_async_copy(
                zero_scr, xb_ref.at[pl.ds(pl.multiple_of(blk * MOE_BLK, MOE_BLK), MOE_BLK)], zsem)

        first_unused = pend_ref[N_EXPERTS - 1] // MOE_BLK
        n_blocks = xb_ref.shape[0] // MOE_BLK
        lax.fori_loop(first_unused, n_blocks, lambda blk, c: (tail_copy(blk).start(), c)[1], 0)
        lax.fori_loop(first_unused, n_blocks, lambda blk, c: (tail_copy(blk).wait(), c)[1], 0)

    def issue(r, carry):
        t = i * tm + r
        row_copy(r, da_ref[t]).start()
        row_copy(r, db_ref[t]).start()
        return carry

    lax.fori_loop(0, tm, issue, 0)

    def drain(r, carry):
        row_copy(0, 0).wait()
        row_copy(0, 0).wait()
        return carry

    lax.fori_loop(0, tm, drain, 0)


def _dispatch(dest_a, dest_b, pad_end, h2, n_rows, tm):
    t_all, d = h2.shape
    kern = functools.partial(_dispatch_kernel, tm=tm)
    return pl.pallas_call(
        kern,
        grid_spec=pltpu.PrefetchScalarGridSpec(
            num_scalar_prefetch=3,
            grid=(t_all // tm,),
            in_specs=[pl.BlockSpec((tm, d), lambda i, *_: (i, 0))],
            out_specs=pl.BlockSpec(memory_space=pl.ANY),
            scratch_shapes=[pltpu.VMEM((MOE_BLK, d), F32),
                            pltpu.SemaphoreType.DMA(()), pltpu.SemaphoreType.DMA(())]),
        out_shape=jax.ShapeDtypeStruct((n_rows, d), F32),
        compiler_params=_cparams("arbitrary"),
        name="moe_dispatch",
    )(dest_a, dest_b, pad_end, h2)


def _expert_kernel(be_ref, nu_ref, x_ref, w1_ref, w3_ref, w2_ref, o_ref):
    @pl.when(pl.program_id(0) < nu_ref[0])
    def _():
        x = x_ref[...].astype(BF16)
        a = _dot(x, w1_ref[0].astype(BF16))
        b = _dot(x, w3_ref[0].astype(BF16))
        o_ref[...] = _dot((a * _sigmoid(a) * b).astype(BF16), w2_ref[0].astype(BF16))

    @pl.when(pl.program_id(0) >= nu_ref[0])
    def _():
        o_ref[...] = jnp.zeros_like(o_ref)


def _experts(block_expert, n_used, xb, w1, w3, w2):
    n_rows, d = xb.shape
    hid = w1.shape[2]
    row_blk = lambda i, be, nu: (jnp.minimum(i, nu[0] - 1), 0)
    return pl.pallas_call(
        _expert_kernel,
        grid_spec=pltpu.PrefetchScalarGridSpec(
            num_scalar_prefetch=2,
            grid=(n_rows // MOE_BLK,),
            in_specs=[pl.BlockSpec((MOE_BLK, d), row_blk),
                      pl.BlockSpec((1, d, hid), lambda i, be, nu: (be[i], 0, 0)),
                      pl.BlockSpec((1, d, hid), lambda i, be, nu: (be[i], 0, 0)),
                      pl.BlockSpec((1, hid, d), lambda i, be, nu: (be[i], 0, 0))],
            out_specs=pl.BlockSpec((MOE_BLK, d), lambda i, be, nu: (i, 0))),
        out_shape=jax.ShapeDtypeStruct((n_rows, d), F32),
        compiler_params=_cparams("arbitrary"),
        name="moe_experts",
    )(block_expert, n_used, xb, w1, w3, w2)


def _combine_kernel(da_ref, db_ref, x1_ref, route_ref, gt2_ref, gf_ref, yb_ref, o_ref,
                    buf, sems, *, tm, tiles_per_batch):
    b = pl.program_id(0)
    i = pl.program_id(1)
    step = b * tiles_per_batch + i
    n_steps = pl.num_programs(0) * tiles_per_batch
    slot = step % 2

    def row_copy(src_row, s, which, r):
        return pltpu.make_async_copy(yb_ref.at[pl.ds(src_row, 1)], buf.at[s, which, pl.ds(r, 1)],
                                     sems.at[s])

    def issue_tile(tile, s):
        def body(r, carry):
            t = tile * tm + r
            row_copy(da_ref[t], s, 0, r).start()
            row_copy(db_ref[t], s, 1, r).start()
            return carry
        lax.fori_loop(0, tm, body, 0)

    @pl.when(step == 0)
    def _():
        issue_tile(0, 0)

    @pl.when(step + 1 < n_steps)
    def _():
        issue_tile(step + 1, 1 - slot)

    def drain(r, carry):
        row_copy(0, slot, 0, 0).wait()
        row_copy(0, slot, 1, 0).wait()
        return carry

    lax.fori_loop(0, tm, drain, 0)

    route = route_ref[0]
    y = route[:, 2:3] * buf[slot, 0] + route[:, 3:4] * buf[slot, 1]
    xo = x1_ref[0] + gt2_ref[0] * y
    o_ref[0] = xo * lax.rsqrt(jnp.mean(xo * xo, axis=-1, keepdims=True) + EPS) * gf_ref[...]


def _combine(dest_a, dest_b, x1, route, gt2, gf, yb, tm):
    bsz, n, d = x1.shape
    nt = n // tm
    kern = functools.partial(_combine_kernel, tm=tm, tiles_per_batch=nt)
    return pl.pallas_call(
        kern,
        grid_spec=pltpu.PrefetchScalarGridSpec(
            num_scalar_prefetch=2,
            grid=(bsz, nt),
            in_specs=[pl.BlockSpec((1, tm, d), lambda b, i, *_: (b, i, 0)),
                      pl.BlockSpec((1, tm, ROUTE_LANES), lambda b, i, *_: (b, i, 0)),
                      pl.BlockSpec((1, 1, d), lambda b, i, *_: (b, 0, 0)),
                      pl.BlockSpec((1, d), lambda b, i, *_: (0, 0)),
                      pl.BlockSpec(memory_space=pl.ANY)],
            out_specs=pl.BlockSpec((1, tm, d), lambda b, i, *_: (b, i, 0)),
            scratch_shapes=[pltpu.VMEM((2, 2, tm, d), F32), pltpu.SemaphoreType.DMA((2,))]),
        out_shape=jax.ShapeDtypeStruct((bsz, n, d), F32),
        compiler_params=_cparams("arbitrary", "arbitrary"),
        name="moe_combine",
    )(dest_a, dest_b, x1, route, gt2, gf, yb)


def _rope_tables(n):
    rows = n // GRID_W
    r, col = jnp.meshgrid(jnp.arange(rows, dtype=F32), jnp.arange(GRID_W, dtype=F32), indexing='ij')
    n_freq = RET_QK_DIM // 4
    inv_freq = ROPE_BASE ** (-jnp.arange(n_freq, dtype=F32) / n_freq)
    ang_r = r.reshape(-1)[:, None] * inv_freq
    ang_c = col.reshape(-1)[:, None] * inv_freq
    cos_t = jnp.concatenate([jnp.cos(ang_r), jnp.cos(ang_r), jnp.cos(ang_c), jnp.cos(ang_c)], axis=-1)
    sin_t = jnp.concatenate([-jnp.sin(ang_r), jnp.sin(ang_r), -jnp.sin(ang_c), jnp.sin(ang_c)], axis=-1)
    return cos_t, sin_t


def _hyena_feats(n):
    t = jnp.arange(n, dtype=F32) / n
    bands = jnp.linspace(1e-4, HY_BANDS - 1, HY_BANDS, dtype=F32)
    phase = 2.0 * math.pi * t[:, None] * bands[None, :]
    feats = jnp.concatenate([t[:, None], jnp.cos(phase), -jnp.sin(phase)], axis=-1)
    return jnp.pad(feats, ((0, 0), (0, FFN_LANES - HY_POS_DIM)))


def _layer(x, ctx, mods, norm1_g, norm2_g, w_in, b_in, ret_decay_logit, ret_w_o, hy_conv_w,
           hy_conv_b, hy_ffn_w1, hy_ffn_b1, hy_ffn_freq, hy_ffn_w2, hy_ffn_b2, hy_ffn_w3, hy_skip,
           hy_w_o, w_out, router_group_w, router_group_b, router_expert_w, router_expert_b,
           expert_w1, expert_w3, expert_w2, final_norm_g):
    bsz, n, d = x.shape
    n_ctx = ctx.shape[1]
    mod_lat = mods[:bsz].reshape(bsz, 6, 1, d)
    sh1, sc1, gt1, sh2, sc2, gt2 = (mod_lat[:, s] for s in range(6))
    mod_ctx = mods[bsz].reshape(6, 1, 1, d)
    csh1 = jnp.broadcast_to(mod_ctx[0], (bsz, 1, d))
    csc1 = jnp.broadcast_to(mod_ctx[1], (bsz, 1, d))

    g1 = norm1_g.reshape(1, d)
    b_in2 = b_in.reshape(1, IN_W)
    proj = _inproj(x, g1, sh1, sc1, w_in, b_in2, tm=min(1024, n), tn=512)
    proj_c = _inproj(ctx, g1, csh1, csc1, w_in[:, COL_K:COL_G], b_in2[:, COL_K:COL_G],
                     tm=n_ctx, tn=512)

    cos_r, sin_r = _rope_tables(n)
    lgt = jnp.broadcast_to(ret_decay_logit.astype(F32).reshape(2 * RET_HEADS, 1),
                           (2 * RET_HEADS, RET_V_DIM))
    ret = _retention(proj, proj_c, cos_r, sin_r, lgt)

    slow = abs(math.log(HY_DECAY_TARGET)) / HY_SLOW_PCT
    fast = abs(math.log(HY_DECAY_TARGET)) / HY_FAST_PCT
    deltas = jnp.tile(jnp.linspace(slow, fast, HY_W, dtype=F32), 2).reshape(1, 2 * HY_W)
    fpad = FFN_LANES - HY_FFN
    row = lambda a: jnp.pad(a.reshape(1, HY_FFN), ((0, 0), (0, fpad)))
    fs, fd = _hyena_filters(n, _hyena_feats(n),
                            jnp.pad(hy_ffn_w1, ((0, FFN_LANES - HY_POS_DIM), (0, fpad))), row(hy_ffn_b1),
                            row(hy_ffn_freq), jnp.pad(hy_ffn_w2, ((0, fpad), (0, fpad))), row(hy_ffn_b2),
                            jnp.pad(hy_ffn_w3, ((0, fpad), (0, 0))), deltas)
    z, x0 = _hyena_pre(proj, hy_conv_w, hy_conv_b)
    cos_d, sin_d = _dft_tables(n)
    tile = min(512, n)
    spec_a, spec_b = _dft_filters(cos_d, sin_d, fs, fd, tf=tile, tk=tile)
    tn = min(2048, bsz * HY_W)
    u, v = _dft_fwd(cos_d, sin_d, z, spec_a, spec_b, tf=tile, tn=tn, tk=tile)
    skip_t = jnp.tile(hy_skip.reshape(1, HY_W), (1, bsz))
    hy = _dft_inv(cos_d.T, sin_d.T, u, v, z, x0, skip_t, tt=tile, tn=tn, tk=tile)

    rw = jnp.zeros((d, ROUTE_LANES), F32)
    rw = rw.at[:, :N_GROUPS].set(router_group_w).at[:, EXPERT_LANE0:EXPERT_LANE0 + N_EXPERTS].set(router_expert_w)
    rb = jnp.zeros((1, ROUTE_LANES), F32)
    rb = rb.at[0, :N_GROUPS].set(router_group_b).at[0, EXPERT_LANE0:EXPERT_LANE0 + N_EXPERTS].set(router_expert_b)
    x1, h2, route, cnt = _merge(ret, hy, proj, x, gt1, sh2, sc2, norm2_g.reshape(1, d),
                                ret_w_o, hy_w_o, w_out, rw, rb, tm=min(512, n))

    t_all = bsz * n
    counts = cnt[0, EXPERT_LANE0:EXPERT_LANE0 + N_EXPERTS].astype(jnp.int32)
    padded = (counts + MOE_BLK - 1) // MOE_BLK * MOE_BLK
    pad_end = jnp.cumsum(padded)
    pad_start = pad_end - padded
    route2 = route.reshape(t_all, ROUTE_LANES)
    e_ab = route2[:, 0:2].astype(jnp.int32)
    onehot = e_ab[:, :, None] == jnp.arange(N_EXPERTS, dtype=jnp.int32)[None, None, :]
    dest = jnp.sum(jnp.where(onehot, pad_start[None, None, :], 0), axis=-1) + route2[:, 4:6].astype(jnp.int32)
    dest_a, dest_b = dest[:, 0], dest[:, 1]
    n_blocks = -(-(2 * t_all + N_EXPERTS * (MOE_BLK - 1)) // MOE_BLK)
    blk0 = jnp.arange(n_blocks, dtype=jnp.int32) * MOE_BLK
    block_expert = jnp.minimum(jnp.sum(blk0[:, None] >= pad_end[None, :], axis=1), N_EXPERTS - 1).astype(jnp.int32)
    n_used = (pad_end[-1:] // MOE_BLK).astype(jnp.int32)

    xb = _dispatch(dest_a, dest_b, pad_end.astype(jnp.int32), h2.reshape(t_all, d),
                   n_blocks * MOE_BLK, tm=min(256, n))
    yb = _experts(block_expert, n_used, xb, expert_w1, expert_w3, expert_w2)
    return _combine(dest_a, dest_b, x1, route, gt2, final_norm_g.reshape(1, d), yb, tm=min(256, n))


def kernel(x, c, ctx, c_ctx, ada_w, ada_b, norm1_g, norm2_g, w_in, b_in, ret_decay_logit, ret_w_o, hy_conv_w, hy_conv_b, hy_ffn_w1, hy_ffn_b1, hy_ffn_freq, hy_ffn_w2, hy_ffn_b2, hy_ffn_w3, hy_skip, hy_w_o, w_out, router_group_w, router_group_b, router_expert_w, router_expert_b, expert_w1, expert_w3, expert_w2, final_norm_g):
    depth = ada_w.shape[0]
    assert depth == 1, "single-layer problem: the context stream is only read by the retention states"
    bsz, d = c.shape
    rows = -(-(bsz + 1) // 8) * 8
    cc = jnp.zeros((rows, d), F32).at[:bsz].set(c).at[bsz].set(c_ctx)
    mods = _adaln(cc, ada_w[0], ada_b[0].reshape(1, -1))
    return _layer(x, ctx, mods, norm1_g[0], norm2_g[0], w_in[0], b_in[0], ret_decay_logit[0],
                  ret_w_o[0], hy_conv_w[0], hy_conv_b[0], hy_ffn_w1[0], hy_ffn_b1[0], hy_ffn_freq[0],
                  hy_ffn_w2[0], hy_ffn_b2[0], hy_ffn_w3[0], hy_skip[0], hy_w_o[0], w_out[0],
                  router_group_w[0], router_group_b[0], router_expert_w[0], router_expert_b[0],
                  expert_w1[0], expert_w3[0], expert_w2[0], final_norm_g)
```

```python
import functools
import math

import jax
import jax.numpy as jnp
from jax import lax
from jax.experimental import pallas as pl
from jax.experimental.pallas import tpu as pltpu

F32 = jnp.float32
BF16 = jnp.bfloat16

D_MODEL = 1024
EPS = 1e-6
GRID_W = 64
ROPE_BASE = 10000.0

RET_HEADS = 4
RET_QK_DIM = 128
RET_V_DIM = 256
RET_CHUNK = 128
RET_UNROLL = 4
QK_W = RET_HEADS * RET_QK_DIM
V_W = RET_HEADS * RET_V_DIM

HY_W = 512
HY_POS_DIM = 33
HY_BANDS = (HY_POS_DIM - 1) // 2
HY_FFN = 64
FFN_LANES = 128
HY_DECAY_TARGET = 1e-2
HY_FAST_PCT = 0.3
HY_SLOW_PCT = 1.5

N_GROUPS = 4
EXPERTS_PER_GROUP = 8
N_EXPERTS = N_GROUPS * EXPERTS_PER_GROUP
EXPERT_HIDDEN = 512
ROUTE_LANES = 128
EXPERT_LANE0 = N_GROUPS
MOE_BLK = 256
SUBLANES = 8

IN_W = 2 * QK_W + 2 * V_W + 3 * HY_W + 2 * D_MODEL
COL_Q, COL_K, COL_V, COL_G = 0, QK_W, 2 * QK_W, 2 * QK_W + V_W
COL_HY = 2 * QK_W + 2 * V_W
COL_GR = COL_HY + 3 * HY_W
COL_GH = COL_GR + D_MODEL

VMEM_LIMIT = 56 * 1024 * 1024
NEG = -1e30


def _cparams(*sem):
    return pltpu.CompilerParams(dimension_semantics=sem, vmem_limit_bytes=VMEM_LIMIT)


def _sigmoid(x):
    return 1.0 / (1.0 + jnp.exp(-x))


def _dot(a, b):
    return jnp.dot(a, b, preferred_element_type=F32)


def _dot_t0(a, b):
    return lax.dot_general(a, b, (((0,), (0,)), ((), ())), preferred_element_type=F32)


def _dot_nt(a, b):
    return lax.dot_general(a, b, (((1,), (1,)), ((), ())), preferred_element_type=F32)


def _split_bf16(a):
    hi = a.astype(BF16)
    lo = (a - hi.astype(F32)).astype(BF16)
    return hi, lo


def _dot3(a, b):
    ah, al = _split_bf16(a)
    bh, bl = _split_bf16(b)
    return _dot(ah, bh) + _dot(al, bh) + _dot(ah, bl)


def _adaln_kernel(c_ref, w_ref, b_ref, o_ref):
    c = c_ref[...]
    o_ref[...] = _dot3(c * _sigmoid(c), w_ref[...]) + b_ref[...]


def _adaln(cc, w, b):
    rows, d = cc.shape
    n = w.shape[1]
    tn = 1536
    return pl.pallas_call(
        _adaln_kernel,
        grid=(n // tn,),
        in_specs=[pl.BlockSpec((rows, d), lambda j: (0, 0)),
                  pl.BlockSpec((d, tn), lambda j: (0, j)),
                  pl.BlockSpec((1, tn), lambda j: (0, j))],
        out_specs=pl.BlockSpec((rows, tn), lambda j: (0, j)),
        out_shape=jax.ShapeDtypeStruct((rows, n), F32),
        compiler_params=_cparams("arbitrary"),
        name="adaln",
    )(cc, w, b)


def _inproj_kernel(x_ref, g_ref, sh_ref, sc_ref, w_ref, b_ref, o_ref, *, tn):
    x = x_ref[0]
    y = x * lax.rsqrt(jnp.mean(x * x, axis=-1, keepdims=True) + EPS) * g_ref[...]
    h = (y * (1.0 + sc_ref[0]) + sh_ref[0]).astype(BF16)
    for j in range(w_ref.shape[1] // tn):
        cols = slice(j * tn, (j + 1) * tn)
        o_ref[0, :, cols] = (_dot(h, w_ref[:, cols]) + b_ref[:, cols]).astype(o_ref.dtype)


def _inproj(x, gain, shift, scale, w, b, tm, tn):
    bsz, n, d = x.shape
    nw = w.shape[1]
    return pl.pallas_call(
        functools.partial(_inproj_kernel, tn=tn),
        grid=(bsz, n // tm),
        in_specs=[pl.BlockSpec((1, tm, d), lambda bi, i: (bi, i, 0)),
                  pl.BlockSpec((1, d), lambda bi, i: (0, 0)),
                  pl.BlockSpec((1, 1, d), lambda bi, i: (bi, 0, 0)),
                  pl.BlockSpec((1, 1, d), lambda bi, i: (bi, 0, 0)),
                  pl.BlockSpec((d, nw), lambda bi, i: (0, 0)),
                  pl.BlockSpec((1, nw), lambda bi, i: (0, 0))],
        out_specs=pl.BlockSpec((1, tm, nw), lambda bi, i: (bi, i, 0)),
        out_shape=jax.ShapeDtypeStruct((bsz, n, nw), BF16),
        compiler_params=_cparams("arbitrary", "arbitrary"),
        name="inproj",
    )(x, gain, shift, scale, w, b)


def _log_sigmoid(x):
    return jnp.minimum(x, 0.0) - jnp.log(1.0 + jnp.exp(-jnp.abs(x)))


def _rope_partner(x):
    lane = lax.broadcasted_iota(jnp.int32, x.shape, 1)
    return jnp.where((lane % 64) < 32, pltpu.roll(x, 96, axis=1), pltpu.roll(x, 32, axis=1))


def _retention_kernel(q_ref, k_ref, v_ref, g_ref, kc_ref, vc_ref, cos_ref, sin_ref, lgt_ref,
                      o_ref, qr_scr, kr_scr, oacc_scr, sf_scr, sb_scr, *, n_tok, n_ctx):
    c_len = RET_CHUNK
    n_chunks = n_tok // c_len
    head = pl.program_id(1)
    k_scale = RET_QK_DIM ** -0.5

    lg_f = _log_sigmoid(lgt_ref[pl.ds(head, 1), :])
    lg_b = _log_sigmoid(lgt_ref[pl.ds(RET_HEADS + head, 1), :])
    lgf_k, lgb_k = lg_f[:, :RET_QK_DIM], lg_b[:, :RET_QK_DIM]

    ii = lax.broadcasted_iota(jnp.int32, (c_len, c_len), 0)
    jj = lax.broadcasted_iota(jnp.int32, (c_len, c_len), 1)
    dif = (ii - jj).astype(F32)
    decay_in = jnp.where(ii >= jj, jnp.exp(lgf_k * jnp.maximum(dif, 0.0)), 0.0) + \
        jnp.where(jj > ii, jnp.exp(lgb_k * jnp.maximum(-dif, 0.0)), 0.0)
    pos_k = ii.astype(F32)
    pos_v = lax.broadcasted_iota(jnp.int32, (c_len, RET_V_DIM), 0).astype(F32)
    dq_f = jnp.exp(lg_f * (pos_v + 1.0))
    dq_b = jnp.exp(lg_b * (c_len - pos_v))
    dk_f = jnp.exp(lgf_k * (c_len - 1.0 - pos_k))
    dk_b = jnp.exp(lgb_k * pos_k)
    dchunk_f = jnp.exp(lg_f * float(c_len))
    dchunk_b = jnp.exp(lg_b * float(c_len))

    pos_c = lax.broadcasted_iota(jnp.int32, (n_ctx, RET_QK_DIM), 0).astype(F32)
    kc = kc_ref[0].astype(F32) * k_scale
    vc = vc_ref[0]
    sf_scr[...] = _dot_t0((kc * jnp.exp(lgf_k * (n_ctx - 1.0 - pos_c))).astype(BF16), vc)
    sb_scr[...] = _dot_t0((kc * jnp.exp(lgb_k * pos_c)).astype(BF16), vc)

    def fwd(c, carry):
        r0 = pl.multiple_of(c * c_len, c_len)
        rows = pl.ds(r0, c_len)
        cos = cos_ref[rows, :]
        sin = sin_ref[rows, :]
        q = q_ref[0, rows, :].astype(F32)
        k = k_ref[0, rows, :].astype(F32)
        qr = q * cos + _rope_partner(q) * sin
        kr = (k * cos + _rope_partner(k) * sin) * k_scale
        qb = qr.astype(BF16)
        kb = kr.astype(BF16)
        qr_scr[rows, :] = qb
        kr_scr[rows, :] = kr
        v = v_ref[0, rows, :]
        scores = _dot_nt(qb, kb) * decay_in
        o = _dot(scores.astype(BF16), v) + _dot(qb, sf_scr[...].astype(BF16)) * dq_f
        oacc_scr[rows, :] = o
        sf_scr[...] = sf_scr[...] * dchunk_f + _dot_t0((kr * dk_f).astype(BF16), v)
        return carry

    lax.fori_loop(0, n_chunks, fwd, 0, unroll=RET_UNROLL)

    def bwd(t, carry):
        c = n_chunks - 1 - t
        r0 = pl.multiple_of(c * c_len, c_len)
        rows = pl.ds(r0, c_len)
        v = v_ref[0, rows, :]
        o = oacc_scr[rows, :] + _dot(qr_scr[rows, :], sb_scr[...].astype(BF16)) * dq_b
        o = o * lax.rsqrt(jnp.mean(o * o, axis=-1, keepdims=True) + EPS)
        g = g_ref[0, rows, :].astype(F32)
        o_ref[0, rows, :] = (g * _sigmoid(g) * o).astype(o_ref.dtype)
        sb_scr[...] = sb_scr[...] * dchunk_b + _dot_t0((kr_scr[rows, :] * dk_b).astype(BF16), v)
        return carry

    lax.fori_loop(0, n_chunks, bwd, 0, unroll=RET_UNROLL)


def _retention(proj, proj_c, cos_t, sin_t, lgt):
    bsz, n, _ = proj.shape
    n_ctx = proj_c.shape[1]
    kq, kv = RET_QK_DIM, RET_V_DIM
    kern = functools.partial(_retention_kernel, n_tok=n, n_ctx=n_ctx)
    return pl.pallas_call(
        kern,
        grid=(bsz, RET_HEADS),
        in_specs=[pl.BlockSpec((1, n, kq), lambda b, h: (b, 0, COL_Q // kq + h)),
                  pl.BlockSpec((1, n, kq), lambda b, h: (b, 0, COL_K // kq + h)),
                  pl.BlockSpec((1, n, kv), lambda b, h: (b, 0, COL_V // kv + h)),
                  pl.BlockSpec((1, n, kv), lambda b, h: (b, 0, COL_G // kv + h)),
                  pl.BlockSpec((1, n_ctx, kq), lambda b, h: (b, 0, h)),
                  pl.BlockSpec((1, n_ctx, kv), lambda b, h: (b, 0, QK_W // kv + h)),
                  pl.BlockSpec((n, kq), lambda b, h: (0, 0)),
                  pl.BlockSpec((n, kq), lambda b, h: (0, 0)),
                  pl.BlockSpec((2 * RET_HEADS, kv), lambda b, h: (0, 0))],
        out_specs=pl.BlockSpec((1, n, kv), lambda b, h: (b, 0, h)),
        out_shape=jax.ShapeDtypeStruct((bsz, n, V_W), BF16),
        scratch_shapes=[pltpu.VMEM((n, kq), BF16), pltpu.VMEM((n, kq), F32),
                        pltpu.VMEM((n, kv), F32), pltpu.VMEM((kq, kv), F32),
                        pltpu.VMEM((kq, kv), F32)],
        compiler_params=_cparams("arbitrary", "arbitrary"),
        name="retention",
    )(proj, proj, proj, proj, proj_c, proj_c, cos_t, sin_t, lgt)


def _filter_kernel(feats_ref, w1_ref, b1_ref, fr_ref, w2_ref, b2_ref, w3f_ref, w3b_ref,
                   df_ref, db_ref, fs_ref, fd_ref, *, n_tok):
    fr = fr_ref[...]
    hid = jnp.sin(fr * (_dot3(feats_ref[...], w1_ref[...]) + b1_ref[...]))
    hid = jnp.sin(fr * (_dot3(hid, w2_ref[...]) + b2_ref[...]))
    t = lax.broadcasted_iota(jnp.int32, (n_tok, df_ref.shape[1]), 0).astype(F32) / n_tok

    def one(w3_ref, d_ref):
        f = _dot3(hid, w3_ref[...]) * jnp.exp(-t * d_ref[...])
        return f / jnp.sum(jnp.abs(f), axis=0, keepdims=True)

    hf = one(w3f_ref, df_ref)
    hb = one(w3b_ref, db_ref)
    fs_ref[...] = (hf + hb).astype(fs_ref.dtype)
    fd_ref[...] = (hf - hb).astype(fd_ref.dtype)


def _hyena_filters(n, feats, w1, b1, freq, w2, b2, w3, deltas):
    tc = 256
    nf = feats.shape[1]
    kern = functools.partial(_filter_kernel, n_tok=n)
    full = lambda shape: pl.BlockSpec(shape, lambda j: (0, 0))
    return pl.pallas_call(
        kern,
        grid=(HY_W // tc,),
        in_specs=[full((n, nf)), full((nf, FFN_LANES)), full((1, FFN_LANES)), full((1, FFN_LANES)),
                  full((FFN_LANES, FFN_LANES)), full((1, FFN_LANES)),
                  pl.BlockSpec((FFN_LANES, tc), lambda j: (0, j)),
                  pl.BlockSpec((FFN_LANES, tc), lambda j: (0, HY_W // tc + j)),
                  pl.BlockSpec((1, tc), lambda j: (0, j)),
                  pl.BlockSpec((1, tc), lambda j: (0, HY_W // tc + j))],
        out_specs=[pl.BlockSpec((n, tc), lambda j: (0, j)),
                   pl.BlockSpec((n, tc), lambda j: (0, j))],
        out_shape=[jax.ShapeDtypeStruct((n, HY_W), BF16)] * 2,
        compiler_params=_cparams("arbitrary"),
        name="hyena_filters",
    )(feats, w1, b1, freq, w2, b2, w3, w3, deltas, deltas)


def _hyena_pre_kernel(u0_ref, u1_ref, u2_ref, w_ref, b_ref, z_ref, x0_ref, *, n_tok, rows):
    n_steps = n_tok // rows
    tc = z_ref.shape[1]
    halo = 16
    rid = lax.broadcasted_iota(jnp.int32, (rows, tc), 0)

    def conv(u_ref, part, r0, has_prev, has_next):
        x = u_ref[0, pl.ds(r0, rows), :].astype(F32)
        prev_g = u_ref[0, pl.ds(pl.multiple_of(jnp.maximum(r0 - halo, 0), halo), halo), :].astype(F32)
        next_g = u_ref[0, pl.ds(pl.multiple_of(jnp.minimum(r0 + rows, n_tok - halo), halo), halo), :].astype(F32)
        prev_row = jnp.where(has_prev, prev_g[halo - 1:halo, :], 0.0)
        next_row = jnp.where(has_next, next_g[0:1, :], 0.0)
        up = jnp.where(rid == 0, prev_row, pltpu.roll(x, 1, axis=0))
        dn = jnp.where(rid == rows - 1, next_row, pltpu.roll(x, rows - 1, axis=0))
        w = w_ref[part]
        return up * w[0:1, :] + x * w[1:2, :] + dn * w[2:3, :] + b_ref[part]

    def body(s, carry):
        r0 = pl.multiple_of(s * rows, rows)
        has_prev = s > 0
        has_next = s < n_steps - 1
        x0 = conv(u0_ref, 0, r0, has_prev, has_next)
        x1 = conv(u1_ref, 1, r0, has_prev, has_next)
        vv = conv(u2_ref, 2, r0, has_prev, has_next)
        z_ref[pl.ds(r0, rows), :] = (x1 * vv).astype(z_ref.dtype)
        x0_ref[pl.ds(r0, rows), :] = x0.astype(x0_ref.dtype)
        return carry

    lax.fori_loop(0, n_steps, body, 0)


def _hyena_pre(proj, conv_w, conv_b):
    bsz, n, _ = proj.shape
    tc = 256
    nj = HY_W // tc
    rows = min(512, n)
    base = COL_HY // tc
    kern = functools.partial(_hyena_pre_kernel, n_tok=n, rows=rows)
    u_spec = lambda part: pl.BlockSpec((1, n, tc), lambda b, j: (b, 0, base + part * nj + j))
    wp = jnp.zeros((3, 8, HY_W), F32).at[:, :3, :].set(conv_w.reshape(3, 3, HY_W).transpose(1, 0, 2))
    bp = conv_b.reshape(3, 1, HY_W)
    return pl.pallas_call(
        kern,
        grid=(bsz, nj),
        in_specs=[u_spec(0), u_spec(1), u_spec(2),
                  pl.BlockSpec((3, 8, tc), lambda b, j: (0, 0, j)),
                  pl.BlockSpec((3, 1, tc), lambda b, j: (0, 0, j))],
        out_specs=[pl.BlockSpec((n, tc), lambda b, j: (0, b * nj + j)),
                   pl.BlockSpec((n, tc), lambda b, j: (0, b * nj + j))],
        out_shape=[jax.ShapeDtypeStruct((n, bsz * HY_W), BF16)] * 2,
        compiler_params=_cparams("arbitrary", "arbitrary"),
        name="hyena_pre",
    )(proj, proj, proj, wp, bp)


def _dft_tables(n):
    kb = 64
    period = 4 * n
    t = jnp.arange(n, dtype=jnp.int32)[None, :]
    k1 = jnp.arange(n // kb, dtype=jnp.int32)[:, None]
    k0 = jnp.arange(kb, dtype=jnp.int32)[:, None]
    ang_a = ((2 * kb * k1 * t) % period).astype(F32) * (2.0 * math.pi / period)
    ang_b = (((2 * k0 + 1) * t) % period).astype(F32) * (2.0 * math.pi / period)
    ca, sa = jnp.cos(ang_a)[:, None, :], jnp.sin(ang_a)[:, None, :]
    cb, sb = jnp.cos(ang_b)[None, :, :], jnp.sin(ang_b)[None, :, :]
    cos_t = (ca * cb - sa * sb).reshape(n, n).astype(BF16)
    sin_t = (sa * cb + ca * sb).reshape(n, n).astype(BF16)
    return cos_t, sin_t


def _dft_filter_kernel(c_ref, s_ref, fs_ref, fd_ref, a_ref, b_ref, acc_a, acc_b):
    kk = pl.program_id(1)

    @pl.when(kk == 0)
    def _():
        acc_a[...] = jnp.zeros_like(acc_a)
        acc_b[...] = jnp.zeros_like(acc_b)

    acc_a[...] += _dot(c_ref[...], fs_ref[...])
    acc_b[...] += _dot(s_ref[...], fd_ref[...])

    @pl.when(kk == pl.num_programs(1) - 1)
    def _():
        a_ref[...] = acc_a[...]
        b_ref[...] = acc_b[...]


def _dft_filters(cos_t, sin_t, fs, fd, tf, tk):
    n = cos_t.shape[0]
    w = fs.shape[1]
    return pl.pallas_call(
        _dft_filter_kernel,
        grid=(n // tf, n // tk),
        in_specs=[pl.BlockSpec((tf, tk), lambda i, kk: (i, kk)),
                  pl.BlockSpec((tf, tk), lambda i, kk: (i, kk)),
                  pl.BlockSpec((tk, w), lambda i, kk: (kk, 0)),
                  pl.BlockSpec((tk, w), lambda i, kk: (kk, 0))],
        out_specs=[pl.BlockSpec((tf, w), lambda i, kk: (i, 0)),
                   pl.BlockSpec((tf, w), lambda i, kk: (i, 0))],
        out_shape=[jax.ShapeDtypeStruct((n, w), F32)] * 2,
        scratch_shapes=[pltpu.VMEM((tf, w), F32), pltpu.VMEM((tf, w), F32)],
        compiler_params=_cparams("arbitrary", "arbitrary"),
        name="dft_filters",
    )(cos_t, sin_t, fs, fd)


def _dft_fwd_kernel(c_ref, s_ref, z_ref, a_ref, b_ref, u_ref, v_ref, acc_p, acc_q):
    kk = pl.program_id(2)

    @pl.when(kk == 0)
    def _():
        acc_p[...] = jnp.zeros_like(acc_p)
        acc_q[...] = jnp.zeros_like(acc_q)

    z = z_ref[...]
    acc_p[...] += _dot(c_ref[...], z)
    acc_q[...] += _dot(s_ref[...], z)

    @pl.when(kk == pl.num_programs(2) - 1)
    def _():
        a = a_ref[...]
        b = b_ref[...]
        w = a.shape[1]
        for s in range(u_ref.shape[1] // w):
            cols = slice(s * w, (s + 1) * w)
            p = acc_p[:, cols]
            q = acc_q[:, cols]
            u_ref[:, cols] = (p * a - q * b).astype(u_ref.dtype)
            v_ref[:, cols] = (p * b + q * a).astype(v_ref.dtype)


def _dft_fwd(cos_t, sin_t, z, spec_a, spec_b, tf, tn, tk):
    n = cos_t.shape[0]
    ncol = z.shape[1]
    w = spec_a.shape[1]
    return pl.pallas_call(
        _dft_fwd_kernel,
        grid=(n // tf, ncol // tn, n // tk),
        in_specs=[pl.BlockSpec((tf, tk), lambda i, j, kk: (i, kk)),
                  pl.BlockSpec((tf, tk), lambda i, j, kk: (i, kk)),
                  pl.BlockSpec((tk, tn), lambda i, j, kk: (kk, j)),
                  pl.BlockSpec((tf, w), lambda i, j, kk: (i, 0)),
                  pl.BlockSpec((tf, w), lambda i, j, kk: (i, 0))],
        out_specs=[pl.BlockSpec((tf, tn), lambda i, j, kk: (i, j)),
                   pl.BlockSpec((tf, tn), lambda i, j, kk: (i, j))],
        out_shape=[jax.ShapeDtypeStruct((n, ncol), BF16)] * 2,
        scratch_shapes=[pltpu.VMEM((tf, tn), F32), pltpu.VMEM((tf, tn), F32)],
        compiler_params=_cparams("arbitrary", "arbitrary", "arbitrary"),
        name="dft_fwd",
    )(cos_t, sin_t, z, spec_a, spec_b)


def _dft_inv_kernel(ct_ref, st_ref, u_ref, v_ref, z_ref, x0_ref, skip_ref, o_ref, acc, *, inv_n):
    kk = pl.program_id(2)

    @pl.when(kk == 0)
    def _():
        acc[...] = jnp.zeros_like(acc)

    acc[...] += _dot(ct_ref[...], u_ref[...]) + _dot(st_ref[...], v_ref[...])

    @pl.when(kk == pl.num_programs(2) - 1)
    def _():
        y = acc[...] * inv_n + z_ref[...].astype(F32) * skip_ref[...]
        o_ref[...] = (x0_ref[...].astype(F32) * y).astype(o_ref.dtype)


def _dft_inv(cos_tt, sin_tt, u, v, z, x0, skip_t, tt, tn, tk):
    n = cos_tt.shape[0]
    ncol = u.shape[1]
    kern = functools.partial(_dft_inv_kernel, inv_n=1.0 / n)
    return pl.pallas_call(
        kern,
        grid=(n // tt, ncol // tn, n // tk),
        in_specs=[pl.BlockSpec((tt, tk), lambda i, j, kk: (i, kk)),
                  pl.BlockSpec((tt, tk), lambda i, j, kk: (i, kk)),
                  pl.BlockSpec((tk, tn), lambda i, j, kk: (kk, j)),
                  pl.BlockSpec((tk, tn), lambda i, j, kk: (kk, j)),
                  pl.BlockSpec((tt, tn), lambda i, j, kk: (i, j)),
                  pl.BlockSpec((tt, tn), lambda i, j, kk: (i, j)),
                  pl.BlockSpec((1, tn), lambda i, j, kk: (0, j))],
        out_specs=pl.BlockSpec((tt, tn), lambda i, j, kk: (i, j)),
        out_shape=jax.ShapeDtypeStruct((n, ncol), BF16),
        scratch_shapes=[pltpu.VMEM((tt, tn), F32)],
        compiler_params=_cparams("arbitrary", "arbitrary", "arbitrary"),
        name="dft_inv",
    )(cos_tt, sin_tt, u, v, z, x0, skip_t)


def _merge_kernel(ret_ref, hy_ref, gr0_ref, gr1_ref, gh0_ref, gh1_ref, x_ref, gt1_ref, sh2_ref,
                  sc2_ref, g2_ref, wro_ref, who_ref, wout_ref, rw_ref, rb_ref,
                  x1_ref, h2_ref, route_ref, cnt_ref, base_scr):
    first = jnp.logical_and(pl.program_id(0) == 0, pl.program_id(1) == 0)

    @pl.when(first)
    def _():
        base_scr[...] = jnp.zeros_like(base_scr)

    tm = x_ref.shape[1]
    gate_r = jnp.concatenate([gr0_ref[0], gr1_ref[0]], axis=1).astype(F32)
    gate_h = jnp.concatenate([gh0_ref[0], gh1_ref[0]], axis=1).astype(F32)
    mixed = _sigmoid(gate_r) * _dot(ret_ref[0], wro_ref[...].astype(BF16)) + \
        _sigmoid(gate_h) * _dot(hy_ref[...], who_ref[...].astype(BF16))
    x1 = x_ref[0] + gt1_ref[0] * _dot(mixed.astype(BF16), wout_ref[...].astype(BF16))
    x1_ref[0] = x1
    h2 = x1 * lax.rsqrt(jnp.mean(x1 * x1, axis=-1, keepdims=True) + EPS) * g2_ref[...]
    h2 = h2 * (1.0 + sc2_ref[0]) + sh2_ref[0]
    h2_ref[0] = h2

    logits = _dot3(h2, rw_ref[...]) + rb_ref[...]
    lane = lax.broadcasted_iota(jnp.int32, logits.shape, 1)
    lane_f = lane.astype(F32)
    big = float(ROUTE_LANES)

    def first_lane(mask):
        return jnp.min(jnp.where(mask, lane_f, big), axis=1, keepdims=True)

    is_group = lane < N_GROUPS
    gl = jnp.where(is_group, logits, NEG)
    ge = jnp.where(is_group, jnp.exp(gl - jnp.max(gl, axis=1, keepdims=True)), 0.0)
    group_p = ge / jnp.sum(ge, axis=1, keepdims=True)
    p_star = jnp.max(group_p, axis=1, keepdims=True)
    g_star = first_lane(jnp.logical_and(is_group, group_p == p_star))
    lo = EXPERT_LANE0 + g_star * EXPERTS_PER_GROUP
    in_group = jnp.logical_and(lane_f >= lo, lane_f < lo + EXPERTS_PER_GROUP)
    el = jnp.where(in_group, logits, NEG)
    ee = jnp.where(in_group, jnp.exp(el - jnp.max(el, axis=1, keepdims=True)), 0.0)
    sp = jnp.where(in_group, ee / jnp.sum(ee, axis=1, keepdims=True), -1.0)
    w_a = jnp.max(sp, axis=1, keepdims=True)
    l_a = first_lane(sp == w_a)
    sp2 = jnp.where(lane_f == l_a, -1.0, sp)
    w_b = jnp.max(sp2, axis=1, keepdims=True)
    l_b = first_lane(sp2 == w_b)
    wsum = w_a + w_b
    wt_a = p_star * w_a / wsum
    wt_b = p_star * w_b / wsum

    hit_a = lane_f == l_a
    hit_b = lane_f == l_b
    onehot = jnp.where(jnp.logical_or(hit_a, hit_b), 1.0, 0.0)
    ri = lax.broadcasted_iota(jnp.int32, (tm, tm), 0)
    ci = lax.broadcasted_iota(jnp.int32, (tm, tm), 1)
    tri = jnp.where(ri > ci, 1.0, 0.0).astype(BF16)
    before = _dot(tri, onehot.astype(BF16)) + base_scr[0:1, :]
    rank_a = jnp.sum(jnp.where(hit_a, before, 0.0), axis=1, keepdims=True)
    rank_b = jnp.sum(jnp.where(hit_b, before, 0.0), axis=1, keepdims=True)
    base_scr[0:1, :] = base_scr[0:1, :] + jnp.sum(onehot, axis=0, keepdims=True)

    vals = (l_a - EXPERT_LANE0, l_b - EXPERT_LANE0, wt_a, wt_b, rank_a, rank_b)
    route = jnp.zeros(logits.shape, F32)
    for idx, val in enumerate(vals):
        route = jnp.where(lane == idx, val, route)
    route_ref[0] = route
    cnt_ref[...] = base_scr[...]


def _merge(ret, hy, proj, x, gt1, sh2, sc2, g2, w_ro, w_ho, w_out, rw, rb, tm):
    bsz, n, d = x.shape
    nt = n // tm
    hw = HY_W
    gspec = lambda col: pl.BlockSpec((1, tm, hw), lambda b, i: (b, i, col // hw))
    vec = pl.BlockSpec((1, 1, d), lambda b, i: (b, 0, 0))
    full = lambda shape: pl.BlockSpec(shape, lambda b, i: (0, 0))
    tok = pl.BlockSpec((1, tm, d), lambda b, i: (b, i, 0))
    return pl.pallas_call(
        _merge_kernel,
        grid=(bsz, nt),
        in_specs=[tok,
                  pl.BlockSpec((tm, hw), lambda b, i: (i, b)),
                  gspec(COL_GR), gspec(COL_GR + hw), gspec(COL_GH), gspec(COL_GH + hw),
                  tok, vec, vec, vec, full((1, d)),
                  full((V_W, d)), full((hw, d)), full((d, d)),
                  full((d, ROUTE_LANES)), full((1, ROUTE_LANES))],
        out_specs=[tok, tok,
                   pl.BlockSpec((1, tm, ROUTE_LANES), lambda b, i: (b, i, 0)),
                   pl.BlockSpec((8, ROUTE_LANES), lambda b, i: (0, 0))],
        out_shape=[jax.ShapeDtypeStruct((bsz, n, d), F32),
                   jax.ShapeDtypeStruct((bsz, n, d), F32),
                   jax.ShapeDtypeStruct((bsz, n, ROUTE_LANES), F32),
                   jax.ShapeDtypeStruct((8, ROUTE_LANES), F32)],
        scratch_shapes=[pltpu.VMEM((8, ROUTE_LANES), F32)],
        compiler_params=_cparams("arbitrary", "arbitrary"),
        name="merge_route",
    )(ret, hy, proj, proj, proj, proj, x, gt1, sh2, sc2, g2, w_ro, w_ho, w_out, rw, rb)


def _dispatch_kernel(da_ref, db_ref, pend_ref, h_ref, xb_ref, zero_scr, sem, zsem, *, tm):
    i = pl.program_id(0)

    def row_copy(g, u, dst_row):
        return pltpu.make_async_copy(h_ref.at[g, pl.ds(u, 1)], xb_ref.at[pl.ds(dst_row, 1)], sem)

    @pl.when(i == 0)
    def _():
        zero_scr[...] = jnp.zeros_like(zero_scr)

        def zcopy(e):
            return pltpu.make_async_copy(
                zero_scr, xb_ref.at[pl.ds(pl.multiple_of(pend_ref[e] - MOE_BLK, MOE_BLK), MOE_BLK)], zsem)

        def nonempty(e):
            prev = jnp.where(e > 0, pend_ref[jnp.maximum(e - 1, 0)], 0)
            return pend_ref[e] > prev

        for e in range(N_EXPERTS):
            @pl.when(nonempty(e))
            def _():
                zcopy(e).start()
        for e in range(N_EXPERTS):
            @pl.when(nonempty(e))
            def _():
                zcopy(e).wait()

        def tail_copy(blk):
            return pltpu.make_async_copy(
                zero_scr, xb_ref.at[pl.ds(pl.multiple_of(blk * MOE_BLK, MOE_BLK), MOE_BLK)], zsem)

        first_unused = pend_ref[N_EXPERTS - 1] // MOE_BLK
        n_blocks = xb_ref.shape[0] // MOE_BLK
        lax.fori_loop(first_unused, n_blocks, lambda blk, c: (tail_copy(blk).start(), c)[1], 0)
        lax.fori_loop(first_unused, n_blocks, lambda blk, c: (tail_copy(blk).wait(), c)[1], 0)

    def issue(g, carry):
        t0 = i * tm + g * SUBLANES
        for u in range(SUBLANES):
            row_copy(g, u, da_ref[t0 + u]).start(priority=0)
            row_copy(g, u, db_ref[t0 + u]).start(priority=1)
        return carry

    lax.fori_loop(0, tm // SUBLANES, issue, 0)

    def drain(g, carry):
        for _ in range(2 * SUBLANES):
            row_copy(0, 0, 0).wait()
        return carry

    lax.fori_loop(0, tm // SUBLANES, drain, 0)


def _dispatch(dest_a, dest_b, pad_end, h2, n_rows, tm):
    t_all, d = h2.shape
    kern = functools.partial(_dispatch_kernel, tm=tm)
    return pl.pallas_call(
        kern,
        grid_spec=pltpu.PrefetchScalarGridSpec(
            num_scalar_prefetch=3,
            grid=(t_all // tm,),
            in_specs=[pl.BlockSpec((tm // SUBLANES, SUBLANES, d), lambda i, *_: (i, 0, 0))],
            out_specs=pl.BlockSpec(memory_space=pl.ANY),
            scratch_shapes=[pltpu.VMEM((MOE_BLK, d), F32),
                            pltpu.SemaphoreType.DMA(()), pltpu.SemaphoreType.DMA(())]),
        out_shape=jax.ShapeDtypeStruct((n_rows, d), F32),
        compiler_params=_cparams("arbitrary"),
        name="moe_dispatch",
    )(dest_a, dest_b, pad_end, h2.reshape(t_all // SUBLANES, SUBLANES, d))


def _expert_kernel(be_ref, nu_ref, x_ref, w1_ref, w3_ref, w2_ref, o_ref):
    @pl.when(pl.program_id(0) < nu_ref[0])
    def _():
        x = x_ref[...].astype(BF16)
        a = _dot(x, w1_ref[0].astype(BF16))
        b = _dot(x, w3_ref[0].astype(BF16))
        o_ref[...] = _dot((a * _sigmoid(a) * b).astype(BF16), w2_ref[0].astype(BF16))

    @pl.when(pl.program_id(0) >= nu_ref[0])
    def _():
        o_ref[...] = jnp.zeros_like(o_ref)


def _experts(block_expert, n_used, xb, w1, w3, w2):
    n_rows, d = xb.shape
    hid = w1.shape[2]
    row_blk = lambda i, be, nu: (jnp.minimum(i, nu[0] - 1), 0)
    return pl.pallas_call(
        _expert_kernel,
        grid_spec=pltpu.PrefetchScalarGridSpec(
            num_scalar_prefetch=2,
            grid=(n_rows // MOE_BLK,),
            in_specs=[pl.BlockSpec((MOE_BLK, d), row_blk),
                      pl.BlockSpec((1, d, hid), lambda i, be, nu: (be[i], 0, 0)),
                      pl.BlockSpec((1, d, hid), lambda i, be, nu: (be[i], 0, 0)),
                      pl.BlockSpec((1, hid, d), lambda i, be, nu: (be[i], 0, 0))],
            out_specs=pl.BlockSpec((MOE_BLK, d), lambda i, be, nu: (i, 0))),
        out_shape=jax.ShapeDtypeStruct((n_rows, d), F32),
        compiler_params=_cparams("arbitrary"),
        name="moe_experts",
    )(block_expert, n_used, xb, w1, w3, w2)


def _combine_kernel(da_ref, db_ref, x1_ref, route_ref, gt2_ref, gf_ref, yb_ref, o_ref,
                    buf, sems, *, tm, tiles_per_batch):
    b = pl.program_id(0)
    i = pl.program_id(1)
    step = b * tiles_per_batch + i
    n_steps = pl.num_programs(0) * tiles_per_batch
    slot = step % 2

    def row_copy(src_row, s, which, g, u):
        return pltpu.make_async_copy(yb_ref.at[pl.ds(src_row, 1)], buf.at[s, which, g, pl.ds(u, 1)],
                                     sems.at[s])

    def issue_tile(tile, s):
        def body(g, carry):
            t0 = tile * tm + g * SUBLANES
            for u in range(SUBLANES):
                row_copy(da_ref[t0 + u], s, 0, g, u).start(priority=0)
                row_copy(db_ref[t0 + u], s, 1, g, u).start(priority=1)
            return carry
        lax.fori_loop(0, tm // SUBLANES, body, 0)

    @pl.when(step == 0)
    def _():
        issue_tile(0, 0)

    @pl.when(step + 1 < n_steps)
    def _():
        issue_tile(step + 1, 1 - slot)

    def drain(g, carry):
        for _ in range(SUBLANES):
            row_copy(0, slot, 0, 0, 0).wait()
            row_copy(0, slot, 1, 0, 0).wait()
        return carry

    lax.fori_loop(0, tm // SUBLANES, drain, 0)

    route = route_ref[0]
    d = o_ref.shape[2]
    y = route[:, 2:3] * buf[slot, 0].reshape(tm, d) + route[:, 3:4] * buf[slot, 1].reshape(tm, d)
    xo = x1_ref[0] + gt2_ref[0] * y
    o_ref[0] = xo * lax.rsqrt(jnp.mean(xo * xo, axis=-1, keepdims=True) + EPS) * gf_ref[...]


def _combine(dest_a, dest_b, x1, route, gt2, gf, yb, tm):
    bsz, n, d = x1.shape
    nt = n // tm
    kern = functools.partial(_combine_kernel, tm=tm, tiles_per_batch=nt)
    return pl.pallas_call(
        kern,
        grid_spec=pltpu.PrefetchScalarGridSpec(
            num_scalar_prefetch=2,
            grid=(bsz, nt),
            in_specs=[pl.BlockSpec((1, tm, d), lambda b, i, *_: (b, i, 0)),
                      pl.BlockSpec((1, tm, ROUTE_LANES), lambda b, i, *_: (b, i, 0)),
                      pl.BlockSpec((1, 1, d), lambda b, i, *_: (b, 0, 0)),
                      pl.BlockSpec((1, d), lambda b, i, *_: (0, 0)),
                      pl.BlockSpec(memory_space=pl.ANY)],
            out_specs=pl.BlockSpec((1, tm, d), lambda b, i, *_: (b, i, 0)),
            scratch_shapes=[pltpu.VMEM((2, 2, tm // SUBLANES, SUBLANES, d), F32),
                            pltpu.SemaphoreType.DMA((2,))]),
        out_shape=jax.ShapeDtypeStruct((bsz, n, d), F32),
        compiler_params=_cparams("arbitrary", "arbitrary"),
        name="moe_combine",
    )(dest_a, dest_b, x1, route, gt2, gf, yb)


def _rope_tables(n):
    rows = n // GRID_W
    r, col = jnp.meshgrid(jnp.arange(rows, dtype=F32), jnp.arange(GRID_W, dtype=F32), indexing='ij')
    n_freq = RET_QK_DIM // 4
    inv_freq = ROPE_BASE ** (-jnp.arange(n_freq, dtype=F32) / n_freq)
    ang_r = r.reshape(-1)[:, None] * inv_freq
    ang_c = col.reshape(-1)[:, None] * inv_freq
    cos_t = jnp.concatenate([jnp.cos(ang_r), jnp.cos(ang_r), jnp.cos(ang_c), jnp.cos(ang_c)], axis=-1)
    sin_t = jnp.concatenate([-jnp.sin(ang_r), jnp.sin(ang_r), -jnp.sin(ang_c), jnp.sin(ang_c)], axis=-1)
    return cos_t, sin_t


def _hyena_feats(n):
    t = jnp.arange(n, dtype=F32) / n
    bands = jnp.linspace(1e-4, HY_BANDS - 1, HY_BANDS, dtype=F32)
    phase = 2.0 * math.pi * t[:, None] * bands[None, :]
    feats = jnp.concatenate([t[:, None], jnp.cos(phase), -jnp.sin(phase)], axis=-1)
    return jnp.pad(feats, ((0, 0), (0, FFN_LANES - HY_POS_DIM)))


def _layer(x, ctx, mods, norm1_g, norm2_g, w_in, b_in, ret_decay_logit, ret_w_o, hy_conv_w,
           hy_conv_b, hy_ffn_w1, hy_ffn_b1, hy_ffn_freq, hy_ffn_w2, hy_ffn_b2, hy_ffn_w3, hy_skip,
           hy_w_o, w_out, router_group_w, router_group_b, router_expert_w, router_expert_b,
           expert_w1, expert_w3, expert_w2, final_norm_g):
    bsz, n, d = x.shape
    n_ctx = ctx.shape[1]
    mod_lat = mods[:bsz].reshape(bsz, 6, 1, d)
    sh1, sc1, gt1, sh2, sc2, gt2 = (mod_lat[:, s] for s in range(6))
    mod_ctx = mods[bsz].reshape(6, 1, 1, d)
    csh1 = jnp.broadcast_to(mod_ctx[0], (bsz, 1, d))
    csc1 = jnp.broadcast_to(mod_ctx[1], (bsz, 1, d))

    g1 = norm1_g.reshape(1, d)
    b_in2 = b_in.reshape(1, IN_W)
    w_in_b = w_in.astype(BF16)
    proj = _inproj(x, g1, sh1, sc1, w_in_b, b_in2, tm=min(512, n), tn=512)
    proj_c = _inproj(ctx, g1, csh1, csc1, w_in_b[:, COL_K:COL_G], b_in2[:, COL_K:COL_G],
                     tm=n_ctx, tn=512)

    cos_r, sin_r = _rope_tables(n)
    lgt = jnp.broadcast_to(ret_decay_logit.astype(F32).reshape(2 * RET_HEADS, 1),
                           (2 * RET_HEADS, RET_V_DIM))
    ret = _retention(proj, proj_c, cos_r, sin_r, lgt)

    slow = abs(math.log(HY_DECAY_TARGET)) / HY_SLOW_PCT
    fast = abs(math.log(HY_DECAY_TARGET)) / HY_FAST_PCT
    deltas = jnp.tile(jnp.linspace(slow, fast, HY_W, dtype=F32), 2).reshape(1, 2 * HY_W)
    fpad = FFN_LANES - HY_FFN
    row = lambda a: jnp.pad(a.reshape(1, HY_FFN), ((0, 0), (0, fpad)))
    fs, fd = _hyena_filters(n, _hyena_feats(n),
                            jnp.pad(hy_ffn_w1, ((0, FFN_LANES - HY_POS_DIM), (0, fpad))), row(hy_ffn_b1),
                            row(hy_ffn_freq), jnp.pad(hy_ffn_w2, ((0, fpad), (0, fpad))), row(hy_ffn_b2),
                            jnp.pad(hy_ffn_w3, ((0, fpad), (0, 0))), deltas)
    z, x0 = _hyena_pre(proj, hy_conv_w, hy_conv_b)
    cos_d, sin_d = _dft_tables(n)
    tile = min(512, n)
    spec_a, spec_b = _dft_filters(cos_d, sin_d, fs, fd, tf=tile, tk=tile)
    tn = min(2048, bsz * HY_W)
    u, v = _dft_fwd(cos_d, sin_d, z, spec_a, spec_b, tf=tile, tn=tn, tk=tile)
    skip_t = jnp.tile(hy_skip.reshape(1, HY_W), (1, bsz))
    hy = _dft_inv(cos_d.T, sin_d.T, u, v, z, x0, skip_t, tt=tile, tn=tn, tk=tile)

    rw = jnp.zeros((d, ROUTE_LANES), F32)
    rw = rw.at[:, :N_GROUPS].set(router_group_w).at[:, EXPERT_LANE0:EXPERT_LANE0 + N_EXPERTS].set(router_expert_w)
    rb = jnp.zeros((1, ROUTE_LANES), F32)
    rb = rb.at[0, :N_GROUPS].set(router_group_b).at[0, EXPERT_LANE0:EXPERT_LANE0 + N_EXPERTS].set(router_expert_b)
    x1, h2, route, cnt = _merge(ret, hy, proj, x, gt1, sh2, sc2, norm2_g.reshape(1, d),
                                ret_w_o, hy_w_o, w_out, rw, rb, tm=min(512, n))

    t_all = bsz * n
    counts = cnt[0, EXPERT_LANE0:EXPERT_LANE0 + N_EXPERTS].astype(jnp.int32)
    padded = (counts + MOE_BLK - 1) // MOE_BLK * MOE_BLK
    pad_end = jnp.cumsum(padded)
    pad_start = pad_end - padded
    route2 = route.reshape(t_all, ROUTE_LANES)
    e_ab = route2[:, 0:2].astype(jnp.int32)
    onehot = e_ab[:, :, None] == jnp.arange(N_EXPERTS, dtype=jnp.int32)[None, None, :]
    dest = jnp.sum(jnp.where(onehot, pad_start[None, None, :], 0), axis=-1) + route2[:, 4:6].astype(jnp.int32)
    dest_a, dest_b = dest[:, 0], dest[:, 1]
    n_blocks = -(-(2 * t_all + N_EXPERTS * (MOE_BLK - 1)) // MOE_BLK)
    blk0 = jnp.arange(n_blocks, dtype=jnp.int32) * MOE_BLK
    block_expert = jnp.minimum(jnp.sum(blk0[:, None] >= pad_end[None, :], axis=1), N_EXPERTS - 1).astype(jnp.int32)
    n_used = (pad_end[-1:] // MOE_BLK).astype(jnp.int32)

    xb = _dispatch(dest_a, dest_b, pad_end.astype(jnp.int32), h2.reshape(t_all, d),
                   n_blocks * MOE_BLK, tm=min(256, n))
    yb = _experts(block_expert, n_used, xb, expert_w1, expert_w3, expert_w2)
    return _combine(dest_a, dest_b, x1, route, gt2, final_norm_g.reshape(1, d), yb, tm=min(256, n))


def kernel(x, c, ctx, c_ctx, ada_w, ada_b, norm1_g, norm2_g, w_in, b_in, ret_decay_logit, ret_w_o, hy_conv_w, hy_conv_b, hy_ffn_w1, hy_ffn_b1, hy_ffn_freq, hy_ffn_w2, hy_ffn_b2, hy_ffn_w3, hy_skip, hy_w_o, w_out, router_group_w, router_group_b, router_expert_w, router_expert_b, expert_w1, expert_w3, expert_w2, final_norm_g):
    depth = ada_w.shape[0]
    assert depth == 1, "single-layer problem: the context stream is only read by the retention states"
    bsz, d = c.shape
    rows = -(-(bsz + 1) // 8) * 8
    cc = jnp.zeros((rows, d), F32).at[:bsz].set(c).at[bsz].set(c_ctx)
    mods = _adaln(cc, ada_w[0], ada_b[0].reshape(1, -1))
    return _layer(x, ctx, mods, norm1_g[0], norm2_g[0], w_in[0], b_in[0], ret_decay_logit[0],
                  ret_w_o[0], hy_conv_w[0], hy_conv_b[0], hy_ffn_w1[0], hy_ffn_b1[0], hy_ffn_freq[0],
                  hy_ffn_w2[0], hy_ffn_b2[0], hy_ffn_w3[0], hy_skip[0], hy_w_o[0], w_out[0],
                  router_group_w[0], router_group_b[0], router_expert_w[0], router_expert_b[0],
                  expert_w1[0], expert_w3[0], expert_w2[0], final_norm_g)
```

```python
import functools
import math

import jax
import jax.numpy as jnp
from jax import lax
from jax.experimental import pallas as pl
from jax.experimental.pallas import tpu as pltpu

F32 = jnp.float32
BF16 = jnp.bfloat16

D_MODEL = 1024
EPS = 1e-6
GRID_W = 64
ROPE_BASE = 10000.0

RET_HEADS = 4
RET_QK_DIM = 128
RET_V_DIM = 256
RET_CHUNK = 256
RET_UNROLL = 2
QK_W = RET_HEADS * RET_QK_DIM
V_W = RET_HEADS * RET_V_DIM

HY_W = 512
HY_POS_DIM = 33
HY_BANDS = (HY_POS_DIM - 1) // 2
HY_FFN = 64
FFN_LANES = 128
HY_DECAY_TARGET = 1e-2
HY_FAST_PCT = 0.3
HY_SLOW_PCT = 1.5

N_GROUPS = 4
EXPERTS_PER_GROUP = 8
N_EXPERTS = N_GROUPS * EXPERTS_PER_GROUP
EXPERT_HIDDEN = 512
ROUTE_LANES = 128
EXPERT_LANE0 = N_GROUPS
MOE_BLK = 512
SUBLANES = 8

IN_W = 2 * QK_W + 2 * V_W + 3 * HY_W + 2 * D_MODEL
COL_Q, COL_K, COL_V, COL_G = 0, QK_W, 2 * QK_W, 2 * QK_W + V_W
COL_HY = 2 * QK_W + 2 * V_W
COL_GR = COL_HY + 3 * HY_W
COL_GH = COL_GR + D_MODEL

VMEM_LIMIT = 56 * 1024 * 1024
NEG = -1e30


def _cparams(*sem):
    return pltpu.CompilerParams(dimension_semantics=sem, vmem_limit_bytes=VMEM_LIMIT)


def _sigmoid(x):
    return 1.0 / (1.0 + jnp.exp(-x))


def _dot(a, b):
    return jnp.dot(a, b, preferred_element_type=F32)


def _dot_t0(a, b):
    return lax.dot_general(a, b, (((0,), (0,)), ((), ())), preferred_element_type=F32)


def _dot_nt(a, b):
    return lax.dot_general(a, b, (((1,), (1,)), ((), ())), preferred_element_type=F32)


def _split_bf16(a):
    hi = a.astype(BF16)
    lo = (a - hi.astype(F32)).astype(BF16)
    return hi, lo


def _dot3(a, b):
    ah, al = _split_bf16(a)
    bh, bl = _split_bf16(b)
    return _dot(ah, bh) + _dot(al, bh) + _dot(ah, bl)


def _adaln_kernel(c_ref, w_ref, b_ref, o_ref):
    c = c_ref[...]
    o_ref[...] = _dot3(c * _sigmoid(c), w_ref[...]) + b_ref[...]


def _adaln(cc, w, b):
    rows, d = cc.shape
    n = w.shape[1]
    tn = 1536
    return pl.pallas_call(
        _adaln_kernel,
        grid=(n // tn,),
        in_specs=[pl.BlockSpec((rows, d), lambda j: (0, 0)),
                  pl.BlockSpec((d, tn), lambda j: (0, j)),
                  pl.BlockSpec((1, tn), lambda j: (0, j))],
        out_specs=pl.BlockSpec((rows, tn), lambda j: (0, j)),
        out_shape=jax.ShapeDtypeStruct((rows, n), F32),
        compiler_params=_cparams("arbitrary"),
        name="adaln",
    )(cc, w, b)


def _inproj_kernel(x_ref, g_ref, sh_ref, sc_ref, w_ref, b_ref, o_ref, *, tn):
    x = x_ref[0]
    y = x * lax.rsqrt(jnp.mean(x * x, axis=-1, keepdims=True) + EPS) * g_ref[...]
    h = (y * (1.0 + sc_ref[0]) + sh_ref[0]).astype(BF16)
    for j in range(w_ref.shape[1] // tn):
        cols = slice(j * tn, (j + 1) * tn)
        o_ref[0, :, cols] = (_dot(h, w_ref[:, cols]) + b_ref[:, cols]).astype(o_ref.dtype)


def _inproj(x, gain, shift, scale, w, b, tm, tn):
    bsz, n, d = x.shape
    nw = w.shape[1]
    return pl.pallas_call(
        functools.partial(_inproj_kernel, tn=tn),
        grid=(bsz, n // tm),
        in_specs=[pl.BlockSpec((1, tm, d), lambda bi, i: (bi, i, 0)),
                  pl.BlockSpec((1, d), lambda bi, i: (0, 0)),
                  pl.BlockSpec((1, 1, d), lambda bi, i: (bi, 0, 0)),
                  pl.BlockSpec((1, 1, d), lambda bi, i: (bi, 0, 0)),
                  pl.BlockSpec((d, nw), lambda bi, i: (0, 0)),
                  pl.BlockSpec((1, nw), lambda bi, i: (0, 0))],
        out_specs=pl.BlockSpec((1, tm, nw), lambda bi, i: (bi, i, 0)),
        out_shape=jax.ShapeDtypeStruct((bsz, n, nw), BF16),
        compiler_params=_cparams("arbitrary", "arbitrary"),
        name="inproj",
    )(x, gain, shift, scale, w, b)


def _log_sigmoid(x):
    return jnp.minimum(x, 0.0) - jnp.log(1.0 + jnp.exp(-jnp.abs(x)))


def _rope_partner(x):
    lane = lax.broadcasted_iota(jnp.int32, x.shape, 1)
    return jnp.where((lane % 64) < 32, pltpu.roll(x, 96, axis=1), pltpu.roll(x, 32, axis=1))


def _retention_kernel(q_ref, k_ref, v_ref, g_ref, kc_ref, vc_ref, cos_ref, sin_ref, lgt_ref,
                      o_ref, qr_scr, kr_scr, oacc_scr, sf_scr, sb_scr, *, n_tok, n_ctx):
    c_len = RET_CHUNK
    n_chunks = n_tok // c_len
    head = pl.program_id(1)
    k_scale = RET_QK_DIM ** -0.5

    lg_f = _log_sigmoid(lgt_ref[pl.ds(head, 1), :])
    lg_b = _log_sigmoid(lgt_ref[pl.ds(RET_HEADS + head, 1), :])
    lgf_k, lgb_k = lg_f[:, :RET_QK_DIM], lg_b[:, :RET_QK_DIM]

    ii = lax.broadcasted_iota(jnp.int32, (c_len, c_len), 0)
    jj = lax.broadcasted_iota(jnp.int32, (c_len, c_len), 1)
    dif = (ii - jj).astype(F32)
    decay_in = jnp.where(ii >= jj, jnp.exp(lg_f[:, :c_len] * jnp.maximum(dif, 0.0)), 0.0) + \
        jnp.where(jj > ii, jnp.exp(lg_b[:, :c_len] * jnp.maximum(-dif, 0.0)), 0.0)
    pos_k = lax.broadcasted_iota(jnp.int32, (c_len, RET_QK_DIM), 0).astype(F32)
    pos_v = lax.broadcasted_iota(jnp.int32, (c_len, RET_V_DIM), 0).astype(F32)
    dq_f = jnp.exp(lg_f * (pos_v + 1.0))
    dq_b = jnp.exp(lg_b * (c_len - pos_v))
    dk_f = jnp.exp(lgf_k * (c_len - 1.0 - pos_k))
    dk_b = jnp.exp(lgb_k * pos_k)
    dchunk_f = jnp.exp(lg_f * float(c_len))
    dchunk_b = jnp.exp(lg_b * float(c_len))

    pos_c = lax.broadcasted_iota(jnp.int32, (n_ctx, RET_QK_DIM), 0).astype(F32)
    kc = kc_ref[0].astype(F32) * k_scale
    vc = vc_ref[0]
    sf_scr[...] = _dot_t0((kc * jnp.exp(lgf_k * (n_ctx - 1.0 - pos_c))).astype(BF16), vc)
    sb_scr[...] = _dot_t0((kc * jnp.exp(lgb_k * pos_c)).astype(BF16), vc)

    def fwd(c, carry):
        r0 = pl.multiple_of(c * c_len, c_len)
        rows = pl.ds(r0, c_len)
        cos = cos_ref[rows, :]
        sin = sin_ref[rows, :]
        q = q_ref[0, rows, :].astype(F32)
        k = k_ref[0, rows, :].astype(F32)
        qr = q * cos + _rope_partner(q) * sin
        kr = (k * cos + _rope_partner(k) * sin) * k_scale
        qb = qr.astype(BF16)
        kb = kr.astype(BF16)
        qr_scr[rows, :] = qb
        kr_scr[rows, :] = kr
        v = v_ref[0, rows, :]
        scores = _dot_nt(qb, kb) * decay_in
        o = _dot(scores.astype(BF16), v) + _dot(qb, sf_scr[...].astype(BF16)) * dq_f
        oacc_scr[rows, :] = o
        sf_scr[...] = sf_scr[...] * dchunk_f + _dot_t0((kr * dk_f).astype(BF16), v)
        return carry

    lax.fori_loop(0, n_chunks, fwd, 0, unroll=RET_UNROLL)

    def bwd(t, carry):
        c = n_chunks - 1 - t
        r0 = pl.multiple_of(c * c_len, c_len)
        rows = pl.ds(r0, c_len)
        v = v_ref[0, rows, :]
        o = oacc_scr[rows, :] + _dot(qr_scr[rows, :], sb_scr[...].astype(BF16)) * dq_b
        o = o * lax.rsqrt(jnp.mean(o * o, axis=-1, keepdims=True) + EPS)
        g = g_ref[0, rows, :].astype(F32)
        o_ref[0, rows, :] = (g * _sigmoid(g) * o).astype(o_ref.dtype)
        sb_scr[...] = sb_scr[...] * dchunk_b + _dot_t0((kr_scr[rows, :] * dk_b).astype(BF16), v)
        return carry

    lax.fori_loop(0, n_chunks, bwd, 0, unroll=RET_UNROLL)


def _retention(proj, proj_c, cos_t, sin_t, lgt):
    bsz, n, _ = proj.shape
    n_ctx = proj_c.shape[1]
    kq, kv = RET_QK_DIM, RET_V_DIM
    assert n % RET_CHUNK == 0 and RET_CHUNK <= kv
    kern = functools.partial(_retention_kernel, n_tok=n, n_ctx=n_ctx)
    return pl.pallas_call(
        kern,
        grid=(bsz, RET_HEADS),
        in_specs=[pl.BlockSpec((1, n, kq), lambda b, h: (b, 0, COL_Q // kq + h)),
                  pl.BlockSpec((1, n, kq), lambda b, h: (b, 0, COL_K // kq + h)),
                  pl.BlockSpec((1, n, kv), lambda b, h: (b, 0, COL_V // kv + h)),
                  pl.BlockSpec((1, n, kv), lambda b, h: (b, 0, COL_G // kv + h)),
                  pl.BlockSpec((1, n_ctx, kq), lambda b, h: (b, 0, h)),
                  pl.BlockSpec((1, n_ctx, kv), lambda b, h: (b, 0, QK_W // kv + h)),
                  pl.BlockSpec((n, kq), lambda b, h: (0, 0)),
                  pl.BlockSpec((n, kq), lambda b, h: (0, 0)),
                  pl.BlockSpec((2 * RET_HEADS, kv), lambda b, h: (0, 0))],
        out_specs=pl.BlockSpec((1, n, kv), lambda b, h: (b, 0, h)),
        out_shape=jax.ShapeDtypeStruct((bsz, n, V_W), BF16),
        scratch_shapes=[pltpu.VMEM((n, kq), BF16), pltpu.VMEM((n, kq), F32),
                        pltpu.VMEM((n, kv), F32), pltpu.VMEM((kq, kv), F32),
                        pltpu.VMEM((kq, kv), F32)],
        compiler_params=_cparams("arbitrary", "arbitrary"),
        name="retention",
    )(proj, proj, proj, proj, proj_c, proj_c, cos_t, sin_t, lgt)


def _filter_kernel(feats_ref, w1_ref, b1_ref, fr_ref, w2_ref, b2_ref, w3f_ref, w3b_ref,
                   df_ref, db_ref, fs_ref, fd_ref, *, n_tok):
    fr = fr_ref[...]
    hid = jnp.sin(fr * (_dot3(feats_ref[...], w1_ref[...]) + b1_ref[...]))
    hid = jnp.sin(fr * (_dot3(hid, w2_ref[...]) + b2_ref[...]))
    t = lax.broadcasted_iota(jnp.int32, (n_tok, df_ref.shape[1]), 0).astype(F32) / n_tok

    def one(w3_ref, d_ref):
        f = _dot3(hid, w3_ref[...]) * jnp.exp(-t * d_ref[...])
        return f / jnp.sum(jnp.abs(f), axis=0, keepdims=True)

    hf = one(w3f_ref, df_ref)
    hb = one(w3b_ref, db_ref)
    fs_ref[...] = (hf + hb).astype(fs_ref.dtype)
    fd_ref[...] = (hf - hb).astype(fd_ref.dtype)


def _hyena_filters(n, feats, w1, b1, freq, w2, b2, w3, deltas):
    tc = HY_W
    nf = feats.shape[1]
    kern = functools.partial(_filter_kernel, n_tok=n)
    full = lambda shape: pl.BlockSpec(shape, lambda j: (0, 0))
    return pl.pallas_call(
        kern,
        grid=(HY_W // tc,),
        in_specs=[full((n, nf)), full((nf, FFN_LANES)), full((1, FFN_LANES)), full((1, FFN_LANES)),
                  full((FFN_LANES, FFN_LANES)), full((1, FFN_LANES)),
                  pl.BlockSpec((FFN_LANES, tc), lambda j: (0, j)),
                  pl.BlockSpec((FFN_LANES, tc), lambda j: (0, HY_W // tc + j)),
                  pl.BlockSpec((1, tc), lambda j: (0, j)),
                  pl.BlockSpec((1, tc), lambda j: (0, HY_W // tc + j))],
        out_specs=[pl.BlockSpec((n, tc), lambda j: (0, j)),
                   pl.BlockSpec((n, tc), lambda j: (0, j))],
        out_shape=[jax.ShapeDtypeStruct((n, HY_W), BF16)] * 2,
        compiler_params=_cparams("arbitrary"),
        name="hyena_filters",
    )(feats, w1, b1, freq, w2, b2, w3, w3, deltas, deltas)


def _hyena_pre_kernel(u0_ref, u1_ref, u2_ref, w_ref, b_ref, z_ref, x0_ref, *, n_tok, rows):
    n_steps = n_tok // rows
    tc = z_ref.shape[1]
    halo = 16
    rid = lax.broadcasted_iota(jnp.int32, (rows, tc), 0)

    def conv(u_ref, part, r0, has_prev, has_next):
        x = u_ref[0, pl.ds(r0, rows), :].astype(F32)
        prev_g = u_ref[0, pl.ds(pl.multiple_of(jnp.maximum(r0 - halo, 0), halo), halo), :].astype(F32)
        next_g = u_ref[0, pl.ds(pl.multiple_of(jnp.minimum(r0 + rows, n_tok - halo), halo), halo), :].astype(F32)
        prev_row = jnp.where(has_prev, prev_g[halo - 1:halo, :], 0.0)
        next_row = jnp.where(has_next, next_g[0:1, :], 0.0)
        up = jnp.where(rid == 0, prev_row, pltpu.roll(x, 1, axis=0))
        dn = jnp.where(rid == rows - 1, next_row, pltpu.roll(x, rows - 1, axis=0))
        w = w_ref[part]
        return up * w[0:1, :] + x * w[1:2, :] + dn * w[2:3, :] + b_ref[part]

    def body(s, carry):
        r0 = pl.multiple_of(s * rows, rows)
        has_prev = s > 0
        has_next = s < n_steps - 1
        x0 = conv(u0_ref, 0, r0, has_prev, has_next)
        x1 = conv(u1_ref, 1, r0, has_prev, has_next)
        vv = conv(u2_ref, 2, r0, has_prev, has_next)
        z_ref[pl.ds(r0, rows), :] = (x1 * vv).astype(z_ref.dtype)
        x0_ref[pl.ds(r0, rows), :] = x0.astype(x0_ref.dtype)
        return carry

    lax.fori_loop(0, n_steps, body, 0)


def _hyena_pre(proj, conv_w, conv_b):
    bsz, n, _ = proj.shape
    tc = 256
    nj = HY_W // tc
    rows = min(512, n)
    base = COL_HY // tc
    kern = functools.partial(_hyena_pre_kernel, n_tok=n, rows=rows)
    u_spec = lambda part: pl.BlockSpec((1, n, tc), lambda b, j: (b, 0, base + part * nj + j))
    wp = jnp.zeros((3, 8, HY_W), F32).at[:, :3, :].set(conv_w.reshape(3, 3, HY_W).transpose(1, 0, 2))
    bp = conv_b.reshape(3, 1, HY_W)
    return pl.pallas_call(
        kern,
        grid=(bsz, nj),
        in_specs=[u_spec(0), u_spec(1), u_spec(2),
                  pl.BlockSpec((3, 8, tc), lambda b, j: (0, 0, j)),
                  pl.BlockSpec((3, 1, tc), lambda b, j: (0, 0, j))],
        out_specs=[pl.BlockSpec((n, tc), lambda b, j: (0, b * nj + j)),
                   pl.BlockSpec((n, tc), lambda b, j: (0, b * nj + j))],
        out_shape=[jax.ShapeDtypeStruct((n, bsz * HY_W), BF16)] * 2,
        compiler_params=_cparams("arbitrary", "arbitrary"),
        name="hyena_pre",
    )(proj, proj, proj, wp, bp)


def _dft_tables(n):
    kb = 64
    period = 4 * n
    t = jnp.arange(n, dtype=jnp.int32)[None, :]
    k1 = jnp.arange(n // kb, dtype=jnp.int32)[:, None]
    k0 = jnp.arange(kb, dtype=jnp.int32)[:, None]
    ang_a = ((2 * kb * k1 * t) % period).astype(F32) * (2.0 * math.pi / period)
    ang_b = (((2 * k0 + 1) * t) % period).astype(F32) * (2.0 * math.pi / period)
    ca, sa = jnp.cos(ang_a)[:, None, :], jnp.sin(ang_a)[:, None, :]
    cb, sb = jnp.cos(ang_b)[None, :, :], jnp.sin(ang_b)[None, :, :]
    cos_t = (ca * cb - sa * sb).reshape(n, n).astype(BF16)
    sin_t = (sa * cb + ca * sb).reshape(n, n).astype(BF16)
    return cos_t, sin_t


def _dft_filter_kernel(c_ref, s_ref, fs_ref, fd_ref, a_ref, b_ref, acc_a, acc_b):
    kk = pl.program_id(1)

    @pl.when(kk == 0)
    def _():
        acc_a[...] = jnp.zeros_like(acc_a)
        acc_b[...] = jnp.zeros_like(acc_b)

    acc_a[...] += _dot(c_ref[...], fs_ref[...])
    acc_b[...] += _dot(s_ref[...], fd_ref[...])

    @pl.when(kk == pl.num_programs(1) - 1)
    def _():
        a_ref[...] = acc_a[...]
        b_ref[...] = acc_b[...]


def _dft_filters(cos_t, sin_t, fs, fd, tf, tk):
    n = cos_t.shape[0]
    w = fs.shape[1]
    return pl.pallas_call(
        _dft_filter_kernel,
        grid=(n // tf, n // tk),
        in_specs=[pl.BlockSpec((tf, tk), lambda i, kk: (i, kk)),
                  pl.BlockSpec((tf, tk), lambda i, kk: (i, kk)),
                  pl.BlockSpec((tk, w), lambda i, kk: (kk, 0)),
                  pl.BlockSpec((tk, w), lambda i, kk: (kk, 0))],
        out_specs=[pl.BlockSpec((tf, w), lambda i, kk: (i, 0)),
                   pl.BlockSpec((tf, w), lambda i, kk: (i, 0))],
        out_shape=[jax.ShapeDtypeStruct((n, w), F32)] * 2,
        scratch_shapes=[pltpu.VMEM((tf, w), F32), pltpu.VMEM((tf, w), F32)],
        compiler_params=_cparams("arbitrary", "arbitrary"),
        name="dft_filters",
    )(cos_t, sin_t, fs, fd)


def _dft_fwd_kernel(c_ref, s_ref, z_ref, a_ref, b_ref, u_ref, v_ref, acc_p, acc_q):
    kk = pl.program_id(2)

    @pl.when(kk == 0)
    def _():
        acc_p[...] = jnp.zeros_like(acc_p)
        acc_q[...] = jnp.zeros_like(acc_q)

    z = z_ref[...]
    acc_p[...] += _dot(c_ref[...], z)
    acc_q[...] += _dot(s_ref[...], z)

    @pl.when(kk == pl.num_programs(2) - 1)
    def _():
        a = a_ref[...]
        b = b_ref[...]
        w = a.shape[1]
        for s in range(u_ref.shape[1] // w):
            cols = slice(s * w, (s + 1) * w)
            p = acc_p[:, cols]
            q = acc_q[:, cols]
            u_ref[:, cols] = (p * a - q * b).astype(u_ref.dtype)
            v_ref[:, cols] = (p * b + q * a).astype(v_ref.dtype)


def _dft_fwd(cos_t, sin_t, z, spec_a, spec_b, tf, tn, tk):
    n = cos_t.shape[0]
    ncol = z.shape[1]
    w = spec_a.shape[1]
    return pl.pallas_call(
        _dft_fwd_kernel,
        grid=(n // tf, ncol // tn, n // tk),
        in_specs=[pl.BlockSpec((tf, tk), lambda i, j, kk: (i, kk)),
                  pl.BlockSpec((tf, tk), lambda i, j, kk: (i, kk)),
                  pl.BlockSpec((tk, tn), lambda i, j, kk: (kk, j)),
                  pl.BlockSpec((tf, w), lambda i, j, kk: (i, 0)),
                  pl.BlockSpec((tf, w), lambda i, j, kk: (i, 0))],
        out_specs=[pl.BlockSpec((tf, tn), lambda i, j, kk: (i, j)),
                   pl.BlockSpec((tf, tn), lambda i, j, kk: (i, j))],
        out_shape=[jax.ShapeDtypeStruct((n, ncol), BF16)] * 2,
        scratch_shapes=[pltpu.VMEM((tf, tn), F32), pltpu.VMEM((tf, tn), F32)],
        compiler_params=_cparams("arbitrary", "arbitrary", "arbitrary"),
        name="dft_fwd",
    )(cos_t, sin_t, z, spec_a, spec_b)


def _dft_inv_kernel(ct_ref, st_ref, u_ref, v_ref, z_ref, x0_ref, skip_ref, o_ref, acc, *, inv_n):
    kk = pl.program_id(2)

    @pl.when(kk == 0)
    def _():
        acc[...] = jnp.zeros_like(acc)

    acc[...] += _dot(ct_ref[...], u_ref[...]) + _dot(st_ref[...], v_ref[...])

    @pl.when(kk == pl.num_programs(2) - 1)
    def _():
        y = acc[...] * inv_n + z_ref[...].astype(F32) * skip_ref[...]
        o_ref[...] = (x0_ref[...].astype(F32) * y).astype(o_ref.dtype)


def _dft_inv(cos_tt, sin_tt, u, v, z, x0, skip_t, tt, tn, tk):
    n = cos_tt.shape[0]
    ncol = u.shape[1]
    kern = functools.partial(_dft_inv_kernel, inv_n=1.0 / n)
    return pl.pallas_call(
        kern,
        grid=(n // tt, ncol // tn, n // tk),
        in_specs=[pl.BlockSpec((tt, tk), lambda i, j, kk: (i, kk)),
                  pl.BlockSpec((tt, tk), lambda i, j, kk: (i, kk)),
                  pl.BlockSpec((tk, tn), lambda i, j, kk: (kk, j)),
                  pl.BlockSpec((tk, tn), lambda i, j, kk: (kk, j)),
                  pl.BlockSpec((tt, tn), lambda i, j, kk: (i, j)),
                  pl.BlockSpec((tt, tn), lambda i, j, kk: (i, j)),
                  pl.BlockSpec((1, tn), lambda i, j, kk: (0, j))],
        out_specs=pl.BlockSpec((tt, tn), lambda i, j, kk: (i, j)),
        out_shape=jax.ShapeDtypeStruct((n, ncol), BF16),
        scratch_shapes=[pltpu.VMEM((tt, tn), F32)],
        compiler_params=_cparams("arbitrary", "arbitrary", "arbitrary"),
        name="dft_inv",
    )(cos_tt, sin_tt, u, v, z, x0, skip_t)


def _merge_kernel(ret_ref, hy_ref, gr0_ref, gr1_ref, gh0_ref, gh1_ref, x_ref, gt1_ref, sh2_ref,
                  sc2_ref, g2_ref, wro_ref, who_ref, wout_ref, rwh_ref, rwl_ref, rb_ref,
                  x1_ref, h2_ref, route_ref, cnt_ref, base_scr, *, sub):
    first = jnp.logical_and(pl.program_id(0) == 0, pl.program_id(1) == 0)

    @pl.when(first)
    def _():
        base_scr[...] = jnp.zeros_like(base_scr)

    for s0 in range(0, x_ref.shape[1], sub):
        _merge_rows(slice(s0, s0 + sub), ret_ref, hy_ref, gr0_ref, gr1_ref, gh0_ref, gh1_ref, x_ref,
                    gt1_ref, sh2_ref, sc2_ref, g2_ref, wro_ref, who_ref, wout_ref, rwh_ref, rwl_ref,
                    rb_ref, x1_ref, h2_ref, route_ref, base_scr)
    cnt_ref[...] = base_scr[...]


def _merge_rows(rows, ret_ref, hy_ref, gr0_ref, gr1_ref, gh0_ref, gh1_ref, x_ref, gt1_ref, sh2_ref,
                sc2_ref, g2_ref, wro_ref, who_ref, wout_ref, rwh_ref, rwl_ref, rb_ref,
                x1_ref, h2_ref, route_ref, base_scr):
    tm = rows.stop - rows.start
    gate_r = jnp.concatenate([gr0_ref[0, rows, :], gr1_ref[0, rows, :]], axis=1).astype(F32)
    gate_h = jnp.concatenate([gh0_ref[0, rows, :], gh1_ref[0, rows, :]], axis=1).astype(F32)
    mixed = _sigmoid(gate_r) * _dot(ret_ref[0, rows, :], wro_ref[...]) + \
        _sigmoid(gate_h) * _dot(hy_ref[rows, :], who_ref[...])
    x1 = x_ref[0, rows, :] + gt1_ref[0] * _dot(mixed.astype(BF16), wout_ref[...])
    x1_ref[0, rows, :] = x1
    h2 = x1 * lax.rsqrt(jnp.mean(x1 * x1, axis=-1, keepdims=True) + EPS) * g2_ref[...]
    h2 = h2 * (1.0 + sc2_ref[0]) + sh2_ref[0]
    h2_ref[0, rows, :] = h2

    h2_hi, h2_lo = _split_bf16(h2)
    rw_hi = rwh_ref[...]
    logits = _dot(h2_hi, rw_hi) + _dot(h2_lo, rw_hi) + _dot(h2_hi, rwl_ref[...]) + rb_ref[...]
    lane = lax.broadcasted_iota(jnp.int32, logits.shape, 1)
    lane_f = lane.astype(F32)
    big = float(ROUTE_LANES)

    def first_lane(mask):
        return jnp.min(jnp.where(mask, lane_f, big), axis=1, keepdims=True)

    is_group = lane < N_GROUPS
    gl = jnp.where(is_group, logits, NEG)
    ge = jnp.where(is_group, jnp.exp(gl - jnp.max(gl, axis=1, keepdims=True)), 0.0)
    group_p = ge / jnp.sum(ge, axis=1, keepdims=True)
    p_star = jnp.max(group_p, axis=1, keepdims=True)
    g_star = first_lane(jnp.logical_and(is_group, group_p == p_star))
    lo = EXPERT_LANE0 + g_star * EXPERTS_PER_GROUP
    in_group = jnp.logical_and(lane_f >= lo, lane_f < lo + EXPERTS_PER_GROUP)
    el = jnp.where(in_group, logits, NEG)
    ee = jnp.where(in_group, jnp.exp(el - jnp.max(el, axis=1, keepdims=True)), 0.0)
    sp = jnp.where(in_group, ee / jnp.sum(ee, axis=1, keepdims=True), -1.0)
    w_a = jnp.max(sp, axis=1, keepdims=True)
    l_a = first_lane(sp == w_a)
    sp2 = jnp.where(lane_f == l_a, -1.0, sp)
    w_b = jnp.max(sp2, axis=1, keepdims=True)
    l_b = first_lane(sp2 == w_b)
    wsum = w_a + w_b
    wt_a = p_star * w_a / wsum
    wt_b = p_star * w_b / wsum

    hit_a = lane_f == l_a
    hit_b = lane_f == l_b
    onehot = jnp.where(jnp.logical_or(hit_a, hit_b), 1.0, 0.0)
    ri = lax.broadcasted_iota(jnp.int32, (tm, tm), 0)
    ci = lax.broadcasted_iota(jnp.int32, (tm, tm), 1)
    tri = jnp.where(ri > ci, 1.0, 0.0).astype(BF16)
    before = _dot(tri, onehot.astype(BF16)) + base_scr[0:1, :]
    rank_a = jnp.sum(jnp.where(hit_a, before, 0.0), axis=1, keepdims=True)
    rank_b = jnp.sum(jnp.where(hit_b, before, 0.0), axis=1, keepdims=True)
    base_scr[0:1, :] = base_scr[0:1, :] + jnp.sum(onehot, axis=0, keepdims=True)

    vals = (l_a - EXPERT_LANE0, l_b - EXPERT_LANE0, wt_a, wt_b, rank_a, rank_b)
    route = jnp.zeros(logits.shape, F32)
    for idx, val in enumerate(vals):
        route = jnp.where(lane == idx, val, route)
    route_ref[0, rows, :] = route


def _merge(ret, hy, proj, x, gt1, sh2, sc2, g2, w_ro, w_ho, w_out, rw_hi, rw_lo, rb, tm, sub):
    bsz, n, d = x.shape
    nt = n // tm
    hw = HY_W
    gspec = lambda col: pl.BlockSpec((1, tm, hw), lambda b, i: (b, i, col // hw))
    vec = pl.BlockSpec((1, 1, d), lambda b, i: (b, 0, 0))
    full = lambda shape: pl.BlockSpec(shape, lambda b, i: (0, 0))
    tok = pl.BlockSpec((1, tm, d), lambda b, i: (b, i, 0))
    return pl.pallas_call(
        functools.partial(_merge_kernel, sub=sub),
        grid=(bsz, nt),
        in_specs=[tok,
                  pl.BlockSpec((tm, hw), lambda b, i: (i, b)),
                  gspec(COL_GR), gspec(COL_GR + hw), gspec(COL_GH), gspec(COL_GH + hw),
                  tok, vec, vec, vec, full((1, d)),
                  full((V_W, d)), full((hw, d)), full((d, d)),
                  full((d, ROUTE_LANES)), full((d, ROUTE_LANES)), full((1, ROUTE_LANES))],
        out_specs=[tok, tok,
                   pl.BlockSpec((1, tm, ROUTE_LANES), lambda b, i: (b, i, 0)),
                   pl.BlockSpec((8, ROUTE_LANES), lambda b, i: (0, 0))],
        out_shape=[jax.ShapeDtypeStruct((bsz, n, d), F32),
                   jax.ShapeDtypeStruct((bsz, n, d), F32),
                   jax.ShapeDtypeStruct((bsz, n, ROUTE_LANES), F32),
                   jax.ShapeDtypeStruct((8, ROUTE_LANES), F32)],
        scratch_shapes=[pltpu.VMEM((8, ROUTE_LANES), F32)],
        compiler_params=_cparams("arbitrary", "arbitrary"),
        name="merge_route",
    )(ret, hy, proj, proj, proj, proj, x, gt1, sh2, sc2, g2, w_ro, w_ho, w_out, rw_hi, rw_lo, rb)


def _dispatch_kernel(da_ref, db_ref, pend_ref, h_ref, xb_ref, zero_scr, sem, zsem, *, tm):
    i = pl.program_id(0)

    def row_copy(g, u, dst_row):
        return pltpu.make_async_copy(h_ref.at[g, pl.ds(u, 1)], xb_ref.at[pl.ds(dst_row, 1)], sem)

    @pl.when(i == 0)
    def _():
        zero_scr[...] = jnp.zeros_like(zero_scr)

        def zcopy(e):
            return pltpu.make_async_copy(
                zero_scr, xb_ref.at[pl.ds(pl.multiple_of(pend_ref[e] - MOE_BLK, MOE_BLK), MOE_BLK)], zsem)

        def nonempty(e):
            prev = jnp.where(e > 0, pend_ref[jnp.maximum(e - 1, 0)], 0)
            return pend_ref[e] > prev

        for e in range(N_EXPERTS):
            @pl.when(nonempty(e))
            def _():
                zcopy(e).start()
        for e in range(N_EXPERTS):
            @pl.when(nonempty(e))
            def _():
                zcopy(e).wait()

        def tail_copy(blk):
            return pltpu.make_async_copy(
                zero_scr, xb_ref.at[pl.ds(pl.multiple_of(blk * MOE_BLK, MOE_BLK), MOE_BLK)], zsem)

        first_unused = pend_ref[N_EXPERTS - 1] // MOE_BLK
        n_blocks = xb_ref.shape[0] // MOE_BLK
        lax.fori_loop(first_unused, n_blocks, lambda blk, c: (tail_copy(blk).start(), c)[1], 0)
        lax.fori_loop(first_unused, n_blocks, lambda blk, c: (tail_copy(blk).wait(), c)[1], 0)

    def issue(g, carry):
        t0 = i * tm + g * SUBLANES
        for u in range(SUBLANES):
            row_copy(g, u, da_ref[t0 + u]).start(priority=0)
            row_copy(g, u, db_ref[t0 + u]).start(priority=1)
        return carry

    lax.fori_loop(0, tm // SUBLANES, issue, 0)

    def drain(g, carry):
        for _ in range(2 * SUBLANES):
            row_copy(0, 0, 0).wait()
        return carry

    lax.fori_loop(0, tm // SUBLANES, drain, 0)


def _dispatch(dest_a, dest_b, pad_end, h2, n_rows, tm):
    t_all, d = h2.shape
    kern = functools.partial(_dispatch_kernel, tm=tm)
    return pl.pallas_call(
        kern,
        grid_spec=pltpu.PrefetchScalarGridSpec(
            num_scalar_prefetch=3,
            grid=(t_all // tm,),
            in_specs=[pl.BlockSpec((tm // SUBLANES, SUBLANES, d), lambda i, *_: (i, 0, 0))],
            out_specs=pl.BlockSpec(memory_space=pl.ANY),
            scratch_shapes=[pltpu.VMEM((MOE_BLK, d), F32),
                            pltpu.SemaphoreType.DMA(()), pltpu.SemaphoreType.DMA(())]),
        out_shape=jax.ShapeDtypeStruct((n_rows, d), F32),
        compiler_params=_cparams("arbitrary"),
        name="moe_dispatch",
    )(dest_a, dest_b, pad_end, h2.reshape(t_all // SUBLANES, SUBLANES, d))


def _expert_kernel(be_ref, nu_ref, x_ref, w1_ref, w3_ref, w2_ref, o_ref, w1_scr, w3_scr, w2_scr):
    i = pl.program_id(0)

    @pl.when(jnp.logical_or(i == 0, be_ref[i] != be_ref[jnp.maximum(i - 1, 0)]))
    def _():
        w1_scr[...] = w1_ref[0].astype(BF16)
        w3_scr[...] = w3_ref[0].astype(BF16)
        w2_scr[...] = w2_ref[0].astype(BF16)

    @pl.when(i < nu_ref[0])
    def _():
        x = x_ref[...].astype(BF16)
        a = _dot(x, w1_scr[...])
        b = _dot(x, w3_scr[...])
        o_ref[...] = _dot((a * _sigmoid(a) * b).astype(BF16), w2_scr[...])

    @pl.when(i >= nu_ref[0])
    def _():
        o_ref[...] = jnp.zeros_like(o_ref)


def _experts(block_expert, n_used, xb, w1, w3, w2):
    n_rows, d = xb.shape
    hid = w1.shape[2]
    row_blk = lambda i, be, nu: (jnp.minimum(i, nu[0] - 1), 0)
    return pl.pallas_call(
        _expert_kernel,
        grid_spec=pltpu.PrefetchScalarGridSpec(
            num_scalar_prefetch=2,
            grid=(n_rows // MOE_BLK,),
            in_specs=[pl.BlockSpec((MOE_BLK, d), row_blk),
                      pl.BlockSpec((1, d, hid), lambda i, be, nu: (be[i], 0, 0)),
                      pl.BlockSpec((1, d, hid), lambda i, be, nu: (be[i], 0, 0)),
                      pl.BlockSpec((1, hid, d), lambda i, be, nu: (be[i], 0, 0))],
            out_specs=pl.BlockSpec((MOE_BLK, d), lambda i, be, nu: (i, 0)),
            scratch_shapes=[pltpu.VMEM((d, hid), BF16), pltpu.VMEM((d, hid), BF16),
                            pltpu.VMEM((hid, d), BF16)]),
        out_shape=jax.ShapeDtypeStruct((n_rows, d), F32),
        compiler_params=_cparams("arbitrary"),
        name="moe_experts",
    )(block_expert, n_used, xb, w1, w3, w2)


def _combine_kernel(da_ref, db_ref, x1_ref, route_ref, gt2_ref, gf_ref, yb_ref, o_ref,
                    buf, sems, *, tm, tiles_per_batch):
    b = pl.program_id(0)
    i = pl.program_id(1)
    step = b * tiles_per_batch + i
    n_steps = pl.num_programs(0) * tiles_per_batch
    slot = step % 2

    def row_copy(src_row, s, which, g, u):
        return pltpu.make_async_copy(yb_ref.at[pl.ds(src_row, 1)], buf.at[s, which, g, pl.ds(u, 1)],
                                     sems.at[s])

    def issue_tile(tile, s):
        def body(g, carry):
            t0 = tile * tm + g * SUBLANES
            for u in range(SUBLANES):
                row_copy(da_ref[t0 + u], s, 0, g, u).start(priority=0)
                row_copy(db_ref[t0 + u], s, 1, g, u).start(priority=1)
            return carry
        lax.fori_loop(0, tm // SUBLANES, body, 0)

    @pl.when(step == 0)
    def _():
        issue_tile(0, 0)

    @pl.when(step + 1 < n_steps)
    def _():
        issue_tile(step + 1, 1 - slot)

    def drain(g, carry):
        for _ in range(SUBLANES):
            row_copy(0, slot, 0, 0, 0).wait()
            row_copy(0, slot, 1, 0, 0).wait()
        return carry

    lax.fori_loop(0, tm // SUBLANES, drain, 0)

    route = route_ref[0]
    d = o_ref.shape[2]
    y = route[:, 2:3] * buf[slot, 0].reshape(tm, d) + route[:, 3:4] * buf[slot, 1].reshape(tm, d)
    xo = x1_ref[0] + gt2_ref[0] * y
    o_ref[0] = xo * lax.rsqrt(jnp.mean(xo * xo, axis=-1, keepdims=True) + EPS) * gf_ref[...]


def _combine(dest_a, dest_b, x1, route, gt2, gf, yb, tm):
    bsz, n, d = x1.shape
    nt = n // tm
    kern = functools.partial(_combine_kernel, tm=tm, tiles_per_batch=nt)
    return pl.pallas_call(
        kern,
        grid_spec=pltpu.PrefetchScalarGridSpec(
            num_scalar_prefetch=2,
            grid=(bsz, nt),
            in_specs=[pl.BlockSpec((1, tm, d), lambda b, i, *_: (b, i, 0)),
                      pl.BlockSpec((1, tm, ROUTE_LANES), lambda b, i, *_: (b, i, 0)),
                      pl.BlockSpec((1, 1, d), lambda b, i, *_: (b, 0, 0)),
                      pl.BlockSpec((1, d), lambda b, i, *_: (0, 0)),
                      pl.BlockSpec(memory_space=pl.ANY)],
            out_specs=pl.BlockSpec((1, tm, d), lambda b, i, *_: (b, i, 0)),
            scratch_shapes=[pltpu.VMEM((2, 2, tm // SUBLANES, SUBLANES, d), F32),
                            pltpu.SemaphoreType.DMA((2,))]),
        out_shape=jax.ShapeDtypeStruct((bsz, n, d), F32),
        compiler_params=_cparams("arbitrary", "arbitrary"),
        name="moe_combine",
    )(dest_a, dest_b, x1, route, gt2, gf, yb)


def _rope_tables(n):
    rows = n // GRID_W
    r, col = jnp.meshgrid(jnp.arange(rows, dtype=F32), jnp.arange(GRID_W, dtype=F32), indexing='ij')
    n_freq = RET_QK_DIM // 4
    inv_freq = ROPE_BASE ** (-jnp.arange(n_freq, dtype=F32) / n_freq)
    ang_r = r.reshape(-1)[:, None] * inv_freq
    ang_c = col.reshape(-1)[:, None] * inv_freq
    cos_t = jnp.concatenate([jnp.cos(ang_r), jnp.cos(ang_r), jnp.cos(ang_c), jnp.cos(ang_c)], axis=-1)
    sin_t = jnp.concatenate([-jnp.sin(ang_r), jnp.sin(ang_r), -jnp.sin(ang_c), jnp.sin(ang_c)], axis=-1)
    return cos_t, sin_t


def _hyena_feats(n):
    t = jnp.arange(n, dtype=F32) / n
    bands = jnp.linspace(1e-4, HY_BANDS - 1, HY_BANDS, dtype=F32)
    phase = 2.0 * math.pi * t[:, None] * bands[None, :]
    feats = jnp.concatenate([t[:, None], jnp.cos(phase), -jnp.sin(phase)], axis=-1)
    return jnp.pad(feats, ((0, 0), (0, FFN_LANES - HY_POS_DIM)))


def _layer(x, ctx, mods, norm1_g, norm2_g, w_in, b_in, ret_decay_logit, ret_w_o, hy_conv_w,
           hy_conv_b, hy_ffn_w1, hy_ffn_b1, hy_ffn_freq, hy_ffn_w2, hy_ffn_b2, hy_ffn_w3, hy_skip,
           hy_w_o, w_out, router_group_w, router_group_b, router_expert_w, router_expert_b,
           expert_w1, expert_w3, expert_w2, final_norm_g):
    bsz, n, d = x.shape
    n_ctx = ctx.shape[1]
    mod_lat = mods[:bsz].reshape(bsz, 6, 1, d)
    sh1, sc1, gt1, sh2, sc2, gt2 = (mod_lat[:, s] for s in range(6))
    mod_ctx = mods[bsz].reshape(6, 1, 1, d)
    csh1 = jnp.broadcast_to(mod_ctx[0], (bsz, 1, d))
    csc1 = jnp.broadcast_to(mod_ctx[1], (bsz, 1, d))

    g1 = norm1_g.reshape(1, d)
    b_in2 = b_in.reshape(1, IN_W)
    w_in_b = w_in.astype(BF16)
    proj = _inproj(x, g1, sh1, sc1, w_in_b, b_in2, tm=min(512, n), tn=512)
    proj_c = _inproj(ctx, g1, csh1, csc1, w_in_b[:, COL_K:COL_G], b_in2[:, COL_K:COL_G],
                     tm=n_ctx, tn=512)

    cos_r, sin_r = _rope_tables(n)
    lgt = jnp.broadcast_to(ret_decay_logit.astype(F32).reshape(2 * RET_HEADS, 1),
                           (2 * RET_HEADS, RET_V_DIM))
    ret = _retention(proj, proj_c, cos_r, sin_r, lgt)

    slow = abs(math.log(HY_DECAY_TARGET)) / HY_SLOW_PCT
    fast = abs(math.log(HY_DECAY_TARGET)) / HY_FAST_PCT
    deltas = jnp.tile(jnp.linspace(slow, fast, HY_W, dtype=F32), 2).reshape(1, 2 * HY_W)
    fpad = FFN_LANES - HY_FFN
    row = lambda a: jnp.pad(a.reshape(1, HY_FFN), ((0, 0), (0, fpad)))
    fs, fd = _hyena_filters(n, _hyena_feats(n),
                            jnp.pad(hy_ffn_w1, ((0, FFN_LANES - HY_POS_DIM), (0, fpad))), row(hy_ffn_b1),
                            row(hy_ffn_freq), jnp.pad(hy_ffn_w2, ((0, fpad), (0, fpad))), row(hy_ffn_b2),
                            jnp.pad(hy_ffn_w3, ((0, fpad), (0, 0))), deltas)
    z, x0 = _hyena_pre(proj, hy_conv_w, hy_conv_b)
    cos_d, sin_d = _dft_tables(n)
    tile = min(512, n)
    spec_a, spec_b = _dft_filters(cos_d, sin_d, fs, fd, tf=tile, tk=n)
    tn = min(2048, bsz * HY_W)
    u, v = _dft_fwd(cos_d, sin_d, z, spec_a, spec_b, tf=tile, tn=tn, tk=tile)
    skip_t = jnp.tile(hy_skip.reshape(1, HY_W), (1, bsz))
    hy = _dft_inv(cos_d.T, sin_d.T, u, v, z, x0, skip_t, tt=tile, tn=tn, tk=tile)

    rw = jnp.zeros((d, ROUTE_LANES), F32)
    rw = rw.at[:, :N_GROUPS].set(router_group_w).at[:, EXPERT_LANE0:EXPERT_LANE0 + N_EXPERTS].set(router_expert_w)
    rb = jnp.zeros((1, ROUTE_LANES), F32)
    rb = rb.at[0, :N_GROUPS].set(router_group_b).at[0, EXPERT_LANE0:EXPERT_LANE0 + N_EXPERTS].set(router_expert_b)
    rw_hi, rw_lo = _split_bf16(rw)
    sub = min(512, n)
    x1, h2, route, cnt = _merge(ret, hy, proj, x, gt1, sh2, sc2, norm2_g.reshape(1, d),
                                ret_w_o.astype(BF16), hy_w_o.astype(BF16), w_out.astype(BF16),
                                rw_hi, rw_lo, rb, tm=min(2 * sub, n), sub=sub)

    t_all = bsz * n
    counts = cnt[0, EXPERT_LANE0:EXPERT_LANE0 + N_EXPERTS].astype(jnp.int32)
    padded = (counts + MOE_BLK - 1) // MOE_BLK * MOE_BLK
    pad_end = jnp.cumsum(padded)
    pad_start = pad_end - padded
    route2 = route.reshape(t_all, ROUTE_LANES)
    e_ab = route2[:, 0:2].astype(jnp.int32)
    onehot = e_ab[:, :, None] == jnp.arange(N_EXPERTS, dtype=jnp.int32)[None, None, :]
    dest = jnp.sum(jnp.where(onehot, pad_start[None, None, :], 0), axis=-1) + route2[:, 4:6].astype(jnp.int32)
    dest_a, dest_b = dest[:, 0], dest[:, 1]
    n_blocks = -(-(2 * t_all + N_EXPERTS * (MOE_BLK - 1)) // MOE_BLK)
    blk0 = jnp.arange(n_blocks, dtype=jnp.int32) * MOE_BLK
    block_expert = jnp.minimum(jnp.sum(blk0[:, None] >= pad_end[None, :], axis=1), N_EXPERTS - 1).astype(jnp.int32)
    n_used = (pad_end[-1:] // MOE_BLK).astype(jnp.int32)

    xb = _dispatch(dest_a, dest_b, pad_end.astype(jnp.int32), h2.reshape(t_all, d),
                   n_blocks * MOE_BLK, tm=min(256, n))
    yb = _experts(block_expert, n_used, xb, expert_w1, expert_w3, expert_w2)
    return _combine(dest_a, dest_b, x1, route, gt2, final_norm_g.reshape(1, d), yb, tm=min(256, n))


def kernel(x, c, ctx, c_ctx, ada_w, ada_b, norm1_g, norm2_g, w_in, b_in, ret_decay_logit, ret_w_o, hy_conv_w, hy_conv_b, hy_ffn_w1, hy_ffn_b1, hy_ffn_freq, hy_ffn_w2, hy_ffn_b2, hy_ffn_w3, hy_skip, hy_w_o, w_out, router_group_w, router_group_b, router_expert_w, router_expert_b, expert_w1, expert_w3, expert_w2, final_norm_g):
    depth = ada_w.shape[0]
    assert depth == 1, "single-layer problem: the context stream is only read by the retention states"
    bsz, d = c.shape
    rows = -(-(bsz + 1) // 8) * 8
    cc = jnp.zeros((rows, d), F32).at[:bsz].set(c).at[bsz].set(c_ctx)
    mods = _adaln(cc, ada_w[0], ada_b[0].reshape(1, -1))
    return _layer(x, ctx, mods, norm1_g[0], norm2_g[0], w_in[0], b_in[0], ret_decay_logit[0],
                  ret_w_o[0], hy_conv_w[0], hy_conv_b[0], hy_ffn_w1[0], hy_ffn_b1[0], hy_ffn_freq[0],
                  hy_ffn_w2[0], hy_ffn_b2[0], hy_ffn_w3[0], hy_skip[0], hy_w_o[0], w_out[0],
                  router_group_w[0], router_group_b[0], router_expert_w[0], router_expert_b[0],
                  expert_w1[0], expert_w3[0], expert_w2[0], final_norm_g)
```

```python
import functools
import math

import jax
import jax.numpy as jnp
from jax import lax
from jax.experimental import pallas as pl
from jax.experimental.pallas import tpu as pltpu

F32 = jnp.float32
BF16 = jnp.bfloat16

D_MODEL = 1024
EPS = 1e-6
GRID_W = 64
ROPE_BASE = 10000.0

RET_HEADS = 4
RET_QK_DIM = 128
RET_V_DIM = 256
RET_CHUNK = 256
RET_UNROLL = 2
QK_W = RET_HEADS * RET_QK_DIM
V_W = RET_HEADS * RET_V_DIM

HY_W = 512
HY_POS_DIM = 33
HY_BANDS = (HY_POS_DIM - 1) // 2
HY_FFN = 64
FFN_LANES = 128
HY_DECAY_TARGET = 1e-2
HY_FAST_PCT = 0.3
HY_SLOW_PCT = 1.5
DFT_GROUP = 64

N_GROUPS = 4
EXPERTS_PER_GROUP = 8
N_EXPERTS = N_GROUPS * EXPERTS_PER_GROUP
EXPERT_HIDDEN = 512
ROUTE_LANES = 128
EXPERT_LANE0 = N_GROUPS
MOE_BLK = 512
SUBLANES = 8

IN_W = 2 * QK_W + 2 * V_W + 3 * HY_W + 2 * D_MODEL
COL_Q, COL_K, COL_V, COL_G = 0, QK_W, 2 * QK_W, 2 * QK_W + V_W
COL_HY = 2 * QK_W + 2 * V_W
COL_GR = COL_HY + 3 * HY_W
COL_GH = COL_GR + D_MODEL

VMEM_LIMIT = 56 * 1024 * 1024
NEG = -1e30


def _cparams(*sem):
    return pltpu.CompilerParams(dimension_semantics=sem, vmem_limit_bytes=VMEM_LIMIT)


def _sigmoid(x):
    return 1.0 / (1.0 + jnp.exp(-x))


def _dot(a, b):
    return jnp.dot(a, b, preferred_element_type=F32)


def _dot_t0(a, b):
    return lax.dot_general(a, b, (((0,), (0,)), ((), ())), preferred_element_type=F32)


def _dot_nt(a, b):
    return lax.dot_general(a, b, (((1,), (1,)), ((), ())), preferred_element_type=F32)


def _split_bf16(a):
    hi = a.astype(BF16)
    lo = (a - hi.astype(F32)).astype(BF16)
    return hi, lo


def _dot3(a, b):
    ah, al = _split_bf16(a)
    bh, bl = _split_bf16(b)
    return _dot(ah, bh) + _dot(al, bh) + _dot(ah, bl)


def _adaln_kernel(c_ref, w_ref, b_ref, o_ref):
    c = c_ref[...]
    o_ref[...] = _dot3(c * _sigmoid(c), w_ref[...]) + b_ref[...]


def _adaln(cc, w, b):
    rows, d = cc.shape
    n = w.shape[1]
    tn = 1536
    return pl.pallas_call(
        _adaln_kernel,
        grid=(n // tn,),
        in_specs=[pl.BlockSpec((rows, d), lambda j: (0, 0)),
                  pl.BlockSpec((d, tn), lambda j: (0, j)),
                  pl.BlockSpec((1, tn), lambda j: (0, j))],
        out_specs=pl.BlockSpec((rows, tn), lambda j: (0, j)),
        out_shape=jax.ShapeDtypeStruct((rows, n), F32),
        compiler_params=_cparams("arbitrary"),
        name="adaln",
    )(cc, w, b)


def _inproj_kernel(x_ref, g_ref, sh_ref, sc_ref, w_ref, b_ref, o_ref, *, tn):
    x = x_ref[0]
    y = x * lax.rsqrt(jnp.mean(x * x, axis=-1, keepdims=True) + EPS) * g_ref[...]
    h = (y * (1.0 + sc_ref[0]) + sh_ref[0]).astype(BF16)
    for j in range(w_ref.shape[1] // tn):
        cols = slice(j * tn, (j + 1) * tn)
        o_ref[0, :, cols] = (_dot(h, w_ref[:, cols]) + b_ref[:, cols]).astype(o_ref.dtype)


def _inproj(x, gain, shift, scale, w, b, tm, tn):
    bsz, n, d = x.shape
    nw = w.shape[1]
    return pl.pallas_call(
        functools.partial(_inproj_kernel, tn=tn),
        grid=(bsz, n // tm),
        in_specs=[pl.BlockSpec((1, tm, d), lambda bi, i: (bi, i, 0)),
                  pl.BlockSpec((1, d), lambda bi, i: (0, 0)),
                  pl.BlockSpec((1, 1, d), lambda bi, i: (bi, 0, 0)),
                  pl.BlockSpec((1, 1, d), lambda bi, i: (bi, 0, 0)),
                  pl.BlockSpec((d, nw), lambda bi, i: (0, 0)),
                  pl.BlockSpec((1, nw), lambda bi, i: (0, 0))],
        out_specs=pl.BlockSpec((1, tm, nw), lambda bi, i: (bi, i, 0)),
        out_shape=jax.ShapeDtypeStruct((bsz, n, nw), BF16),
        compiler_params=_cparams("arbitrary", "arbitrary"),
        name="inproj",
    )(x, gain, shift, scale, w, b)


def _log_sigmoid(x):
    return jnp.minimum(x, 0.0) - jnp.log(1.0 + jnp.exp(-jnp.abs(x)))


def _rope_partner(x):
    lane = lax.broadcasted_iota(jnp.int32, x.shape, 1)
    return jnp.where((lane % 64) < 32, pltpu.roll(x, 96, axis=1), pltpu.roll(x, 32, axis=1))


def _retention_kernel(q_ref, k_ref, v_ref, g_ref, kc_ref, vc_ref, cos_ref, sin_ref, lgt_ref,
                      o_ref, qr_scr, kr_scr, oacc_scr, sf_scr, sb_scr, *, n_tok, n_ctx):
    c_len = RET_CHUNK
    n_chunks = n_tok // c_len
    head = pl.program_id(1)
    k_scale = RET_QK_DIM ** -0.5

    lg_f = _log_sigmoid(lgt_ref[pl.ds(head, 1), :])
    lg_b = _log_sigmoid(lgt_ref[pl.ds(RET_HEADS + head, 1), :])
    lgf_k, lgb_k = lg_f[:, :RET_QK_DIM], lg_b[:, :RET_QK_DIM]

    ii = lax.broadcasted_iota(jnp.int32, (c_len, c_len), 0)
    jj = lax.broadcasted_iota(jnp.int32, (c_len, c_len), 1)
    dif = (ii - jj).astype(F32)
    decay_in = jnp.where(ii >= jj, jnp.exp(lg_f[:, :c_len] * jnp.maximum(dif, 0.0)), 0.0) + \
        jnp.where(jj > ii, jnp.exp(lg_b[:, :c_len] * jnp.maximum(-dif, 0.0)), 0.0)
    pos_k = lax.broadcasted_iota(jnp.int32, (c_len, RET_QK_DIM), 0).astype(F32)
    pos_v = lax.broadcasted_iota(jnp.int32, (c_len, RET_V_DIM), 0).astype(F32)
    dq_f = jnp.exp(lg_f * (pos_v + 1.0))
    dq_b = jnp.exp(lg_b * (c_len - pos_v))
    dk_f = jnp.exp(lgf_k * (c_len - 1.0 - pos_k))
    dk_b = jnp.exp(lgb_k * pos_k)
    dchunk_f = jnp.exp(lg_f * float(c_len))
    dchunk_b = jnp.exp(lg_b * float(c_len))

    pos_c = lax.broadcasted_iota(jnp.int32, (n_ctx, RET_QK_DIM), 0).astype(F32)
    kc = kc_ref[0].astype(F32) * k_scale
    vc = vc_ref[0]
    sf_scr[...] = _dot_t0((kc * jnp.exp(lgf_k * (n_ctx - 1.0 - pos_c))).astype(BF16), vc)
    sb_scr[...] = _dot_t0((kc * jnp.exp(lgb_k * pos_c)).astype(BF16), vc)

    def fwd(c, carry):
        r0 = pl.multiple_of(c * c_len, c_len)
        rows = pl.ds(r0, c_len)
        cos = cos_ref[rows, :]
        sin = sin_ref[rows, :]
        q = q_ref[0, rows, :].astype(F32)
        k = k_ref[0, rows, :].astype(F32)
        qr = q * cos + _rope_partner(q) * sin
        kr = (k * cos + _rope_partner(k) * sin) * k_scale
        qb = qr.astype(BF16)
        kb = kr.astype(BF16)
        qr_scr[rows, :] = qb
        kr_scr[rows, :] = kr
        v = v_ref[0, rows, :]
        scores = _dot_nt(qb, kb) * decay_in
        o = _dot(scores.astype(BF16), v) + _dot(qb, sf_scr[...].astype(BF16)) * dq_f
        oacc_scr[rows, :] = o
        sf_scr[...] = sf_scr[...] * dchunk_f + _dot_t0((kr * dk_f).astype(BF16), v)
        return carry

    lax.fori_loop(0, n_chunks, fwd, 0, unroll=RET_UNROLL)

    def bwd(t, carry):
        c = n_chunks - 1 - t
        r0 = pl.multiple_of(c * c_len, c_len)
        rows = pl.ds(r0, c_len)
        v = v_ref[0, rows, :]
        o = oacc_scr[rows, :] + _dot(qr_scr[rows, :], sb_scr[...].astype(BF16)) * dq_b
        o = o * lax.rsqrt(jnp.mean(o * o, axis=-1, keepdims=True) + EPS)
        g = g_ref[0, rows, :].astype(F32)
        o_ref[0, rows, :] = (g * _sigmoid(g) * o).astype(o_ref.dtype)
        sb_scr[...] = sb_scr[...] * dchunk_b + _dot_t0((kr_scr[rows, :] * dk_b).astype(BF16), v)
        return carry

    lax.fori_loop(0, n_chunks, bwd, 0, unroll=RET_UNROLL)


def _retention(proj, proj_c, cos_t, sin_t, lgt):
    bsz, n, _ = proj.shape
    n_ctx = proj_c.shape[1]
    kq, kv = RET_QK_DIM, RET_V_DIM
    assert n % RET_CHUNK == 0 and RET_CHUNK <= kv
    kern = functools.partial(_retention_kernel, n_tok=n, n_ctx=n_ctx)
    return pl.pallas_call(
        kern,
        grid=(bsz, RET_HEADS),
        in_specs=[pl.BlockSpec((1, n, kq), lambda b, h: (b, 0, COL_Q // kq + h)),
                  pl.BlockSpec((1, n, kq), lambda b, h: (b, 0, COL_K // kq + h)),
                  pl.BlockSpec((1, n, kv), lambda b, h: (b, 0, COL_V // kv + h)),
                  pl.BlockSpec((1, n, kv), lambda b, h: (b, 0, COL_G // kv + h)),
                  pl.BlockSpec((1, n_ctx, kq), lambda b, h: (b, 0, h)),
                  pl.BlockSpec((1, n_ctx, kv), lambda b, h: (b, 0, QK_W // kv + h)),
                  pl.BlockSpec((n, kq), lambda b, h: (0, 0)),
                  pl.BlockSpec((n, kq), lambda b, h: (0, 0)),
                  pl.BlockSpec((2 * RET_HEADS, kv), lambda b, h: (0, 0))],
        out_specs=pl.BlockSpec((1, n, kv), lambda b, h: (b, 0, h)),
        out_shape=jax.ShapeDtypeStruct((bsz, n, V_W), BF16),
        scratch_shapes=[pltpu.VMEM((n, kq), BF16), pltpu.VMEM((n, kq), F32),
                        pltpu.VMEM((n, kv), F32), pltpu.VMEM((kq, kv), F32),
                        pltpu.VMEM((kq, kv), F32)],
        compiler_params=_cparams("arbitrary", "arbitrary"),
        name="retention",
    )(proj, proj, proj, proj, proj_c, proj_c, cos_t, sin_t, lgt)


def _filter_kernel(feats_ref, w1_ref, b1_ref, fr_ref, w2_ref, b2_ref, w3f_ref, w3b_ref,
                   df_ref, db_ref, fs_ref, fd_ref, *, n_tok):
    fr = fr_ref[...]
    hid = jnp.sin(fr * (_dot3(feats_ref[...], w1_ref[...]) + b1_ref[...]))
    hid = jnp.sin(fr * (_dot3(hid, w2_ref[...]) + b2_ref[...]))
    t = lax.broadcasted_iota(jnp.int32, (n_tok, df_ref.shape[1]), 0).astype(F32) / n_tok

    def one(w3_ref, d_ref):
        f = _dot3(hid, w3_ref[...]) * jnp.exp(-t * d_ref[...])
        return f / jnp.sum(jnp.abs(f), axis=0, keepdims=True)

    hf = one(w3f_ref, df_ref)
    hb = one(w3b_ref, db_ref)
    fs_ref[...] = (hf + hb).astype(fs_ref.dtype)
    fd_ref[...] = (hf - hb).astype(fd_ref.dtype)


def _hyena_filters(n, feats, w1, b1, freq, w2, b2, w3, deltas):
    tc = HY_W
    nf = feats.shape[1]
    kern = functools.partial(_filter_kernel, n_tok=n)
    full = lambda shape: pl.BlockSpec(shape, lambda j: (0, 0))
    return pl.pallas_call(
        kern,
        grid=(HY_W // tc,),
        in_specs=[full((n, nf)), full((nf, FFN_LANES)), full((1, FFN_LANES)), full((1, FFN_LANES)),
                  full((FFN_LANES, FFN_LANES)), full((1, FFN_LANES)),
                  pl.BlockSpec((FFN_LANES, tc), lambda j: (0, j)),
                  pl.BlockSpec((FFN_LANES, tc), lambda j: (0, HY_W // tc + j)),
                  pl.BlockSpec((1, tc), lambda j: (0, j)),
                  pl.BlockSpec((1, tc), lambda j: (0, HY_W // tc + j))],
        out_specs=[pl.BlockSpec((n, tc), lambda j: (0, j)),
                   pl.BlockSpec((n, tc), lambda j: (0, j))],
        out_shape=[jax.ShapeDtypeStruct((n, HY_W), BF16)] * 2,
        compiler_params=_cparams("arbitrary"),
        name="hyena_filters",
    )(feats, w1, b1, freq, w2, b2, w3, w3, deltas, deltas)


def _hyena_pre_kernel(u0_ref, u1_ref, u2_ref, w_ref, b_ref, z_ref, x0_ref, *, n_tok, rows):
    n_steps = n_tok // rows
    tc = z_ref.shape[1]
    halo = 16
    rid = lax.broadcasted_iota(jnp.int32, (rows, tc), 0)

    def conv(u_ref, part, r0, has_prev, has_next):
        x = u_ref[0, pl.ds(r0, rows), :].astype(F32)
        prev_g = u_ref[0, pl.ds(pl.multiple_of(jnp.maximum(r0 - halo, 0), halo), halo), :].astype(F32)
        next_g = u_ref[0, pl.ds(pl.multiple_of(jnp.minimum(r0 + rows, n_tok - halo), halo), halo), :].astype(F32)
        prev_row = jnp.where(has_prev, prev_g[halo - 1:halo, :], 0.0)
        next_row = jnp.where(has_next, next_g[0:1, :], 0.0)
        up = jnp.where(rid == 0, prev_row, pltpu.roll(x, 1, axis=0))
        dn = jnp.where(rid == rows - 1, next_row, pltpu.roll(x, rows - 1, axis=0))
        w = w_ref[part]
        return up * w[0:1, :] + x * w[1:2, :] + dn * w[2:3, :] + b_ref[part]

    def body(s, carry):
        r0 = pl.multiple_of(s * rows, rows)
        has_prev = s > 0
        has_next = s < n_steps - 1
        x0 = conv(u0_ref, 0, r0, has_prev, has_next)
        x1 = conv(u1_ref, 1, r0, has_prev, has_next)
        vv = conv(u2_ref, 2, r0, has_prev, has_next)
        z_ref[pl.ds(r0, rows), :] = (x1 * vv).astype(z_ref.dtype)
        x0_ref[pl.ds(r0, rows), :] = x0.astype(x0_ref.dtype)
        return carry

    lax.fori_loop(0, n_steps, body, 0)


def _hyena_pre(proj, conv_w, conv_b):
    bsz, n, _ = proj.shape
    tc = 256
    nj = HY_W // tc
    rows = min(512, n)
    base = COL_HY // tc
    kern = functools.partial(_hyena_pre_kernel, n_tok=n, rows=rows)
    u_spec = lambda part: pl.BlockSpec((1, n, tc), lambda b, j: (b, 0, base + part * nj + j))
    wp = jnp.zeros((3, 8, HY_W), F32).at[:, :3, :].set(conv_w.reshape(3, 3, HY_W).transpose(1, 0, 2))
    bp = conv_b.reshape(3, 1, HY_W)
    return pl.pallas_call(
        kern,
        grid=(bsz, nj),
        in_specs=[u_spec(0), u_spec(1), u_spec(2),
                  pl.BlockSpec((3, 8, tc), lambda b, j: (0, 0, j)),
                  pl.BlockSpec((3, 1, tc), lambda b, j: (0, 0, j))],
        out_specs=[pl.BlockSpec((n, tc), lambda b, j: (0, b * nj + j)),
                   pl.BlockSpec((n, tc), lambda b, j: (0, b * nj + j))],
        out_shape=[jax.ShapeDtypeStruct((n, bsz * HY_W), BF16)] * 2,
        compiler_params=_cparams("arbitrary", "arbitrary"),
        name="hyena_pre",
    )(proj, proj, proj, wp, bp)


def _dft_factors(n, inverse):
    period = 4 * n
    col = jnp.arange(n, dtype=jnp.int32)[None, :]
    hi = jnp.arange(n // DFT_GROUP, dtype=jnp.int32)[:, None]
    lo = jnp.arange(DFT_GROUP, dtype=jnp.int32)[:, None]
    if inverse:
        m_a, m_b = (2 * col + 1) * (DFT_GROUP * hi), (2 * col + 1) * lo
    else:
        m_a, m_b = (2 * DFT_GROUP * hi) * col, (2 * lo + 1) * col
    ang_a = (m_a % period).astype(F32) * (2.0 * math.pi / period)
    ang_b = (m_b % period).astype(F32) * (2.0 * math.pi / period)
    return jnp.cos(ang_a), jnp.sin(ang_a), jnp.cos(ang_b), jnp.sin(ang_b)


def _dft_tile(ca_ref, sa_ref, cb_ref, sb_ref):
    cb = cb_ref[...]
    sb = sb_ref[...]
    cos_rows, sin_rows = [], []
    for g in range(ca_ref.shape[0]):
        ca = ca_ref[g:g + 1, :]
        sa = sa_ref[g:g + 1, :]
        cos_rows.append((ca * cb - sa * sb).astype(BF16))
        sin_rows.append((sa * cb + ca * sb).astype(BF16))
    return jnp.concatenate(cos_rows, axis=0), jnp.concatenate(sin_rows, axis=0)


def _factor_specs(tr, tc, row_axis, col_axis):
    a_spec = pl.BlockSpec((tr // DFT_GROUP, tc), lambda *ids: (ids[row_axis], ids[col_axis]))
    b_spec = pl.BlockSpec((DFT_GROUP, tc), lambda *ids: (0, ids[col_axis]))
    return [a_spec, a_spec, b_spec, b_spec]


def _dft_filter_kernel(ca_ref, sa_ref, cb_ref, sb_ref, fs_ref, fd_ref, a_ref, b_ref, acc_a, acc_b):
    kk = pl.program_id(1)

    @pl.when(kk == 0)
    def _():
        acc_a[...] = jnp.zeros_like(acc_a)
        acc_b[...] = jnp.zeros_like(acc_b)

    cos_t, sin_t = _dft_tile(ca_ref, sa_ref, cb_ref, sb_ref)
    acc_a[...] += _dot(cos_t, fs_ref[...])
    acc_b[...] += _dot(sin_t, fd_ref[...])

    @pl.when(kk == pl.num_programs(1) - 1)
    def _():
        a_ref[...] = acc_a[...]
        b_ref[...] = acc_b[...]


def _dft_filters(factors, fs, fd, tf, tk):
    n, w = fs.shape
    return pl.pallas_call(
        _dft_filter_kernel,
        grid=(n // tf, n // tk),
        in_specs=_factor_specs(tf, tk, 0, 1) + [
            pl.BlockSpec((tk, w), lambda i, kk: (kk, 0)),
            pl.BlockSpec((tk, w), lambda i, kk: (kk, 0))],
        out_specs=[pl.BlockSpec((tf, w), lambda i, kk: (i, 0)),
                   pl.BlockSpec((tf, w), lambda i, kk: (i, 0))],
        out_shape=[jax.ShapeDtypeStruct((n, w), F32)] * 2,
        scratch_shapes=[pltpu.VMEM((tf, w), F32), pltpu.VMEM((tf, w), F32)],
        compiler_params=_cparams("arbitrary", "arbitrary"),
        name="dft_filters",
    )(*factors, fs, fd)


def _dft_fwd_kernel(ca_ref, sa_ref, cb_ref, sb_ref, z_ref, a_ref, b_ref, u_ref, v_ref, acc_p, acc_q):
    kk = pl.program_id(2)

    @pl.when(kk == 0)
    def _():
        acc_p[...] = jnp.zeros_like(acc_p)
        acc_q[...] = jnp.zeros_like(acc_q)

    cos_t, sin_t = _dft_tile(ca_ref, sa_ref, cb_ref, sb_ref)
    z = z_ref[...]
    acc_p[...] += _dot(cos_t, z)
    acc_q[...] += _dot(sin_t, z)

    @pl.when(kk == pl.num_programs(2) - 1)
    def _():
        a = a_ref[...]
        b = b_ref[...]
        w = a.shape[1]
        for s in range(u_ref.shape[1] // w):
            cols = slice(s * w, (s + 1) * w)
            p = acc_p[:, cols]
            q = acc_q[:, cols]
            u_ref[:, cols] = (p * a - q * b).astype(u_ref.dtype)
            v_ref[:, cols] = (p * b + q * a).astype(v_ref.dtype)


def _dft_fwd(factors, z, spec_a, spec_b, tf, tn, tk):
    n, ncol = z.shape
    w = spec_a.shape[1]
    return pl.pallas_call(
        _dft_fwd_kernel,
        grid=(n // tf, ncol // tn, n // tk),
        in_specs=_factor_specs(tf, tk, 0, 2) + [
            pl.BlockSpec((tk, tn), lambda i, j, kk: (kk, j)),
            pl.BlockSpec((tf, w), lambda i, j, kk: (i, 0)),
            pl.BlockSpec((tf, w), lambda i, j, kk: (i, 0))],
        out_specs=[pl.BlockSpec((tf, tn), lambda i, j, kk: (i, j)),
                   pl.BlockSpec((tf, tn), lambda i, j, kk: (i, j))],
        out_shape=[jax.ShapeDtypeStruct((n, ncol), BF16)] * 2,
        scratch_shapes=[pltpu.VMEM((tf, tn), F32), pltpu.VMEM((tf, tn), F32)],
        compiler_params=_cparams("arbitrary", "arbitrary", "arbitrary"),
        name="dft_fwd",
    )(*factors, z, spec_a, spec_b)


def _dft_inv_kernel(ca_ref, sa_ref, cb_ref, sb_ref, u_ref, v_ref, z_ref, x0_ref, skip_ref, o_ref, acc,
                    *, inv_n):
    kk = pl.program_id(2)

    @pl.when(kk == 0)
    def _():
        acc[...] = jnp.zeros_like(acc)

    cos_t, sin_t = _dft_tile(ca_ref, sa_ref, cb_ref, sb_ref)
    acc[...] += _dot(cos_t, u_ref[...]) + _dot(sin_t, v_ref[...])

    @pl.when(kk == pl.num_programs(2) - 1)
    def _():
        y = acc[...] * inv_n + z_ref[...].astype(F32) * skip_ref[...]
        o_ref[...] = (x0_ref[...].astype(F32) * y).astype(o_ref.dtype)


def _dft_inv(factors, u, v, z, x0, skip_t, tt, tn, tk):
    n, ncol = u.shape
    kern = functools.partial(_dft_inv_kernel, inv_n=1.0 / n)
    return pl.pallas_call(
        kern,
        grid=(n // tt, ncol // tn, n // tk),
        in_specs=_factor_specs(tt, tk, 0, 2) + [
                  pl.BlockSpec((tk, tn), lambda i, j, kk: (kk, j)),
                  pl.BlockSpec((tk, tn), lambda i, j, kk: (kk, j)),
                  pl.BlockSpec((tt, tn), lambda i, j, kk: (i, j)),
                  pl.BlockSpec((tt, tn), lambda i, j, kk: (i, j)),
                  pl.BlockSpec((1, tn), lambda i, j, kk: (0, j))],
        out_specs=pl.BlockSpec((tt, tn), lambda i, j, kk: (i, j)),
        out_shape=jax.ShapeDtypeStruct((n, ncol), BF16),
        scratch_shapes=[pltpu.VMEM((tt, tn), F32)],
        compiler_params=_cparams("arbitrary", "arbitrary", "arbitrary"),
        name="dft_inv",
    )(*factors, u, v, z, x0, skip_t)


def _merge_kernel(ret_ref, hy_ref, gr0_ref, gr1_ref, gh0_ref, gh1_ref, x_ref, gt1_ref, sh2_ref,
                  sc2_ref, g2_ref, wro_ref, who_ref, wout_ref, rwh_ref, rwl_ref, rb_ref,
                  x1_ref, h2_ref, route_ref, cnt_ref, base_scr, *, sub):
    first = jnp.logical_and(pl.program_id(0) == 0, pl.program_id(1) == 0)

    @pl.when(first)
    def _():
        base_scr[...] = jnp.zeros_like(base_scr)

    for s0 in range(0, x_ref.shape[1], sub):
        _merge_rows(slice(s0, s0 + sub), ret_ref, hy_ref, gr0_ref, gr1_ref, gh0_ref, gh1_ref, x_ref,
                    gt1_ref, sh2_ref, sc2_ref, g2_ref, wro_ref, who_ref, wout_ref, rwh_ref, rwl_ref,
                    rb_ref, x1_ref, h2_ref, route_ref, base_scr)
    cnt_ref[...] = base_scr[...]


def _merge_rows(rows, ret_ref, hy_ref, gr0_ref, gr1_ref, gh0_ref, gh1_ref, x_ref, gt1_ref, sh2_ref,
                sc2_ref, g2_ref, wro_ref, who_ref, wout_ref, rwh_ref, rwl_ref, rb_ref,
                x1_ref, h2_ref, route_ref, base_scr):
    tm = rows.stop - rows.start
    gate_r = jnp.concatenate([gr0_ref[0, rows, :], gr1_ref[0, rows, :]], axis=1).astype(F32)
    gate_h = jnp.concatenate([gh0_ref[0, rows, :], gh1_ref[0, rows, :]], axis=1).astype(F32)
    mixed = _sigmoid(gate_r) * _dot(ret_ref[0, rows, :], wro_ref[...]) + \
        _sigmoid(gate_h) * _dot(hy_ref[rows, :], who_ref[...])
    x1 = x_ref[0, rows, :] + gt1_ref[0] * _dot(mixed.astype(BF16), wout_ref[...])
    x1_ref[0, rows, :] = x1
    h2 = x1 * lax.rsqrt(jnp.mean(x1 * x1, axis=-1, keepdims=True) + EPS) * g2_ref[...]
    h2 = h2 * (1.0 + sc2_ref[0]) + sh2_ref[0]
    h2_ref[0, rows, :] = h2

    h2_hi, h2_lo = _split_bf16(h2)
    rw_hi = rwh_ref[...]
    logits = _dot(h2_hi, rw_hi) + _dot(h2_lo, rw_hi) + _dot(h2_hi, rwl_ref[...]) + rb_ref[...]
    lane = lax.broadcasted_iota(jnp.int32, logits.shape, 1)
    lane_f = lane.astype(F32)
    big = float(ROUTE_LANES)

    def first_lane(mask):
        return jnp.min(jnp.where(mask, lane_f, big), axis=1, keepdims=True)

    is_group = lane < N_GROUPS
    gl = jnp.where(is_group, logits, NEG)
    ge = jnp.where(is_group, jnp.exp(gl - jnp.max(gl, axis=1, keepdims=True)), 0.0)
    group_p = ge / jnp.sum(ge, axis=1, keepdims=True)
    p_star = jnp.max(group_p, axis=1, keepdims=True)
    g_star = first_lane(jnp.logical_and(is_group, group_p == p_star))
    lo = EXPERT_LANE0 + g_star * EXPERTS_PER_GROUP
    in_group = jnp.logical_and(lane_f >= lo, lane_f < lo + EXPERTS_PER_GROUP)
    el = jnp.where(in_group, logits, NEG)
    ee = jnp.where(in_group, jnp.exp(el - jnp.max(el, axis=1, keepdims=True)), 0.0)
    sp = jnp.where(in_group, ee / jnp.sum(ee, axis=1, keepdims=True), -1.0)
    w_a = jnp.max(sp, axis=1, keepdims=True)
    l_a = first_lane(sp == w_a)
    sp2 = jnp.where(lane_f == l_a, -1.0, sp)
    w_b = jnp.max(sp2, axis=1, keepdims=True)
    l_b = first_lane(sp2 == w_b)
    wsum = w_a + w_b
    wt_a = p_star * w_a / wsum
    wt_b = p_star * w_b / wsum

    hit_a = lane_f == l_a
    hit_b = lane_f == l_b
    onehot = jnp.where(jnp.logical_or(hit_a, hit_b), 1.0, 0.0)
    ri = lax.broadcasted_iota(jnp.int32, (tm, tm), 0)
    ci = lax.broadcasted_iota(jnp.int32, (tm, tm), 1)
    tri = jnp.where(ri > ci, 1.0, 0.0).astype(BF16)
    before = _dot(tri, onehot.astype(BF16)) + base_scr[0:1, :]
    rank_a = jnp.sum(jnp.where(hit_a, before, 0.0), axis=1, keepdims=True)
    rank_b = jnp.sum(jnp.where(hit_b, before, 0.0), axis=1, keepdims=True)
    base_scr[0:1, :] = base_scr[0:1, :] + jnp.sum(onehot, axis=0, keepdims=True)

    vals = (l_a - EXPERT_LANE0, l_b - EXPERT_LANE0, wt_a, wt_b, rank_a, rank_b)
    route = jnp.zeros(logits.shape, F32)
    for idx, val in enumerate(vals):
        route = jnp.where(lane == idx, val, route)
    route_ref[0, rows, :] = route


def _merge(ret, hy, proj, x, gt1, sh2, sc2, g2, w_ro, w_ho, w_out, rw_hi, rw_lo, rb, tm, sub):
    bsz, n, d = x.shape
    nt = n // tm
    hw = HY_W
    gspec = lambda col: pl.BlockSpec((1, tm, hw), lambda b, i: (b, i, col // hw))
    vec = pl.BlockSpec((1, 1, d), lambda b, i: (b, 0, 0))
    full = lambda shape: pl.BlockSpec(shape, lambda b, i: (0, 0))
    tok = pl.BlockSpec((1, tm, d), lambda b, i: (b, i, 0))
    return pl.pallas_call(
        functools.partial(_merge_kernel, sub=sub),
        grid=(bsz, nt),
        in_specs=[tok,
                  pl.BlockSpec((tm, hw), lambda b, i: (i, b)),
                  gspec(COL_GR), gspec(COL_GR + hw), gspec(COL_GH), gspec(COL_GH + hw),
                  tok, vec, vec, vec, full((1, d)),
                  full((V_W, d)), full((hw, d)), full((d, d)),
                  full((d, ROUTE_LANES)), full((d, ROUTE_LANES)), full((1, ROUTE_LANES))],
        out_specs=[tok, tok,
                   pl.BlockSpec((1, tm, ROUTE_LANES), lambda b, i: (b, i, 0)),
                   pl.BlockSpec((8, ROUTE_LANES), lambda b, i: (0, 0))],
        out_shape=[jax.ShapeDtypeStruct((bsz, n, d), F32),
                   jax.ShapeDtypeStruct((bsz, n, d), F32),
                   jax.ShapeDtypeStruct((bsz, n, ROUTE_LANES), F32),
                   jax.ShapeDtypeStruct((8, ROUTE_LANES), F32)],
        scratch_shapes=[pltpu.VMEM((8, ROUTE_LANES), F32)],
        compiler_params=_cparams("arbitrary", "arbitrary"),
        name="merge_route",
    )(ret, hy, proj, proj, proj, proj, x, gt1, sh2, sc2, g2, w_ro, w_ho, w_out, rw_hi, rw_lo, rb)


def _dispatch_kernel(da_ref, db_ref, pend_ref, h_ref, xb_ref, zero_scr, sem, zsem, *, tm):
    i = pl.program_id(0)

    def row_copy(g, u, dst_row):
        return pltpu.make_async_copy(h_ref.at[g, pl.ds(u, 1)], xb_ref.at[pl.ds(dst_row, 1)], sem)

    @pl.when(i == 0)
    def _():
        zero_scr[...] = jnp.zeros_like(zero_scr)

        def zcopy(e):
            return pltpu.make_async_copy(
                zero_scr, xb_ref.at[pl.ds(pl.multiple_of(pend_ref[e] - MOE_BLK, MOE_BLK), MOE_BLK)], zsem)

        def nonempty(e):
            prev = jnp.where(e > 0, pend_ref[jnp.maximum(e - 1, 0)], 0)
            return pend_ref[e] > prev

        for e in range(N_EXPERTS):
            @pl.when(nonempty(e))
            def _():
                zcopy(e).start()
        for e in range(N_EXPERTS):
            @pl.when(nonempty(e))
            def _():
                zcopy(e).wait()

        def tail_copy(blk):
            return pltpu.make_async_copy(
                zero_scr, xb_ref.at[pl.ds(pl.multiple_of(blk * MOE_BLK, MOE_BLK), MOE_BLK)], zsem)

        first_unused = pend_ref[N_EXPERTS - 1] // MOE_BLK
        n_blocks = xb_ref.shape[0] // MOE_BLK
        lax.fori_loop(first_unused, n_blocks, lambda blk, c: (tail_copy(blk).start(), c)[1], 0)
        lax.fori_loop(first_unused, n_blocks, lambda blk, c: (tail_copy(blk).wait(), c)[1], 0)

    def issue(g, carry):
        t0 = i * tm + g * SUBLANES
        for u in range(SUBLANES):
            row_copy(g, u, da_ref[t0 + u]).start(priority=0)
            row_copy(g, u, db_ref[t0 + u]).start(priority=1)
        return carry

    lax.fori_loop(0, tm // SUBLANES, issue, 0)

    def drain(g, carry):
        for _ in range(2 * SUBLANES):
            row_copy(0, 0, 0).wait()
        return carry

    lax.fori_loop(0, tm // SUBLANES, drain, 0)


def _dispatch(dest_a, dest_b, pad_end, h2, n_rows, tm):
    t_all, d = h2.shape
    kern = functools.partial(_dispatch_kernel, tm=tm)
    return pl.pallas_call(
        kern,
        grid_spec=pltpu.PrefetchScalarGridSpec(
            num_scalar_prefetch=3,
            grid=(t_all // tm,),
            in_specs=[pl.BlockSpec((tm // SUBLANES, SUBLANES, d), lambda i, *_: (i, 0, 0))],
            out_specs=pl.BlockSpec(memory_space=pl.ANY),
            scratch_shapes=[pltpu.VMEM((MOE_BLK, d), F32),
                            pltpu.SemaphoreType.DMA(()), pltpu.SemaphoreType.DMA(())]),
        out_shape=jax.ShapeDtypeStruct((n_rows, d), F32),
        compiler_params=_cparams("arbitrary"),
        name="moe_dispatch",
    )(dest_a, dest_b, pad_end, h2.reshape(t_all // SUBLANES, SUBLANES, d))


def _expert_kernel(be_ref, nu_ref, x_ref, w1_ref, w3_ref, w2_ref, o_ref, w1_scr, w3_scr, w2_scr):
    i = pl.program_id(0)

    @pl.when(jnp.logical_or(i == 0, be_ref[i] != be_ref[jnp.maximum(i - 1, 0)]))
    def _():
        w1_scr[...] = w1_ref[0].astype(BF16)
        w3_scr[...] = w3_ref[0].astype(BF16)
        w2_scr[...] = w2_ref[0].astype(BF16)

    @pl.when(i < nu_ref[0])
    def _():
        half = x_ref.shape[0] // 2
        for rows in (slice(0, half), slice(half, 2 * half)):
            x = x_ref[rows, :].astype(BF16)
            a = _dot(x, w1_scr[...])
            b = _dot(x, w3_scr[...])
            o_ref[rows, :] = _dot((a * _sigmoid(a) * b).astype(BF16), w2_scr[...])

    @pl.when(i >= nu_ref[0])
    def _():
        o_ref[...] = jnp.zeros_like(o_ref)


def _experts(block_expert, n_used, xb, w1, w3, w2):
    n_rows, d = xb.shape
    hid = w1.shape[2]
    row_blk = lambda i, be, nu: (jnp.minimum(i, nu[0] - 1), 0)
    return pl.pallas_call(
        _expert_kernel,
        grid_spec=pltpu.PrefetchScalarGridSpec(
            num_scalar_prefetch=2,
            grid=(n_rows // MOE_BLK,),
            in_specs=[pl.BlockSpec((MOE_BLK, d), row_blk),
                      pl.BlockSpec((1, d, hid), lambda i, be, nu: (be[i], 0, 0)),
                      pl.BlockSpec((1, d, hid), lambda i, be, nu: (be[i], 0, 0)),
                      pl.BlockSpec((1, hid, d), lambda i, be, nu: (be[i], 0, 0))],
            out_specs=pl.BlockSpec((MOE_BLK, d), lambda i, be, nu: (i, 0)),
            scratch_shapes=[pltpu.VMEM((d, hid), BF16), pltpu.VMEM((d, hid), BF16),
                            pltpu.VMEM((hid, d), BF16)]),
        out_shape=jax.ShapeDtypeStruct((n_rows, d), F32),
        compiler_params=_cparams("arbitrary"),
        name="moe_experts",
    )(block_expert, n_used, xb, w1, w3, w2)


def _combine_kernel(da_ref, db_ref, x1_ref, route_ref, gt2_ref, gf_ref, yb_ref, o_ref,
                    buf, sems, *, tm, tiles_per_batch):
    b = pl.program_id(0)
    i = pl.program_id(1)
    step = b * tiles_per_batch + i
    n_steps = pl.num_programs(0) * tiles_per_batch
    slot = step % 2

    def row_copy(src_row, s, which, g, u):
        return pltpu.make_async_copy(yb_ref.at[pl.ds(src_row, 1)], buf.at[s, which, g, pl.ds(u, 1)],
                                     sems.at[s])

    def issue_tile(tile, s):
        def body(g, carry):
            t0 = tile * tm + g * SUBLANES
            for u in range(SUBLANES):
                row_copy(da_ref[t0 + u], s, 0, g, u).start(priority=0)
                row_copy(db_ref[t0 + u], s, 1, g, u).start(priority=1)
            return carry
        lax.fori_loop(0, tm // SUBLANES, body, 0)

    @pl.when(step == 0)
    def _():
        issue_tile(0, 0)

    @pl.when(step + 1 < n_steps)
    def _():
        issue_tile(step + 1, 1 - slot)

    def drain(g, carry):
        for _ in range(SUBLANES):
            row_copy(0, slot, 0, 0, 0).wait()
            row_copy(0, slot, 1, 0, 0).wait()
        return carry

    lax.fori_loop(0, tm // SUBLANES, drain, 0)

    route = route_ref[0]
    d = o_ref.shape[2]
    y = route[:, 2:3] * buf[slot, 0].reshape(tm, d) + route[:, 3:4] * buf[slot, 1].reshape(tm, d)
    xo = x1_ref[0] + gt2_ref[0] * y
    o_ref[0] = xo * lax.rsqrt(jnp.mean(xo * xo, axis=-1, keepdims=True) + EPS) * gf_ref[...]


def _combine(dest_a, dest_b, x1, route, gt2, gf, yb, tm):
    bsz, n, d = x1.shape
    nt = n // tm
    kern = functools.partial(_combine_kernel, tm=tm, tiles_per_batch=nt)
    return pl.pallas_call(
        kern,
        grid_spec=pltpu.PrefetchScalarGridSpec(
            num_scalar_prefetch=2,
            grid=(bsz, nt),
            in_specs=[pl.BlockSpec((1, tm, d), lambda b, i, *_: (b, i, 0)),
                      pl.BlockSpec((1, tm, ROUTE_LANES), lambda b, i, *_: (b, i, 0)),
                      pl.BlockSpec((1, 1, d), lambda b, i, *_: (b, 0, 0)),
                      pl.BlockSpec((1, d), lambda b, i, *_: (0, 0)),
                      pl.BlockSpec(memory_space=pl.ANY)],
            out_specs=pl.BlockSpec((1, tm, d), lambda b, i, *_: (b, i, 0)),
            scratch_shapes=[pltpu.VMEM((2, 2, tm // SUBLANES, SUBLANES, d), F32),
                            pltpu.SemaphoreType.DMA((2,))]),
        out_shape=jax.ShapeDtypeStruct((bsz, n, d), F32),
        compiler_params=_cparams("arbitrary", "arbitrary"),
        name="moe_combine",
    )(dest_a, dest_b, x1, route, gt2, gf, yb)


def _rope_tables(n):
    rows = n // GRID_W
    r, col = jnp.meshgrid(jnp.arange(rows, dtype=F32), jnp.arange(GRID_W, dtype=F32), indexing='ij')
    n_freq = RET_QK_DIM // 4
    inv_freq = ROPE_BASE ** (-jnp.arange(n_freq, dtype=F32) / n_freq)
    ang_r = r.reshape(-1)[:, None] * inv_freq
    ang_c = col.reshape(-1)[:, None] * inv_freq
    cos_t = jnp.concatenate([jnp.cos(ang_r), jnp.cos(ang_r), jnp.cos(ang_c), jnp.cos(ang_c)], axis=-1)
    sin_t = jnp.concatenate([-jnp.sin(ang_r), jnp.sin(ang_r), -jnp.sin(ang_c), jnp.sin(ang_c)], axis=-1)
    return cos_t, sin_t


def _hyena_feats(n):
    t = jnp.arange(n, dtype=F32) / n
    bands = jnp.linspace(1e-4, HY_BANDS - 1, HY_BANDS, dtype=F32)
    phase = 2.0 * math.pi * t[:, None] * bands[None, :]
    feats = jnp.concatenate([t[:, None], jnp.cos(phase), -jnp.sin(phase)], axis=-1)
    return jnp.pad(feats, ((0, 0), (0, FFN_LANES - HY_POS_DIM)))


def _layer(x, ctx, mods, norm1_g, norm2_g, w_in, b_in, ret_decay_logit, ret_w_o, hy_conv_w,
           hy_conv_b, hy_ffn_w1, hy_ffn_b1, hy_ffn_freq, hy_ffn_w2, hy_ffn_b2, hy_ffn_w3, hy_skip,
           hy_w_o, w_out, router_group_w, router_group_b, router_expert_w, router_expert_b,
           expert_w1, expert_w3, expert_w2, final_norm_g):
    bsz, n, d = x.shape
    n_ctx = ctx.shape[1]
    mod_lat = mods[:bsz].reshape(bsz, 6, 1, d)
    sh1, sc1, gt1, sh2, sc2, gt2 = (mod_lat[:, s] for s in range(6))
    mod_ctx = mods[bsz].reshape(6, 1, 1, d)
    csh1 = jnp.broadcast_to(mod_ctx[0], (bsz, 1, d))
    csc1 = jnp.broadcast_to(mod_ctx[1], (bsz, 1, d))

    g1 = norm1_g.reshape(1, d)
    b_in2 = b_in.reshape(1, IN_W)
    w_in_b = w_in.astype(BF16)
    proj = _inproj(x, g1, sh1, sc1, w_in_b, b_in2, tm=min(512, n), tn=512)
    proj_c = _inproj(ctx, g1, csh1, csc1, w_in_b[:, COL_K:COL_G], b_in2[:, COL_K:COL_G],
                     tm=n_ctx, tn=512)

    cos_r, sin_r = _rope_tables(n)
    lgt = jnp.broadcast_to(ret_decay_logit.astype(F32).reshape(2 * RET_HEADS, 1),
                           (2 * RET_HEADS, RET_V_DIM))
    ret = _retention(proj, proj_c, cos_r, sin_r, lgt)

    slow = abs(math.log(HY_DECAY_TARGET)) / HY_SLOW_PCT
    fast = abs(math.log(HY_DECAY_TARGET)) / HY_FAST_PCT
    deltas = jnp.tile(jnp.linspace(slow, fast, HY_W, dtype=F32), 2).reshape(1, 2 * HY_W)
    fpad = FFN_LANES - HY_FFN
    row = lambda a: jnp.pad(a.reshape(1, HY_FFN), ((0, 0), (0, fpad)))
    fs, fd = _hyena_filters(n, _hyena_feats(n),
                            jnp.pad(hy_ffn_w1, ((0, FFN_LANES - HY_POS_DIM), (0, fpad))), row(hy_ffn_b1),
                            row(hy_ffn_freq), jnp.pad(hy_ffn_w2, ((0, fpad), (0, fpad))), row(hy_ffn_b2),
                            jnp.pad(hy_ffn_w3, ((0, fpad), (0, 0))), deltas)
    z, x0 = _hyena_pre(proj, hy_conv_w, hy_conv_b)
    fac_fwd = _dft_factors(n, inverse=False)
    fac_inv = _dft_factors(n, inverse=True)
    tile = min(512, n)
    spec_a, spec_b = _dft_filters(fac_fwd, fs, fd, tf=tile, tk=min(2048, n))
    tn = min(2048, bsz * HY_W)
    u, v = _dft_fwd(fac_fwd, z, spec_a, spec_b, tf=tile, tn=tn, tk=tile)
    skip_t = jnp.tile(hy_skip.reshape(1, HY_W), (1, bsz))
    hy = _dft_inv(fac_inv, u, v, z, x0, skip_t, tt=tile, tn=tn, tk=tile)

    rw = jnp.zeros((d, ROUTE_LANES), F32)
    rw = rw.at[:, :N_GROUPS].set(router_group_w).at[:, EXPERT_LANE0:EXPERT_LANE0 + N_EXPERTS].set(router_expert_w)
    rb = jnp.zeros((1, ROUTE_LANES), F32)
    rb = rb.at[0, :N_GROUPS].set(router_group_b).at[0, EXPERT_LANE0:EXPERT_LANE0 + N_EXPERTS].set(router_expert_b)
    rw_hi, rw_lo = _split_bf16(rw)
    sub = min(512, n)
    x1, h2, route, cnt = _merge(ret, hy, proj, x, gt1, sh2, sc2, norm2_g.reshape(1, d),
                                ret_w_o.astype(BF16), hy_w_o.astype(BF16), w_out.astype(BF16),
                                rw_hi, rw_lo, rb, tm=min(2 * sub, n), sub=sub)

    t_all = bsz * n
    counts = cnt[0, EXPERT_LANE0:EXPERT_LANE0 + N_EXPERTS].astype(jnp.int32)
    padded = (counts + MOE_BLK - 1) // MOE_BLK * MOE_BLK
    pad_end = jnp.cumsum(padded)
    pad_start = pad_end - padded
    route2 = route.reshape(t_all, ROUTE_LANES)
    e_ab = route2[:, 0:2].astype(jnp.int32)
    onehot = e_ab[:, :, None] == jnp.arange(N_EXPERTS, dtype=jnp.int32)[None, None, :]
    dest = jnp.sum(jnp.where(onehot, pad_start[None, None, :], 0), axis=-1) + route2[:, 4:6].astype(jnp.int32)
    dest_a, dest_b = dest[:, 0], dest[:, 1]
    n_blocks = -(-(2 * t_all + N_EXPERTS * (MOE_BLK - 1)) // MOE_BLK)
    blk0 = jnp.arange(n_blocks, dtype=jnp.int32) * MOE_BLK
    block_expert = jnp.minimum(jnp.sum(blk0[:, None] >= pad_end[None, :], axis=1), N_EXPERTS - 1).astype(jnp.int32)
    n_used = (pad_end[-1:] // MOE_BLK).astype(jnp.int32)

    xb = _dispatch(dest_a, dest_b, pad_end.astype(jnp.int32), h2.reshape(t_all, d),
                   n_blocks * MOE_BLK, tm=min(256, n))
    yb = _experts(block_expert, n_used, xb, expert_w1, expert_w3, expert_w2)
    return _combine(dest_a, dest_b, x1, route, gt2, final_norm_g.reshape(1, d), yb, tm=min(256, n))


def kernel(x, c, ctx, c_ctx, ada_w, ada_b, norm1_g, norm2_g, w_in, b_in, ret_decay_logit, ret_w_o, hy_conv_w, hy_conv_b, hy_ffn_w1, hy_ffn_b1, hy_ffn_freq, hy_ffn_w2, hy_ffn_b2, hy_ffn_w3, hy_skip, hy_w_o, w_out, router_group_w, router_group_b, router_expert_w, router_expert_b, expert_w1, expert_w3, expert_w2, final_norm_g):
    depth = ada_w.shape[0]
    assert depth == 1, "single-layer problem: the context stream is only read by the retention states"
    bsz, d = c.shape
    rows = -(-(bsz + 1) // 8) * 8
    cc = jnp.zeros((rows, d), F32).at[:bsz].set(c).at[bsz].set(c_ctx)
    mods = _adaln(cc, ada_w[0], ada_b[0].reshape(1, -1))
    return _layer(x, ctx, mods, norm1_g[0], norm2_g[0], w_in[0], b_in[0], ret_decay_logit[0],
                  ret_w_o[0], hy_conv_w[0], hy_conv_b[0], hy_ffn_w1[0], hy_ffn_b1[0], hy_ffn_freq[0],
                  hy_ffn_w2[0], hy_ffn_b2[0], hy_ffn_w3[0], hy_skip[0], hy_w_o[0], w_out[0],
                  router_group_w[0], router_group_b[0], router_expert_w[0], router_expert_b[0],
                  expert_w1[0], expert_w3[0], expert_w2[0], final_norm_g)
```

```python
import functools
import math

import jax
import jax.numpy as jnp
from jax import lax
from jax.experimental import pallas as pl
from jax.experimental.pallas import tpu as pltpu

F32 = jnp.float32
BF16 = jnp.bfloat16

D_MODEL = 1024
EPS = 1e-6
GRID_W = 64
ROPE_BASE = 10000.0

RET_HEADS = 4
RET_QK_DIM = 128
RET_V_DIM = 256
RET_CHUNK = 256
RET_UNROLL = 2
QK_W = RET_HEADS * RET_QK_DIM
V_W = RET_HEADS * RET_V_DIM

HY_W = 512
HY_POS_DIM = 33
HY_BANDS = (HY_POS_DIM - 1) // 2
HY_FFN = 64
FFN_LANES = 128
HY_DECAY_TARGET = 1e-2
HY_FAST_PCT = 0.3
HY_SLOW_PCT = 1.5
DFT_GROUP = 64

N_GROUPS = 4
EXPERTS_PER_GROUP = 8
N_EXPERTS = N_GROUPS * EXPERTS_PER_GROUP
EXPERT_HIDDEN = 512
ROUTE_LANES = 128
EXPERT_LANE0 = N_GROUPS
MOE_BLK = 512
SUBLANES = 8

IN_W = 2 * QK_W + 2 * V_W + 3 * HY_W + 2 * D_MODEL
COL_Q, COL_K, COL_V, COL_G = 0, QK_W, 2 * QK_W, 2 * QK_W + V_W
COL_HY = 2 * QK_W + 2 * V_W
COL_GR = COL_HY + 3 * HY_W
COL_GH = COL_GR + D_MODEL

VMEM_LIMIT = 56 * 1024 * 1024
NEG = -1e30


def _cparams(*sem):
    return pltpu.CompilerParams(dimension_semantics=sem, vmem_limit_bytes=VMEM_LIMIT)


def _sigmoid(x):
    return 1.0 / (1.0 + jnp.exp(-x))


def _dot(a, b):
    return jnp.dot(a, b, preferred_element_type=F32)


def _dot_t0(a, b):
    return lax.dot_general(a, b, (((0,), (0,)), ((), ())), preferred_element_type=F32)


def _dot_nt(a, b):
    return lax.dot_general(a, b, (((1,), (1,)), ((), ())), preferred_element_type=F32)


def _split_bf16(a):
    hi = a.astype(BF16)
    lo = (a - hi.astype(F32)).astype(BF16)
    return hi, lo


def _pack_halves(x):
    w = x.shape[1] // 2
    return pltpu.pack_elementwise([x[:, :w], x[:, w:]], packed_dtype=BF16)


def _unpack_halves(p):
    lo = pltpu.unpack_elementwise(p, index=0, packed_dtype=BF16, unpacked_dtype=F32)
    hi = pltpu.unpack_elementwise(p, index=1, packed_dtype=BF16, unpacked_dtype=F32)
    return jnp.concatenate([lo, hi], axis=1)


def _dot3(a, b):
    ah, al = _split_bf16(a)
    bh, bl = _split_bf16(b)
    return _dot(ah, bh) + _dot(al, bh) + _dot(ah, bl)


def _adaln_kernel(c_ref, w_ref, b_ref, o_ref):
    c = c_ref[...]
    o_ref[...] = _dot3(c * _sigmoid(c), w_ref[...]) + b_ref[...]


def _adaln(cc, w, b):
    rows, d = cc.shape
    n = w.shape[1]
    tn = 1536
    return pl.pallas_call(
        _adaln_kernel,
        grid=(n // tn,),
        in_specs=[pl.BlockSpec((rows, d), lambda j: (0, 0)),
                  pl.BlockSpec((d, tn), lambda j: (0, j)),
                  pl.BlockSpec((1, tn), lambda j: (0, j))],
        out_specs=pl.BlockSpec((rows, tn), lambda j: (0, j)),
        out_shape=jax.ShapeDtypeStruct((rows, n), F32),
        compiler_params=_cparams("arbitrary"),
        name="adaln",
    )(cc, w, b)


def _inproj_kernel(x_ref, g_ref, sh_ref, sc_ref, w_ref, b_ref, o_ref, *, tn):
    x = x_ref[0]
    y = x * lax.rsqrt(jnp.mean(x * x, axis=-1, keepdims=True) + EPS) * g_ref[...]
    h = (y * (1.0 + sc_ref[0]) + sh_ref[0]).astype(BF16)
    for j in range(w_ref.shape[1] // tn):
        cols = slice(j * tn, (j + 1) * tn)
        o_ref[0, :, cols] = (_dot(h, w_ref[:, cols]) + b_ref[:, cols]).astype(o_ref.dtype)


def _inproj(x, gain, shift, scale, w, b, tm, tn):
    bsz, n, d = x.shape
    nw = w.shape[1]
    return pl.pallas_call(
        functools.partial(_inproj_kernel, tn=tn),
        grid=(bsz, n // tm),
        in_specs=[pl.BlockSpec((1, tm, d), lambda bi, i: (bi, i, 0)),
                  pl.BlockSpec((1, d), lambda bi, i: (0, 0)),
                  pl.BlockSpec((1, 1, d), lambda bi, i: (bi, 0, 0)),
                  pl.BlockSpec((1, 1, d), lambda bi, i: (bi, 0, 0)),
                  pl.BlockSpec((d, nw), lambda bi, i: (0, 0)),
                  pl.BlockSpec((1, nw), lambda bi, i: (0, 0))],
        out_specs=pl.BlockSpec((1, tm, nw), lambda bi, i: (bi, i, 0)),
        out_shape=jax.ShapeDtypeStruct((bsz, n, nw), BF16),
        compiler_params=_cparams("arbitrary", "arbitrary"),
        name="inproj",
    )(x, gain, shift, scale, w, b)


def _log_sigmoid(x):
    return jnp.minimum(x, 0.0) - jnp.log(1.0 + jnp.exp(-jnp.abs(x)))


def _rope_partner(x):
    lane = lax.broadcasted_iota(jnp.int32, x.shape, 1)
    return jnp.where((lane % 64) < 32, pltpu.roll(x, 96, axis=1), pltpu.roll(x, 32, axis=1))


def _retention_kernel(q_ref, k_ref, v_ref, g_ref, kc_ref, vc_ref, cos_ref, sin_ref, lgt_ref,
                      o_ref, qr_scr, kr_scr, oacc_scr, sf_scr, sb_scr, *, n_tok, n_ctx):
    c_len = RET_CHUNK
    n_chunks = n_tok // c_len
    head = pl.program_id(1)
    k_scale = RET_QK_DIM ** -0.5

    lg_f = _log_sigmoid(lgt_ref[pl.ds(head, 1), :])
    lg_b = _log_sigmoid(lgt_ref[pl.ds(RET_HEADS + head, 1), :])
    lgf_k, lgb_k = lg_f[:, :RET_QK_DIM], lg_b[:, :RET_QK_DIM]

    ii = lax.broadcasted_iota(jnp.int32, (c_len, c_len), 0)
    jj = lax.broadcasted_iota(jnp.int32, (c_len, c_len), 1)
    dif = (ii - jj).astype(F32)
    decay_in = jnp.where(ii >= jj, jnp.exp(lg_f[:, :c_len] * jnp.maximum(dif, 0.0)), 0.0) + \
        jnp.where(jj > ii, jnp.exp(lg_b[:, :c_len] * jnp.maximum(-dif, 0.0)), 0.0)
    pos_k = lax.broadcasted_iota(jnp.int32, (c_len, RET_QK_DIM), 0).astype(F32)
    pos_v = lax.broadcasted_iota(jnp.int32, (c_len, RET_V_DIM), 0).astype(F32)
    dq_f = jnp.exp(lg_f * (pos_v + 1.0))
    dq_b = jnp.exp(lg_b * (c_len - pos_v))
    dk_f = jnp.exp(lgf_k * (c_len - 1.0 - pos_k))
    dk_b = jnp.exp(lgb_k * pos_k)
    dchunk_f = jnp.exp(lg_f * float(c_len))
    dchunk_b = jnp.exp(lg_b * float(c_len))

    pos_c = lax.broadcasted_iota(jnp.int32, (n_ctx, RET_QK_DIM), 0).astype(F32)
    kc = kc_ref[0].astype(F32) * k_scale
    vc = vc_ref[0]
    sf_scr[...] = _dot_t0((kc * jnp.exp(lgf_k * (n_ctx - 1.0 - pos_c))).astype(BF16), vc)
    sb_scr[...] = _dot_t0((kc * jnp.exp(lgb_k * pos_c)).astype(BF16), vc)

    def fwd(c, carry):
        r0 = pl.multiple_of(c * c_len, c_len)
        rows = pl.ds(r0, c_len)
        cos = cos_ref[rows, :]
        sin = sin_ref[rows, :]
        q = q_ref[0, rows, :].astype(F32)
        k = k_ref[0, rows, :].astype(F32)
        qr = q * cos + _rope_partner(q) * sin
        kr = (k * cos + _rope_partner(k) * sin) * k_scale
        qb = qr.astype(BF16)
        kb = kr.astype(BF16)
        qr_scr[rows, :] = qb
        kr_scr[rows, :] = kr
        v = v_ref[0, rows, :]
        scores = _dot_nt(qb, kb) * decay_in
        o = _dot(scores.astype(BF16), v) + _dot(qb, sf_scr[...].astype(BF16)) * dq_f
        oacc_scr[rows, :] = o
        sf_scr[...] = sf_scr[...] * dchunk_f + _dot_t0((kr * dk_f).astype(BF16), v)
        return carry

    lax.fori_loop(0, n_chunks, fwd, 0, unroll=RET_UNROLL)

    def bwd(t, carry):
        c = n_chunks - 1 - t
        r0 = pl.multiple_of(c * c_len, c_len)
        rows = pl.ds(r0, c_len)
        v = v_ref[0, rows, :]
        o = oacc_scr[rows, :] + _dot(qr_scr[rows, :], sb_scr[...].astype(BF16)) * dq_b
        o = o * lax.rsqrt(jnp.mean(o * o, axis=-1, keepdims=True) + EPS)
        g = g_ref[0, rows, :].astype(F32)
        o_ref[0, rows, :] = (g * _sigmoid(g) * o).astype(o_ref.dtype)
        sb_scr[...] = sb_scr[...] * dchunk_b + _dot_t0((kr_scr[rows, :] * dk_b).astype(BF16), v)
        return carry

    lax.fori_loop(0, n_chunks, bwd, 0, unroll=RET_UNROLL)


def _retention(proj, proj_c, cos_t, sin_t, lgt):
    bsz, n, _ = proj.shape
    n_ctx = proj_c.shape[1]
    kq, kv = RET_QK_DIM, RET_V_DIM
    assert n % RET_CHUNK == 0 and RET_CHUNK <= kv
    kern = functools.partial(_retention_kernel, n_tok=n, n_ctx=n_ctx)
    return pl.pallas_call(
        kern,
        grid=(bsz, RET_HEADS),
        in_specs=[pl.BlockSpec((1, n, kq), lambda b, h: (b, 0, COL_Q // kq + h)),
                  pl.BlockSpec((1, n, kq), lambda b, h: (b, 0, COL_K // kq + h)),
                  pl.BlockSpec((1, n, kv), lambda b, h: (b, 0, COL_V // kv + h)),
                  pl.BlockSpec((1, n, kv), lambda b, h: (b, 0, COL_G // kv + h)),
                  pl.BlockSpec((1, n_ctx, kq), lambda b, h: (b, 0, h)),
                  pl.BlockSpec((1, n_ctx, kv), lambda b, h: (b, 0, QK_W // kv + h)),
                  pl.BlockSpec((n, kq), lambda b, h: (0, 0)),
                  pl.BlockSpec((n, kq), lambda b, h: (0, 0)),
                  pl.BlockSpec((2 * RET_HEADS, kv), lambda b, h: (0, 0))],
        out_specs=pl.BlockSpec((1, n, kv), lambda b, h: (b, 0, h)),
        out_shape=jax.ShapeDtypeStruct((bsz, n, V_W), BF16),
        scratch_shapes=[pltpu.VMEM((n, kq), BF16), pltpu.VMEM((n, kq), F32),
                        pltpu.VMEM((n, kv), F32), pltpu.VMEM((kq, kv), F32),
                        pltpu.VMEM((kq, kv), F32)],
        compiler_params=_cparams("arbitrary", "arbitrary"),
        name="retention",
    )(proj, proj, proj, proj, proj_c, proj_c, cos_t, sin_t, lgt)


def _filter_kernel(feats_ref, w1_ref, b1_ref, fr_ref, w2_ref, b2_ref, w3f_ref, w3b_ref,
                   df_ref, db_ref, fs_ref, fd_ref, *, n_tok):
    fr = fr_ref[...]
    hid = jnp.sin(fr * (_dot3(feats_ref[...], w1_ref[...]) + b1_ref[...]))
    hid = jnp.sin(fr * (_dot3(hid, w2_ref[...]) + b2_ref[...]))
    t = lax.broadcasted_iota(jnp.int32, (n_tok, df_ref.shape[1]), 0).astype(F32) / n_tok

    def one(w3_ref, d_ref):
        f = _dot3(hid, w3_ref[...]) * jnp.exp(-t * d_ref[...])
        return f / jnp.sum(jnp.abs(f), axis=0, keepdims=True)

    hf = one(w3f_ref, df_ref)
    hb = one(w3b_ref, db_ref)
    fs_ref[...] = (hf + hb).astype(fs_ref.dtype)
    fd_ref[...] = (hf - hb).astype(fd_ref.dtype)


def _hyena_filters(n, feats, w1, b1, freq, w2, b2, w3, deltas):
    tc = HY_W
    nf = feats.shape[1]
    kern = functools.partial(_filter_kernel, n_tok=n)
    full = lambda shape: pl.BlockSpec(shape, lambda j: (0, 0))
    return pl.pallas_call(
        kern,
        grid=(HY_W // tc,),
        in_specs=[full((n, nf)), full((nf, FFN_LANES)), full((1, FFN_LANES)), full((1, FFN_LANES)),
                  full((FFN_LANES, FFN_LANES)), full((1, FFN_LANES)),
                  pl.BlockSpec((FFN_LANES, tc), lambda j: (0, j)),
                  pl.BlockSpec((FFN_LANES, tc), lambda j: (0, HY_W // tc + j)),
                  pl.BlockSpec((1, tc), lambda j: (0, j)),
                  pl.BlockSpec((1, tc), lambda j: (0, HY_W // tc + j))],
        out_specs=[pl.BlockSpec((n, tc), lambda j: (0, j)),
                   pl.BlockSpec((n, tc), lambda j: (0, j))],
        out_shape=[jax.ShapeDtypeStruct((n, HY_W), BF16)] * 2,
        compiler_params=_cparams("arbitrary"),
        name="hyena_filters",
    )(feats, w1, b1, freq, w2, b2, w3, w3, deltas, deltas)


def _hyena_pre_kernel(u0_ref, u1_ref, u2_ref, w_ref, b_ref, z_ref, x0_ref, *, n_tok, rows):
    n_steps = n_tok // rows
    tc = z_ref.shape[1]
    halo = 16
    rid = lax.broadcasted_iota(jnp.int32, (rows, tc), 0)

    def conv(u_ref, part, r0, has_prev, has_next):
        x = u_ref[0, pl.ds(r0, rows), :].astype(F32)
        prev_g = u_ref[0, pl.ds(pl.multiple_of(jnp.maximum(r0 - halo, 0), halo), halo), :].astype(F32)
        next_g = u_ref[0, pl.ds(pl.multiple_of(jnp.minimum(r0 + rows, n_tok - halo), halo), halo), :].astype(F32)
        prev_row = jnp.where(has_prev, prev_g[halo - 1:halo, :], 0.0)
        next_row = jnp.where(has_next, next_g[0:1, :], 0.0)
        up = jnp.where(rid == 0, prev_row, pltpu.roll(x, 1, axis=0))
        dn = jnp.where(rid == rows - 1, next_row, pltpu.roll(x, rows - 1, axis=0))
        w = w_ref[part]
        return up * w[0:1, :] + x * w[1:2, :] + dn * w[2:3, :] + b_ref[part]

    def body(s, carry):
        r0 = pl.multiple_of(s * rows, rows)
        has_prev = s > 0
        has_next = s < n_steps - 1
        x0 = conv(u0_ref, 0, r0, has_prev, has_next)
        x1 = conv(u1_ref, 1, r0, has_prev, has_next)
        vv = conv(u2_ref, 2, r0, has_prev, has_next)
        z_ref[pl.ds(r0, rows), :] = (x1 * vv).astype(z_ref.dtype)
        x0_ref[pl.ds(r0, rows), :] = x0.astype(x0_ref.dtype)
        return carry

    lax.fori_loop(0, n_steps, body, 0)


def _hyena_pre(proj, conv_w, conv_b):
    bsz, n, _ = proj.shape
    tc = 256
    nj = HY_W // tc
    rows = min(512, n)
    base = COL_HY // tc
    kern = functools.partial(_hyena_pre_kernel, n_tok=n, rows=rows)
    u_spec = lambda part: pl.BlockSpec((1, n, tc), lambda b, j: (b, 0, base + part * nj + j))
    wp = jnp.zeros((3, 8, HY_W), F32).at[:, :3, :].set(conv_w.reshape(3, 3, HY_W).transpose(1, 0, 2))
    bp = conv_b.reshape(3, 1, HY_W)
    return pl.pallas_call(
        kern,
        grid=(bsz, nj),
        in_specs=[u_spec(0), u_spec(1), u_spec(2),
                  pl.BlockSpec((3, 8, tc), lambda b, j: (0, 0, j)),
                  pl.BlockSpec((3, 1, tc), lambda b, j: (0, 0, j))],
        out_specs=[pl.BlockSpec((n, tc), lambda b, j: (0, b * nj + j)),
                   pl.BlockSpec((n, tc), lambda b, j: (0, b * nj + j))],
        out_shape=[jax.ShapeDtypeStruct((n, bsz * HY_W), BF16)] * 2,
        compiler_params=_cparams("arbitrary", "arbitrary"),
        name="hyena_pre",
    )(proj, proj, proj, wp, bp)


def _dft_factors(n, inverse):
    period = 4 * n
    col = jnp.arange(n, dtype=jnp.int32)[None, :]
    hi = jnp.arange(n // DFT_GROUP, dtype=jnp.int32)[:, None]
    lo = jnp.arange(DFT_GROUP, dtype=jnp.int32)[:, None]
    if inverse:
        m_a, m_b = (2 * col + 1) * (DFT_GROUP * hi), (2 * col + 1) * lo
    else:
        m_a, m_b = (2 * DFT_GROUP * hi) * col, (2 * lo + 1) * col
    ang_a = (m_a % period).astype(F32) * (2.0 * math.pi / period)
    ang_b = (m_b % period).astype(F32) * (2.0 * math.pi / period)
    return jnp.cos(ang_a), jnp.sin(ang_a), jnp.cos(ang_b), jnp.sin(ang_b)


def _dft_tile(ca_ref, sa_ref, cb_ref, sb_ref):
    cb = cb_ref[...]
    sb = sb_ref[...]
    cos_rows, sin_rows = [], []
    for g in range(ca_ref.shape[0]):
        ca = ca_ref[g:g + 1, :]
        sa = sa_ref[g:g + 1, :]
        cos_rows.append((ca * cb - sa * sb).astype(BF16))
        sin_rows.append((sa * cb + ca * sb).astype(BF16))
    return jnp.concatenate(cos_rows, axis=0), jnp.concatenate(sin_rows, axis=0)


def _factor_specs(tr, tc, row_axis, col_axis):
    a_spec = pl.BlockSpec((tr // DFT_GROUP, tc), lambda *ids: (ids[row_axis], ids[col_axis]))
    b_spec = pl.BlockSpec((DFT_GROUP, tc), lambda *ids: (0, ids[col_axis]))
    return [a_spec, a_spec, b_spec, b_spec]


def _dft_filter_kernel(ca_ref, sa_ref, cb_ref, sb_ref, fs_ref, fd_ref, a_ref, b_ref, acc_a, acc_b):
    kk = pl.program_id(1)

    @pl.when(kk == 0)
    def _():
        acc_a[...] = jnp.zeros_like(acc_a)
        acc_b[...] = jnp.zeros_like(acc_b)

    cos_t, sin_t = _dft_tile(ca_ref, sa_ref, cb_ref, sb_ref)
    acc_a[...] += _dot(cos_t, fs_ref[...])
    acc_b[...] += _dot(sin_t, fd_ref[...])

    @pl.when(kk == pl.num_programs(1) - 1)
    def _():
        a_ref[...] = acc_a[...]
        b_ref[...] = acc_b[...]


def _dft_filters(factors, fs, fd, tf, tk):
    n, w = fs.shape
    return pl.pallas_call(
        _dft_filter_kernel,
        grid=(n // tf, n // tk),
        in_specs=_factor_specs(tf, tk, 0, 1) + [
            pl.BlockSpec((tk, w), lambda i, kk: (kk, 0)),
            pl.BlockSpec((tk, w), lambda i, kk: (kk, 0))],
        out_specs=[pl.BlockSpec((tf, w), lambda i, kk: (i, 0)),
                   pl.BlockSpec((tf, w), lambda i, kk: (i, 0))],
        out_shape=[jax.ShapeDtypeStruct((n, w), F32)] * 2,
        scratch_shapes=[pltpu.VMEM((tf, w), F32), pltpu.VMEM((tf, w), F32)],
        compiler_params=_cparams("arbitrary", "arbitrary"),
        name="dft_filters",
    )(*factors, fs, fd)


def _dft_fwd_kernel(ca_ref, sa_ref, cb_ref, sb_ref, z_ref, a_ref, b_ref, u_ref, v_ref, acc_p, acc_q):
    kk = pl.program_id(2)

    @pl.when(kk == 0)
    def _():
        acc_p[...] = jnp.zeros_like(acc_p)
        acc_q[...] = jnp.zeros_like(acc_q)

    cos_t, sin_t = _dft_tile(ca_ref, sa_ref, cb_ref, sb_ref)
    z = z_ref[...]
    acc_p[...] += _dot(cos_t, z)
    acc_q[...] += _dot(sin_t, z)

    @pl.when(kk == pl.num_programs(2) - 1)
    def _():
        a = a_ref[...]
        b = b_ref[...]
        w = a.shape[1]
        for s in range(u_ref.shape[1] // w):
            cols = slice(s * w, (s + 1) * w)
            p = acc_p[:, cols]
            q = acc_q[:, cols]
            u_ref[:, cols] = (p * a - q * b).astype(u_ref.dtype)
            v_ref[:, cols] = (p * b + q * a).astype(v_ref.dtype)


def _dft_fwd(factors, z, spec_a, spec_b, tf, tn, tk):
    n, ncol = z.shape
    w = spec_a.shape[1]
    return pl.pallas_call(
        _dft_fwd_kernel,
        grid=(n // tf, ncol // tn, n // tk),
        in_specs=_factor_specs(tf, tk, 0, 2) + [
            pl.BlockSpec((tk, tn), lambda i, j, kk: (kk, j)),
            pl.BlockSpec((tf, w), lambda i, j, kk: (i, 0)),
            pl.BlockSpec((tf, w), lambda i, j, kk: (i, 0))],
        out_specs=[pl.BlockSpec((tf, tn), lambda i, j, kk: (i, j)),
                   pl.BlockSpec((tf, tn), lambda i, j, kk: (i, j))],
        out_shape=[jax.ShapeDtypeStruct((n, ncol), BF16)] * 2,
        scratch_shapes=[pltpu.VMEM((tf, tn), F32), pltpu.VMEM((tf, tn), F32)],
        compiler_params=_cparams("arbitrary", "arbitrary", "arbitrary"),
        name="dft_fwd",
    )(*factors, z, spec_a, spec_b)


def _dft_inv_kernel(ca_ref, sa_ref, cb_ref, sb_ref, u_ref, v_ref, z_ref, x0_ref, skip_ref, o_ref, acc,
                    *, inv_n):
    kk = pl.program_id(2)

    @pl.when(kk == 0)
    def _():
        acc[...] = jnp.zeros_like(acc)

    cos_t, sin_t = _dft_tile(ca_ref, sa_ref, cb_ref, sb_ref)
    acc[...] += _dot(cos_t, u_ref[...]) + _dot(sin_t, v_ref[...])

    @pl.when(kk == pl.num_programs(2) - 1)
    def _():
        y = acc[...] * inv_n + z_ref[...].astype(F32) * skip_ref[...]
        o_ref[...] = (x0_ref[...].astype(F32) * y).astype(o_ref.dtype)


def _dft_inv(factors, u, v, z, x0, skip_t, tt, tn, tk):
    n, ncol = u.shape
    kern = functools.partial(_dft_inv_kernel, inv_n=1.0 / n)
    return pl.pallas_call(
        kern,
        grid=(n // tt, ncol // tn, n // tk),
        in_specs=_factor_specs(tt, tk, 0, 2) + [
                  pl.BlockSpec((tk, tn), lambda i, j, kk: (kk, j)),
                  pl.BlockSpec((tk, tn), lambda i, j, kk: (kk, j)),
                  pl.BlockSpec((tt, tn), lambda i, j, kk: (i, j)),
                  pl.BlockSpec((tt, tn), lambda i, j, kk: (i, j)),
                  pl.BlockSpec((1, tn), lambda i, j, kk: (0, j))],
        out_specs=pl.BlockSpec((tt, tn), lambda i, j, kk: (i, j)),
        out_shape=jax.ShapeDtypeStruct((n, ncol), BF16),
        scratch_shapes=[pltpu.VMEM((tt, tn), F32)],
        compiler_params=_cparams("arbitrary", "arbitrary", "arbitrary"),
        name="dft_inv",
    )(*factors, u, v, z, x0, skip_t)


def _merge_kernel(ret_ref, hy_ref, gr0_ref, gr1_ref, gh0_ref, gh1_ref, x_ref, gt1_ref, sh2_ref,
                  sc2_ref, g2_ref, wro_ref, who_ref, wout_ref, rwh_ref, rwl_ref, rb_ref,
                  x1_ref, h2_ref, route_ref, cnt_ref, base_scr, *, sub):
    first = jnp.logical_and(pl.program_id(0) == 0, pl.program_id(1) == 0)

    @pl.when(first)
    def _():
        base_scr[...] = jnp.zeros_like(base_scr)

    for s0 in range(0, x_ref.shape[1], sub):
        _merge_rows(slice(s0, s0 + sub), ret_ref, hy_ref, gr0_ref, gr1_ref, gh0_ref, gh1_ref, x_ref,
                    gt1_ref, sh2_ref, sc2_ref, g2_ref, wro_ref, who_ref, wout_ref, rwh_ref, rwl_ref,
                    rb_ref, x1_ref, h2_ref, route_ref, base_scr)
    cnt_ref[...] = base_scr[...]


def _merge_rows(rows, ret_ref, hy_ref, gr0_ref, gr1_ref, gh0_ref, gh1_ref, x_ref, gt1_ref, sh2_ref,
                sc2_ref, g2_ref, wro_ref, who_ref, wout_ref, rwh_ref, rwl_ref, rb_ref,
                x1_ref, h2_ref, route_ref, base_scr):
    tm = rows.stop - rows.start
    gate_r = jnp.concatenate([gr0_ref[0, rows, :], gr1_ref[0, rows, :]], axis=1).astype(F32)
    gate_h = jnp.concatenate([gh0_ref[0, rows, :], gh1_ref[0, rows, :]], axis=1).astype(F32)
    mixed = _sigmoid(gate_r) * _dot(ret_ref[0, rows, :], wro_ref[...]) + \
        _sigmoid(gate_h) * _dot(hy_ref[rows, :], who_ref[...])
    x1 = x_ref[0, rows, :] + gt1_ref[0] * _dot(mixed.astype(BF16), wout_ref[...])
    x1_ref[0, rows, :] = x1
    h2 = x1 * lax.rsqrt(jnp.mean(x1 * x1, axis=-1, keepdims=True) + EPS) * g2_ref[...]
    h2 = h2 * (1.0 + sc2_ref[0]) + sh2_ref[0]
    h2_ref[0, rows, :] = _pack_halves(h2)

    h2_hi, h2_lo = _split_bf16(h2)
    rw_hi = rwh_ref[...]
    logits = _dot(h2_hi, rw_hi) + _dot(h2_lo, rw_hi) + _dot(h2_hi, rwl_ref[...]) + rb_ref[...]
    lane = lax.broadcasted_iota(jnp.int32, logits.shape, 1)
    lane_f = lane.astype(F32)
    big = float(ROUTE_LANES)

    def first_lane(mask):
        return jnp.min(jnp.where(mask, lane_f, big), axis=1, keepdims=True)

    is_group = lane < N_GROUPS
    gl = jnp.where(is_group, logits, NEG)
    ge = jnp.where(is_group, jnp.exp(gl - jnp.max(gl, axis=1, keepdims=True)), 0.0)
    group_p = ge / jnp.sum(ge, axis=1, keepdims=True)
    p_star = jnp.max(group_p, axis=1, keepdims=True)
    g_star = first_lane(jnp.logical_and(is_group, group_p == p_star))
    lo = EXPERT_LANE0 + g_star * EXPERTS_PER_GROUP
    in_group = jnp.logical_and(lane_f >= lo, lane_f < lo + EXPERTS_PER_GROUP)
    el = jnp.where(in_group, logits, NEG)
    ee = jnp.where(in_group, jnp.exp(el - jnp.max(el, axis=1, keepdims=True)), 0.0)
    sp = jnp.where(in_group, ee / jnp.sum(ee, axis=1, keepdims=True), -1.0)
    w_a = jnp.max(sp, axis=1, keepdims=True)
    l_a = first_lane(sp == w_a)
    sp2 = jnp.where(lane_f == l_a, -1.0, sp)
    w_b = jnp.max(sp2, axis=1, keepdims=True)
    l_b = first_lane(sp2 == w_b)
    wsum = w_a + w_b
    wt_a = p_star * w_a / wsum
    wt_b = p_star * w_b / wsum

    hit_a = lane_f == l_a
    hit_b = lane_f == l_b
    onehot = jnp.where(jnp.logical_or(hit_a, hit_b), 1.0, 0.0)
    ri = lax.broadcasted_iota(jnp.int32, (tm, tm), 0)
    ci = lax.broadcasted_iota(jnp.int32, (tm, tm), 1)
    tri = jnp.where(ri > ci, 1.0, 0.0).astype(BF16)
    before = _dot(tri, onehot.astype(BF16)) + base_scr[0:1, :]
    rank_a = jnp.sum(jnp.where(hit_a, before, 0.0), axis=1, keepdims=True)
    rank_b = jnp.sum(jnp.where(hit_b, before, 0.0), axis=1, keepdims=True)
    base_scr[0:1, :] = base_scr[0:1, :] + jnp.sum(onehot, axis=0, keepdims=True)

    vals = (l_a - EXPERT_LANE0, l_b - EXPERT_LANE0, wt_a, wt_b, rank_a, rank_b)
    route = jnp.zeros(logits.shape, F32)
    for idx, val in enumerate(vals):
        route = jnp.where(lane == idx, val, route)
    route_ref[0, rows, :] = route


def _merge(ret, hy, proj, x, gt1, sh2, sc2, g2, w_ro, w_ho, w_out, rw_hi, rw_lo, rb, tm, sub):
    bsz, n, d = x.shape
    nt = n // tm
    hw = HY_W
    gspec = lambda col: pl.BlockSpec((1, tm, hw), lambda b, i: (b, i, col // hw))
    vec = pl.BlockSpec((1, 1, d), lambda b, i: (b, 0, 0))
    full = lambda shape: pl.BlockSpec(shape, lambda b, i: (0, 0))
    tok = pl.BlockSpec((1, tm, d), lambda b, i: (b, i, 0))
    return pl.pallas_call(
        functools.partial(_merge_kernel, sub=sub),
        grid=(bsz, nt),
        in_specs=[tok,
                  pl.BlockSpec((tm, hw), lambda b, i: (i, b)),
                  gspec(COL_GR), gspec(COL_GR + hw), gspec(COL_GH), gspec(COL_GH + hw),
                  tok, vec, vec, vec, full((1, d)),
                  full((V_W, d)), full((hw, d)), full((d, d)),
                  full((d, ROUTE_LANES)), full((d, ROUTE_LANES)), full((1, ROUTE_LANES))],
        out_specs=[tok, pl.BlockSpec((1, tm, d // 2), lambda b, i: (b, i, 0)),
                   pl.BlockSpec((1, tm, ROUTE_LANES), lambda b, i: (b, i, 0)),
                   pl.BlockSpec((8, ROUTE_LANES), lambda b, i: (0, 0))],
        out_shape=[jax.ShapeDtypeStruct((bsz, n, d), F32),
                   jax.ShapeDtypeStruct((bsz, n, d // 2), jnp.int32),
                   jax.ShapeDtypeStruct((bsz, n, ROUTE_LANES), F32),
                   jax.ShapeDtypeStruct((8, ROUTE_LANES), F32)],
        scratch_shapes=[pltpu.VMEM((8, ROUTE_LANES), F32)],
        compiler_params=_cparams("arbitrary", "arbitrary"),
        name="merge_route",
    )(ret, hy, proj, proj, proj, proj, x, gt1, sh2, sc2, g2, w_ro, w_ho, w_out, rw_hi, rw_lo, rb)


def _dispatch_kernel(da_ref, db_ref, pend_ref, h_ref, xb_ref, zero_scr, sem, zsem, *, tm):
    i = pl.program_id(0)

    def row_copy(g, u, dst_row):
        return pltpu.make_async_copy(h_ref.at[g, pl.ds(u, 1)], xb_ref.at[pl.ds(dst_row, 1)], sem)

    @pl.when(i == 0)
    def _():
        zero_scr[...] = jnp.zeros_like(zero_scr)

        def zcopy(e):
            return pltpu.make_async_copy(
                zero_scr, xb_ref.at[pl.ds(pl.multiple_of(pend_ref[e] - MOE_BLK, MOE_BLK), MOE_BLK)], zsem)

        def nonempty(e):
            prev = jnp.where(e > 0, pend_ref[jnp.maximum(e - 1, 0)], 0)
            return pend_ref[e] > prev

        for e in range(N_EXPERTS):
            @pl.when(nonempty(e))
            def _():
                zcopy(e).start()
        for e in range(N_EXPERTS):
            @pl.when(nonempty(e))
            def _():
                zcopy(e).wait()

        def tail_copy(blk):
            return pltpu.make_async_copy(
                zero_scr, xb_ref.at[pl.ds(pl.multiple_of(blk * MOE_BLK, MOE_BLK), MOE_BLK)], zsem)

        first_unused = pend_ref[N_EXPERTS - 1] // MOE_BLK
        n_blocks = xb_ref.shape[0] // MOE_BLK
        lax.fori_loop(first_unused, n_blocks, lambda blk, c: (tail_copy(blk).start(), c)[1], 0)
        lax.fori_loop(first_unused, n_blocks, lambda blk, c: (tail_copy(blk).wait(), c)[1], 0)

    def issue(g, carry):
        t0 = i * tm + g * SUBLANES
        for u in range(SUBLANES):
            row_copy(g, u, da_ref[t0 + u]).start(priority=0)
            row_copy(g, u, db_ref[t0 + u]).start(priority=1)
        return carry

    lax.fori_loop(0, tm // SUBLANES, issue, 0)

    def drain(g, carry):
        for _ in range(2 * SUBLANES):
            row_copy(0, 0, 0).wait()
        return carry

    lax.fori_loop(0, tm // SUBLANES, drain, 0)


def _dispatch(dest_a, dest_b, pad_end, h2, n_rows, tm):
    t_all, d = h2.shape
    kern = functools.partial(_dispatch_kernel, tm=tm)
    return pl.pallas_call(
        kern,
        grid_spec=pltpu.PrefetchScalarGridSpec(
            num_scalar_prefetch=3,
            grid=(t_all // tm,),
            in_specs=[pl.BlockSpec((tm // SUBLANES, SUBLANES, d), lambda i, *_: (i, 0, 0))],
            out_specs=pl.BlockSpec(memory_space=pl.ANY),
            scratch_shapes=[pltpu.VMEM((MOE_BLK, d), h2.dtype),
                            pltpu.SemaphoreType.DMA(()), pltpu.SemaphoreType.DMA(())]),
        out_shape=jax.ShapeDtypeStruct((n_rows, d), h2.dtype),
        compiler_params=_cparams("arbitrary"),
        name="moe_dispatch",
    )(dest_a, dest_b, pad_end, h2.reshape(t_all // SUBLANES, SUBLANES, d))


def _expert_kernel(be_ref, nu_ref, x_ref, w1_ref, w3_ref, w2_ref, o_ref, w1_scr, w3_scr, w2_scr):
    i = pl.program_id(0)

    @pl.when(jnp.logical_or(i == 0, be_ref[i] != be_ref[jnp.maximum(i - 1, 0)]))
    def _():
        w1_scr[...] = w1_ref[0].astype(BF16)
        w3_scr[...] = w3_ref[0].astype(BF16)
        w2_scr[...] = w2_ref[0].astype(BF16)

    @pl.when(i < nu_ref[0])
    def _():
        x = _unpack_halves(x_ref[...]).astype(BF16)
        a = _dot(x, w1_scr[...])
        b = _dot(x, w3_scr[...])
        o_ref[...] = _pack_halves(_dot((a * _sigmoid(a) * b).astype(BF16), w2_scr[...]))

    @pl.when(i >= nu_ref[0])
    def _():
        o_ref[...] = jnp.zeros_like(o_ref)


def _experts(block_expert, n_used, xb, w1, w3, w2):
    n_rows, dp = xb.shape
    _, d, hid = w1.shape
    row_blk = lambda i, be, nu: (jnp.minimum(i, nu[0] - 1), 0)
    return pl.pallas_call(
        _expert_kernel,
        grid_spec=pltpu.PrefetchScalarGridSpec(
            num_scalar_prefetch=2,
            grid=(n_rows // MOE_BLK,),
            in_specs=[pl.BlockSpec((MOE_BLK, dp), row_blk),
                      pl.BlockSpec((1, d, hid), lambda i, be, nu: (be[i], 0, 0)),
                      pl.BlockSpec((1, d, hid), lambda i, be, nu: (be[i], 0, 0)),
                      pl.BlockSpec((1, hid, d), lambda i, be, nu: (be[i], 0, 0))],
            out_specs=pl.BlockSpec((MOE_BLK, dp), lambda i, be, nu: (i, 0)),
            scratch_shapes=[pltpu.VMEM((d, hid), BF16), pltpu.VMEM((d, hid), BF16),
                            pltpu.VMEM((hid, d), BF16)]),
        out_shape=jax.ShapeDtypeStruct((n_rows, dp), xb.dtype),
        compiler_params=_cparams("arbitrary"),
        name="moe_experts",
    )(block_expert, n_used, xb, w1, w3, w2)


def _combine_kernel(da_ref, db_ref, x1_ref, route_ref, gt2_ref, gf_ref, yb_ref, o_ref,
                    buf, sems, *, tm, tiles_per_batch):
    b = pl.program_id(0)
    i = pl.program_id(1)
    step = b * tiles_per_batch + i
    n_steps = pl.num_programs(0) * tiles_per_batch
    slot = step % 2

    def row_copy(src_row, s, which, g, u):
        return pltpu.make_async_copy(yb_ref.at[pl.ds(src_row, 1)], buf.at[s, which, g, pl.ds(u, 1)],
                                     sems.at[s])

    def issue_tile(tile, s):
        def body(g, carry):
            t0 = tile * tm + g * SUBLANES
            for u in range(SUBLANES):
                row_copy(da_ref[t0 + u], s, 0, g, u).start(priority=0)
                row_copy(db_ref[t0 + u], s, 1, g, u).start(priority=1)
            return carry
        lax.fori_loop(0, tm // SUBLANES, body, 0)

    @pl.when(step == 0)
    def _():
        issue_tile(0, 0)

    @pl.when(step + 1 < n_steps)
    def _():
        issue_tile(step + 1, 1 - slot)

    def drain(g, carry):
        for _ in range(SUBLANES):
            row_copy(0, slot, 0, 0, 0).wait()
            row_copy(0, slot, 1, 0, 0).wait()
        return carry

    lax.fori_loop(0, tm // SUBLANES, drain, 0)

    route = route_ref[0]
    dp = buf.shape[-1]
    y = route[:, 2:3] * _unpack_halves(buf[slot, 0].reshape(tm, dp)) + \
        route[:, 3:4] * _unpack_halves(buf[slot, 1].reshape(tm, dp))
    xo = x1_ref[0] + gt2_ref[0] * y
    o_ref[0] = xo * lax.rsqrt(jnp.mean(xo * xo, axis=-1, keepdims=True) + EPS) * gf_ref[...]


def _combine(dest_a, dest_b, x1, route, gt2, gf, yb, tm):
    bsz, n, d = x1.shape
    nt = n // tm
    kern = functools.partial(_combine_kernel, tm=tm, tiles_per_batch=nt)
    return pl.pallas_call(
        kern,
        grid_spec=pltpu.PrefetchScalarGridSpec(
            num_scalar_prefetch=2,
            grid=(bsz, nt),
            in_specs=[pl.BlockSpec((1, tm, d), lambda b, i, *_: (b, i, 0)),
                      pl.BlockSpec((1, tm, ROUTE_LANES), lambda b, i, *_: (b, i, 0)),
                      pl.BlockSpec((1, 1, d), lambda b, i, *_: (b, 0, 0)),
                      pl.BlockSpec((1, d), lambda b, i, *_: (0, 0)),
                      pl.BlockSpec(memory_space=pl.ANY)],
            out_specs=pl.BlockSpec((1, tm, d), lambda b, i, *_: (b, i, 0)),
            scratch_shapes=[pltpu.VMEM((2, 2, tm // SUBLANES, SUBLANES, yb.shape[1]), yb.dtype),
                            pltpu.SemaphoreType.DMA((2,))]),
        out_shape=jax.ShapeDtypeStruct((bsz, n, d), F32),
        compiler_params=_cparams("arbitrary", "arbitrary"),
        name="moe_combine",
    )(dest_a, dest_b, x1, route, gt2, gf, yb)


def _rope_tables(n):
    rows = n // GRID_W
    r, col = jnp.meshgrid(jnp.arange(rows, dtype=F32), jnp.arange(GRID_W, dtype=F32), indexing='ij')
    n_freq = RET_QK_DIM // 4
    inv_freq = ROPE_BASE ** (-jnp.arange(n_freq, dtype=F32) / n_freq)
    ang_r = r.reshape(-1)[:, None] * inv_freq
    ang_c = col.reshape(-1)[:, None] * inv_freq
    cos_t = jnp.concatenate([jnp.cos(ang_r), jnp.cos(ang_r), jnp.cos(ang_c), jnp.cos(ang_c)], axis=-1)
    sin_t = jnp.concatenate([-jnp.sin(ang_r), jnp.sin(ang_r), -jnp.sin(ang_c), jnp.sin(ang_c)], axis=-1)
    return cos_t, sin_t


def _hyena_feats(n):
    t = jnp.arange(n, dtype=F32) / n
    bands = jnp.linspace(1e-4, HY_BANDS - 1, HY_BANDS, dtype=F32)
    phase = 2.0 * math.pi * t[:, None] * bands[None, :]
    feats = jnp.concatenate([t[:, None], jnp.cos(phase), -jnp.sin(phase)], axis=-1)
    return jnp.pad(feats, ((0, 0), (0, FFN_LANES - HY_POS_DIM)))


def _layer(x, ctx, mods, norm1_g, norm2_g, w_in, b_in, ret_decay_logit, ret_w_o, hy_conv_w,
           hy_conv_b, hy_ffn_w1, hy_ffn_b1, hy_ffn_freq, hy_ffn_w2, hy_ffn_b2, hy_ffn_w3, hy_skip,
           hy_w_o, w_out, router_group_w, router_group_b, router_expert_w, router_expert_b,
           expert_w1, expert_w3, expert_w2, final_norm_g):
    bsz, n, d = x.shape
    n_ctx = ctx.shape[1]
    mod_lat = mods[:bsz].reshape(bsz, 6, 1, d)
    sh1, sc1, gt1, sh2, sc2, gt2 = (mod_lat[:, s] for s in range(6))
    mod_ctx = mods[bsz].reshape(6, 1, 1, d)
    csh1 = jnp.broadcast_to(mod_ctx[0], (bsz, 1, d))
    csc1 = jnp.broadcast_to(mod_ctx[1], (bsz, 1, d))

    g1 = norm1_g.reshape(1, d)
    b_in2 = b_in.reshape(1, IN_W)
    w_in_b = w_in.astype(BF16)
    proj = _inproj(x, g1, sh1, sc1, w_in_b, b_in2, tm=min(512, n), tn=512)
    proj_c = _inproj(ctx, g1, csh1, csc1, w_in_b[:, COL_K:COL_G], b_in2[:, COL_K:COL_G],
                     tm=n_ctx, tn=512)

    cos_r, sin_r = _rope_tables(n)
    lgt = jnp.broadcast_to(ret_decay_logit.astype(F32).reshape(2 * RET_HEADS, 1),
                           (2 * RET_HEADS, RET_V_DIM))
    ret = _retention(proj, proj_c, cos_r, sin_r, lgt)

    slow = abs(math.log(HY_DECAY_TARGET)) / HY_SLOW_PCT
    fast = abs(math.log(HY_DECAY_TARGET)) / HY_FAST_PCT
    deltas = jnp.tile(jnp.linspace(slow, fast, HY_W, dtype=F32), 2).reshape(1, 2 * HY_W)
    fpad = FFN_LANES - HY_FFN
    row = lambda a: jnp.pad(a.reshape(1, HY_FFN), ((0, 0), (0, fpad)))
    fs, fd = _hyena_filters(n, _hyena_feats(n),
                            jnp.pad(hy_ffn_w1, ((0, FFN_LANES - HY_POS_DIM), (0, fpad))), row(hy_ffn_b1),
                            row(hy_ffn_freq), jnp.pad(hy_ffn_w2, ((0, fpad), (0, fpad))), row(hy_ffn_b2),
                            jnp.pad(hy_ffn_w3, ((0, fpad), (0, 0))), deltas)
    z, x0 = _hyena_pre(proj, hy_conv_w, hy_conv_b)
    fac_fwd = _dft_factors(n, inverse=False)
    fac_inv = _dft_factors(n, inverse=True)
    tile = min(512, n)
    spec_a, spec_b = _dft_filters(fac_fwd, fs, fd, tf=tile, tk=min(2048, n))
    tn = min(2048, bsz * HY_W)
    u, v = _dft_fwd(fac_fwd, z, spec_a, spec_b, tf=tile, tn=tn, tk=tile)
    skip_t = jnp.tile(hy_skip.reshape(1, HY_W), (1, bsz))
    hy = _dft_inv(fac_inv, u, v, z, x0, skip_t, tt=tile, tn=tn, tk=tile)

    rw = jnp.zeros((d, ROUTE_LANES), F32)
    rw = rw.at[:, :N_GROUPS].set(router_group_w).at[:, EXPERT_LANE0:EXPERT_LANE0 + N_EXPERTS].set(router_expert_w)
    rb = jnp.zeros((1, ROUTE_LANES), F32)
    rb = rb.at[0, :N_GROUPS].set(router_group_b).at[0, EXPERT_LANE0:EXPERT_LANE0 + N_EXPERTS].set(router_expert_b)
    rw_hi, rw_lo = _split_bf16(rw)
    sub = min(512, n)
    x1, h2, route, cnt = _merge(ret, hy, proj, x, gt1, sh2, sc2, norm2_g.reshape(1, d),
                                ret_w_o.astype(BF16), hy_w_o.astype(BF16), w_out.astype(BF16),
                                rw_hi, rw_lo, rb, tm=min(2 * sub, n), sub=sub)

    t_all = bsz * n
    counts = cnt[0, EXPERT_LANE0:EXPERT_LANE0 + N_EXPERTS].astype(jnp.int32)
    padded = (counts + MOE_BLK - 1) // MOE_BLK * MOE_BLK
    pad_end = jnp.cumsum(padded)
    pad_start = pad_end - padded
    route2 = route.reshape(t_all, ROUTE_LANES)
    e_ab = route2[:, 0:2].astype(jnp.int32)
    onehot = e_ab[:, :, None] == jnp.arange(N_EXPERTS, dtype=jnp.int32)[None, None, :]
    dest = jnp.sum(jnp.where(onehot, pad_start[None, None, :], 0), axis=-1) + route2[:, 4:6].astype(jnp.int32)
    dest_a, dest_b = dest[:, 0], dest[:, 1]
    n_blocks = -(-(2 * t_all + N_EXPERTS * (MOE_BLK - 1)) // MOE_BLK)
    blk0 = jnp.arange(n_blocks, dtype=jnp.int32) * MOE_BLK
    block_expert = jnp.minimum(jnp.sum(blk0[:, None] >= pad_end[None, :], axis=1), N_EXPERTS - 1).astype(jnp.int32)
    n_used = (pad_end[-1:] // MOE_BLK).astype(jnp.int32)

    xb = _dispatch(dest_a, dest_b, pad_end.astype(jnp.int32), h2.reshape(t_all, h2.shape[-1]),
                   n_blocks * MOE_BLK, tm=min(256, n))
    yb = _experts(block_expert, n_used, xb, expert_w1, expert_w3, expert_w2)
    return _combine(dest_a, dest_b, x1, route, gt2, final_norm_g.reshape(1, d), yb, tm=min(256, n))


def kernel(x, c, ctx, c_ctx, ada_w, ada_b, norm1_g, norm2_g, w_in, b_in, ret_decay_logit, ret_w_o, hy_conv_w, hy_conv_b, hy_ffn_w1, hy_ffn_b1, hy_ffn_freq, hy_ffn_w2, hy_ffn_b2, hy_ffn_w3, hy_skip, hy_w_o, w_out, router_group_w, router_group_b, router_expert_w, router_expert_b, expert_w1, expert_w3, expert_w2, final_norm_g):
    depth = ada_w.shape[0]
    assert depth == 1, "single-layer problem: the context stream is only read by the retention states"
    bsz, d = c.shape
    rows = -(-(bsz + 1) // 8) * 8
    cc = jnp.zeros((rows, d), F32).at[:bsz].set(c).at[bsz].set(c_ctx)
    mods = _adaln(cc, ada_w[0], ada_b[0].reshape(1, -1))
    return _layer(x, ctx, mods, norm1_g[0], norm2_g[0], w_in[0], b_in[0], ret_decay_logit[0],
                  ret_w_o[0], hy_conv_w[0], hy_conv_b[0], hy_ffn_w1[0], hy_ffn_b1[0], hy_ffn_freq[0],
                  hy_ffn_w2[0], hy_ffn_b2[0], hy_ffn_w3[0], hy_skip[0], hy_w_o[0], w_out[0],
                  router_group_w[0], router_group_b[0], router_expert_w[0], router_expert_b[0],
                  expert_w1[0], expert_w3[0], expert_w2[0], final_norm_g)
```

```python
import functools
import math

import numpy as np

import jax
import jax.numpy as jnp
from jax import lax
from jax.experimental import pallas as pl
from jax.experimental.pallas import tpu as pltpu

F32 = jnp.float32
BF16 = jnp.bfloat16

D_MODEL = 1024
EPS = 1e-6
GRID_W = 64
ROPE_BASE = 10000.0

RET_HEADS = 4
RET_QK_DIM = 128
RET_V_DIM = 256
RET_CHUNK = 256
RET_UNROLL = 2
QK_W = RET_HEADS * RET_QK_DIM
V_W = RET_HEADS * RET_V_DIM

HY_W = 512
HY_POS_DIM = 33
HY_BANDS = (HY_POS_DIM - 1) // 2
HY_FFN = 64
FFN_LANES = 128
HY_DECAY_TARGET = 1e-2
HY_FAST_PCT = 0.3
HY_SLOW_PCT = 1.5
DFT_GROUP = 64

N_GROUPS = 4
EXPERTS_PER_GROUP = 8
N_EXPERTS = N_GROUPS * EXPERTS_PER_GROUP
EXPERT_HIDDEN = 512
ROUTE_LANES = 128
EXPERT_LANE0 = N_GROUPS
MOE_BLK = 512
SUBLANES = 8

IN_W = 2 * QK_W + 2 * V_W + 3 * HY_W + 2 * D_MODEL
COL_Q, COL_K, COL_V, COL_G = 0, QK_W, 2 * QK_W, 2 * QK_W + V_W
COL_HY = 2 * QK_W + 2 * V_W
COL_GR = COL_HY + 3 * HY_W
COL_GH = COL_GR + D_MODEL

VMEM_LIMIT = 56 * 1024 * 1024
NEG = -1e30


def _cparams(*sem):
    return pltpu.CompilerParams(dimension_semantics=sem, vmem_limit_bytes=VMEM_LIMIT)


def _sigmoid(x):
    return 1.0 / (1.0 + jnp.exp(-x))


def _dot(a, b):
    return jnp.dot(a, b, preferred_element_type=F32)


def _dot_t0(a, b):
    return lax.dot_general(a, b, (((0,), (0,)), ((), ())), preferred_element_type=F32)


def _dot_nt(a, b):
    return lax.dot_general(a, b, (((1,), (1,)), ((), ())), preferred_element_type=F32)


def _split_bf16(a):
    hi = a.astype(BF16)
    lo = (a - hi.astype(F32)).astype(BF16)
    return hi, lo


def _pack_halves(x):
    w = x.shape[1] // 2
    return pltpu.pack_elementwise([x[:, :w], x[:, w:]], packed_dtype=BF16)


def _unpack_halves(p):
    lo = pltpu.unpack_elementwise(p, index=0, packed_dtype=BF16, unpacked_dtype=F32)
    hi = pltpu.unpack_elementwise(p, index=1, packed_dtype=BF16, unpacked_dtype=F32)
    return jnp.concatenate([lo, hi], axis=1)


def _dot3(a, b):
    ah, al = _split_bf16(a)
    bh, bl = _split_bf16(b)
    return _dot(ah, bh) + _dot(al, bh) + _dot(ah, bl)


def _adaln_kernel(c_ref, w_ref, b_ref, o_ref):
    c = c_ref[...]
    o_ref[...] = _dot3(c * _sigmoid(c), w_ref[...]) + b_ref[...]


def _adaln(cc, w, b):
    rows, d = cc.shape
    n = w.shape[1]
    tn = 1536
    return pl.pallas_call(
        _adaln_kernel,
        grid=(n // tn,),
        in_specs=[pl.BlockSpec((rows, d), lambda j: (0, 0)),
                  pl.BlockSpec((d, tn), lambda j: (0, j)),
                  pl.BlockSpec((1, tn), lambda j: (0, j))],
        out_specs=pl.BlockSpec((rows, tn), lambda j: (0, j)),
        out_shape=jax.ShapeDtypeStruct((rows, n), F32),
        compiler_params=_cparams("arbitrary"),
        name="adaln",
    )(cc, w, b)


def _inproj_kernel(x_ref, g_ref, sh_ref, sc_ref, w_ref, b_ref, o_ref, *, tn):
    x = x_ref[0]
    y = x * lax.rsqrt(jnp.mean(x * x, axis=-1, keepdims=True) + EPS) * g_ref[...]
    h = (y * (1.0 + sc_ref[0]) + sh_ref[0]).astype(BF16)
    for j in range(w_ref.shape[1] // tn):
        cols = slice(j * tn, (j + 1) * tn)
        o_ref[0, :, cols] = (_dot(h, w_ref[:, cols]) + b_ref[:, cols]).astype(o_ref.dtype)


def _inproj(x, gain, shift, scale, w, b, tm, tn):
    bsz, n, d = x.shape
    nw = w.shape[1]
    return pl.pallas_call(
        functools.partial(_inproj_kernel, tn=tn),
        grid=(bsz, n // tm),
        in_specs=[pl.BlockSpec((1, tm, d), lambda bi, i: (bi, i, 0)),
                  pl.BlockSpec((1, d), lambda bi, i: (0, 0)),
                  pl.BlockSpec((1, 1, d), lambda bi, i: (bi, 0, 0)),
                  pl.BlockSpec((1, 1, d), lambda bi, i: (bi, 0, 0)),
                  pl.BlockSpec((d, nw), lambda bi, i: (0, 0)),
                  pl.BlockSpec((1, nw), lambda bi, i: (0, 0))],
        out_specs=pl.BlockSpec((1, tm, nw), lambda bi, i: (bi, i, 0)),
        out_shape=jax.ShapeDtypeStruct((bsz, n, nw), BF16),
        compiler_params=_cparams("arbitrary", "arbitrary"),
        name="inproj",
    )(x, gain, shift, scale, w, b)


def _log_sigmoid(x):
    return jnp.minimum(x, 0.0) - jnp.log(1.0 + jnp.exp(-jnp.abs(x)))


def _rope_partner(x):
    lane = lax.broadcasted_iota(jnp.int32, x.shape, 1)
    return jnp.where((lane % 64) < 32, pltpu.roll(x, 96, axis=1), pltpu.roll(x, 32, axis=1))


def _retention_kernel(q_ref, k_ref, v_ref, g_ref, kc_ref, vc_ref, cos_ref, sin_ref, lgt_ref,
                      o_ref, qr_scr, kr_scr, oacc_scr, sf_scr, sb_scr, *, n_tok, n_ctx):
    c_len = RET_CHUNK
    n_chunks = n_tok // c_len
    head = pl.program_id(1)
    k_scale = RET_QK_DIM ** -0.5

    lg_f = _log_sigmoid(lgt_ref[pl.ds(head, 1), :])
    lg_b = _log_sigmoid(lgt_ref[pl.ds(RET_HEADS + head, 1), :])
    lgf_k, lgb_k = lg_f[:, :RET_QK_DIM], lg_b[:, :RET_QK_DIM]

    ii = lax.broadcasted_iota(jnp.int32, (c_len, c_len), 0)
    jj = lax.broadcasted_iota(jnp.int32, (c_len, c_len), 1)
    dif = (ii - jj).astype(F32)
    decay_in = jnp.where(ii >= jj, jnp.exp(lg_f[:, :c_len] * jnp.maximum(dif, 0.0)), 0.0) + \
        jnp.where(jj > ii, jnp.exp(lg_b[:, :c_len] * jnp.maximum(-dif, 0.0)), 0.0)
    pos_k = lax.broadcasted_iota(jnp.int32, (c_len, RET_QK_DIM), 0).astype(F32)
    pos_v = lax.broadcasted_iota(jnp.int32, (c_len, RET_V_DIM), 0).astype(F32)
    dq_f = jnp.exp(lg_f * (pos_v + 1.0))
    dq_b = jnp.exp(lg_b * (c_len - pos_v))
    dk_f = jnp.exp(lgf_k * (c_len - 1.0 - pos_k))
    dk_b = jnp.exp(lgb_k * pos_k)
    dchunk_f = jnp.exp(lg_f * float(c_len))
    dchunk_b = jnp.exp(lg_b * float(c_len))

    pos_c = lax.broadcasted_iota(jnp.int32, (n_ctx, RET_QK_DIM), 0).astype(F32)
    kc = kc_ref[0].astype(F32) * k_scale
    vc = vc_ref[0]
    sf_scr[...] = _dot_t0((kc * jnp.exp(lgf_k * (n_ctx - 1.0 - pos_c))).astype(BF16), vc)
    sb_scr[...] = _dot_t0((kc * jnp.exp(lgb_k * pos_c)).astype(BF16), vc)

    def fwd(c, carry):
        r0 = pl.multiple_of(c * c_len, c_len)
        rows = pl.ds(r0, c_len)
        cos = cos_ref[rows, :]
        sin = sin_ref[rows, :]
        q = q_ref[0, rows, :].astype(F32)
        k = k_ref[0, rows, :].astype(F32)
        qr = q * cos + _rope_partner(q) * sin
        kr = (k * cos + _rope_partner(k) * sin) * k_scale
        qb = qr.astype(BF16)
        kb = kr.astype(BF16)
        qr_scr[rows, :] = qb
        kr_scr[rows, :] = kr
        v = v_ref[0, rows, :]
        scores = _dot_nt(qb, kb) * decay_in
        o = _dot(scores.astype(BF16), v) + _dot(qb, sf_scr[...].astype(BF16)) * dq_f
        oacc_scr[rows, :] = o
        sf_scr[...] = sf_scr[...] * dchunk_f + _dot_t0((kr * dk_f).astype(BF16), v)
        return carry

    lax.fori_loop(0, n_chunks, fwd, 0, unroll=RET_UNROLL)

    def bwd(t, carry):
        c = n_chunks - 1 - t
        r0 = pl.multiple_of(c * c_len, c_len)
        rows = pl.ds(r0, c_len)
        v = v_ref[0, rows, :]
        o = oacc_scr[rows, :] + _dot(qr_scr[rows, :], sb_scr[...].astype(BF16)) * dq_b
        o = o * lax.rsqrt(jnp.mean(o * o, axis=-1, keepdims=True) + EPS)
        g = g_ref[0, rows, :].astype(F32)
        o_ref[0, rows, :] = (g * _sigmoid(g) * o).astype(o_ref.dtype)
        sb_scr[...] = sb_scr[...] * dchunk_b + _dot_t0((kr_scr[rows, :] * dk_b).astype(BF16), v)
        return carry

    lax.fori_loop(0, n_chunks, bwd, 0, unroll=RET_UNROLL)


def _retention(proj, proj_c, cos_t, sin_t, lgt):
    bsz, n, _ = proj.shape
    n_ctx = proj_c.shape[1]
    kq, kv = RET_QK_DIM, RET_V_DIM
    assert n % RET_CHUNK == 0 and RET_CHUNK <= kv
    kern = functools.partial(_retention_kernel, n_tok=n, n_ctx=n_ctx)
    return pl.pallas_call(
        kern,
        grid=(bsz, RET_HEADS),
        in_specs=[pl.BlockSpec((1, n, kq), lambda b, h: (b, 0, COL_Q // kq + h)),
                  pl.BlockSpec((1, n, kq), lambda b, h: (b, 0, COL_K // kq + h)),
                  pl.BlockSpec((1, n, kv), lambda b, h: (b, 0, COL_V // kv + h)),
                  pl.BlockSpec((1, n, kv), lambda b, h: (b, 0, COL_G // kv + h)),
                  pl.BlockSpec((1, n_ctx, kq), lambda b, h: (b, 0, h)),
                  pl.BlockSpec((1, n_ctx, kv), lambda b, h: (b, 0, QK_W // kv + h)),
                  pl.BlockSpec((n, kq), lambda b, h: (0, 0)),
                  pl.BlockSpec((n, kq), lambda b, h: (0, 0)),
                  pl.BlockSpec((2 * RET_HEADS, kv), lambda b, h: (0, 0))],
        out_specs=pl.BlockSpec((1, n, kv), lambda b, h: (b, 0, h)),
        out_shape=jax.ShapeDtypeStruct((bsz, n, V_W), BF16),
        scratch_shapes=[pltpu.VMEM((n, kq), BF16), pltpu.VMEM((n, kq), F32),
                        pltpu.VMEM((n, kv), F32), pltpu.VMEM((kq, kv), F32),
                        pltpu.VMEM((kq, kv), F32)],
        compiler_params=_cparams("arbitrary", "arbitrary"),
        name="retention",
    )(proj, proj, proj, proj, proj_c, proj_c, cos_t, sin_t, lgt)


def _filter_kernel(feats_ref, w1_ref, b1_ref, fr_ref, w2_ref, b2_ref, w3f_ref, w3b_ref,
                   df_ref, db_ref, fs_ref, fd_ref, *, n_tok):
    fr = fr_ref[...]
    hid = jnp.sin(fr * (_dot3(feats_ref[...], w1_ref[...]) + b1_ref[...]))
    hid = jnp.sin(fr * (_dot3(hid, w2_ref[...]) + b2_ref[...]))
    t = lax.broadcasted_iota(jnp.int32, (n_tok, df_ref.shape[1]), 0).astype(F32) / n_tok

    def one(w3_ref, d_ref):
        f = _dot3(hid, w3_ref[...]) * jnp.exp(-t * d_ref[...])
        return f / jnp.sum(jnp.abs(f), axis=0, keepdims=True)

    hf = one(w3f_ref, df_ref)
    hb = one(w3b_ref, db_ref)
    fs_ref[...] = (hf + hb).astype(fs_ref.dtype)
    fd_ref[...] = (hf - hb).astype(fd_ref.dtype)


def _hyena_filters(n, feats, w1, b1, freq, w2, b2, w3, deltas):
    tc = HY_W
    nf = feats.shape[1]
    kern = functools.partial(_filter_kernel, n_tok=n)
    full = lambda shape: pl.BlockSpec(shape, lambda j: (0, 0))
    return pl.pallas_call(
        kern,
        grid=(HY_W // tc,),
        in_specs=[full((n, nf)), full((nf, FFN_LANES)), full((1, FFN_LANES)), full((1, FFN_LANES)),
                  full((FFN_LANES, FFN_LANES)), full((1, FFN_LANES)),
                  pl.BlockSpec((FFN_LANES, tc), lambda j: (0, j)),
                  pl.BlockSpec((FFN_LANES, tc), lambda j: (0, HY_W // tc + j)),
                  pl.BlockSpec((1, tc), lambda j: (0, j)),
                  pl.BlockSpec((1, tc), lambda j: (0, HY_W // tc + j))],
        out_specs=[pl.BlockSpec((n, tc), lambda j: (0, j)),
                   pl.BlockSpec((n, tc), lambda j: (0, j))],
        out_shape=[jax.ShapeDtypeStruct((n, HY_W), BF16)] * 2,
        compiler_params=_cparams("arbitrary"),
        name="hyena_filters",
    )(feats, w1, b1, freq, w2, b2, w3, w3, deltas, deltas)


def _hyena_pre_kernel(u0_ref, u1_ref, u2_ref, w_ref, b_ref, z_ref, x0_ref, *, n_tok, rows):
    n_steps = n_tok // rows
    tc = z_ref.shape[1]
    halo = 16
    rid = lax.broadcasted_iota(jnp.int32, (rows, tc), 0)

    def conv(u_ref, part, r0, has_prev, has_next):
        x = u_ref[0, pl.ds(r0, rows), :].astype(F32)
        prev_g = u_ref[0, pl.ds(pl.multiple_of(jnp.maximum(r0 - halo, 0), halo), halo), :].astype(F32)
        next_g = u_ref[0, pl.ds(pl.multiple_of(jnp.minimum(r0 + rows, n_tok - halo), halo), halo), :].astype(F32)
        prev_row = jnp.where(has_prev, prev_g[halo - 1:halo, :], 0.0)
        next_row = jnp.where(has_next, next_g[0:1, :], 0.0)
        up = jnp.where(rid == 0, prev_row, pltpu.roll(x, 1, axis=0))
        dn = jnp.where(rid == rows - 1, next_row, pltpu.roll(x, rows - 1, axis=0))
        w = w_ref[part]
        return up * w[0:1, :] + x * w[1:2, :] + dn * w[2:3, :] + b_ref[part]

    def body(s, carry):
        r0 = pl.multiple_of(s * rows, rows)
        has_prev = s > 0
        has_next = s < n_steps - 1
        x0 = conv(u0_ref, 0, r0, has_prev, has_next)
        x1 = conv(u1_ref, 1, r0, has_prev, has_next)
        vv = conv(u2_ref, 2, r0, has_prev, has_next)
        z_ref[pl.ds(r0, rows), :] = (x1 * vv).astype(z_ref.dtype)
        x0_ref[pl.ds(r0, rows), :] = x0.astype(x0_ref.dtype)
        return carry

    lax.fori_loop(0, n_steps, body, 0)


def _hyena_pre(proj, conv_w, conv_b):
    bsz, n, _ = proj.shape
    tc = 256
    nj = HY_W // tc
    rows = min(512, n)
    base = COL_HY // tc
    kern = functools.partial(_hyena_pre_kernel, n_tok=n, rows=rows)
    u_spec = lambda part: pl.BlockSpec((1, n, tc), lambda b, j: (b, 0, base + part * nj + j))
    wp = jnp.zeros((3, 8, HY_W), F32).at[:, :3, :].set(conv_w.reshape(3, 3, HY_W).transpose(1, 0, 2))
    bp = conv_b.reshape(3, 1, HY_W)
    return pl.pallas_call(
        kern,
        grid=(bsz, nj),
        in_specs=[u_spec(0), u_spec(1), u_spec(2),
                  pl.BlockSpec((3, 8, tc), lambda b, j: (0, 0, j)),
                  pl.BlockSpec((3, 1, tc), lambda b, j: (0, 0, j))],
        out_specs=[pl.BlockSpec((n, tc), lambda b, j: (0, b * nj + j)),
                   pl.BlockSpec((n, tc), lambda b, j: (0, b * nj + j))],
        out_shape=[jax.ShapeDtypeStruct((n, bsz * HY_W), BF16)] * 2,
        compiler_params=_cparams("arbitrary", "arbitrary"),
        name="hyena_pre",
    )(proj, proj, proj, wp, bp)


def _dft_factors(n, inverse):
    period = 4 * n
    col = np.arange(n, dtype=np.int64)[None, :]
    hi = np.arange(n // DFT_GROUP, dtype=np.int64)[:, None]
    lo = np.arange(DFT_GROUP, dtype=np.int64)[:, None]
    if inverse:
        m_a, m_b = (2 * col + 1) * (DFT_GROUP * hi), (2 * col + 1) * lo
    else:
        m_a, m_b = (2 * DFT_GROUP * hi) * col, (2 * lo + 1) * col
    ang_a = (m_a % period) * (2.0 * math.pi / period)
    ang_b = (m_b % period) * (2.0 * math.pi / period)
    return tuple(jnp.asarray(t, F32) for t in (np.cos(ang_a), np.sin(ang_a), np.cos(ang_b), np.sin(ang_b)))


def _dft_tile(ca_ref, sa_ref, cb_ref, sb_ref):
    cb = cb_ref[...]
    sb = sb_ref[...]
    cos_rows, sin_rows = [], []
    for g in range(ca_ref.shape[0]):
        ca = ca_ref[g:g + 1, :]
        sa = sa_ref[g:g + 1, :]
        cos_rows.append((ca * cb - sa * sb).astype(BF16))
        sin_rows.append((sa * cb + ca * sb).astype(BF16))
    return jnp.concatenate(cos_rows, axis=0), jnp.concatenate(sin_rows, axis=0)


def _factor_specs(tr, tc, row_axis, col_axis):
    a_spec = pl.BlockSpec((tr // DFT_GROUP, tc), lambda *ids: (ids[row_axis], ids[col_axis]))
    b_spec = pl.BlockSpec((DFT_GROUP, tc), lambda *ids: (0, ids[col_axis]))
    return [a_spec, a_spec, b_spec, b_spec]


def _dft_filter_kernel(ca_ref, sa_ref, cb_ref, sb_ref, fs_ref, fd_ref, a_ref, b_ref, acc_a, acc_b):
    kk = pl.program_id(1)

    @pl.when(kk == 0)
    def _():
        acc_a[...] = jnp.zeros_like(acc_a)
        acc_b[...] = jnp.zeros_like(acc_b)

    cos_t, sin_t = _dft_tile(ca_ref, sa_ref, cb_ref, sb_ref)
    acc_a[...] += _dot(cos_t, fs_ref[...])
    acc_b[...] += _dot(sin_t, fd_ref[...])

    @pl.when(kk == pl.num_programs(1) - 1)
    def _():
        a_ref[...] = acc_a[...]
        b_ref[...] = acc_b[...]


def _dft_filters(factors, fs, fd, tf, tk):
    n, w = fs.shape
    return pl.pallas_call(
        _dft_filter_kernel,
        grid=(n // tf, n // tk),
        in_specs=_factor_specs(tf, tk, 0, 1) + [
            pl.BlockSpec((tk, w), lambda i, kk: (kk, 0)),
            pl.BlockSpec((tk, w), lambda i, kk: (kk, 0))],
        out_specs=[pl.BlockSpec((tf, w), lambda i, kk: (i, 0)),
                   pl.BlockSpec((tf, w), lambda i, kk: (i, 0))],
        out_shape=[jax.ShapeDtypeStruct((n, w), F32)] * 2,
        scratch_shapes=[pltpu.VMEM((tf, w), F32), pltpu.VMEM((tf, w), F32)],
        compiler_params=_cparams("arbitrary", "arbitrary"),
        name="dft_filters",
    )(*factors, fs, fd)


def _dft_fwd_kernel(ca_ref, sa_ref, cb_ref, sb_ref, z_ref, a_ref, b_ref, u_ref, v_ref, acc_p, acc_q):
    kk = pl.program_id(2)

    @pl.when(kk == 0)
    def _():
        acc_p[...] = jnp.zeros_like(acc_p)
        acc_q[...] = jnp.zeros_like(acc_q)

    cos_t, sin_t = _dft_tile(ca_ref, sa_ref, cb_ref, sb_ref)
    z = z_ref[...]
    acc_p[...] += _dot(cos_t, z)
    acc_q[...] += _dot(sin_t, z)

    @pl.when(kk == pl.num_programs(2) - 1)
    def _():
        a = a_ref[...]
        b = b_ref[...]
        w = a.shape[1]
        for s in range(u_ref.shape[1] // w):
            cols = slice(s * w, (s + 1) * w)
            p = acc_p[:, cols]
            q = acc_q[:, cols]
            u_ref[:, cols] = (p * a - q * b).astype(u_ref.dtype)
            v_ref[:, cols] = (p * b + q * a).astype(v_ref.dtype)


def _dft_fwd(factors, z, spec_a, spec_b, tf, tn, tk):
    n, ncol = z.shape
    w = spec_a.shape[1]
    return pl.pallas_call(
        _dft_fwd_kernel,
        grid=(n // tf, ncol // tn, n // tk),
        in_specs=_factor_specs(tf, tk, 0, 2) + [
            pl.BlockSpec((tk, tn), lambda i, j, kk: (kk, j)),
            pl.BlockSpec((tf, w), lambda i, j, kk: (i, 0)),
            pl.BlockSpec((tf, w), lambda i, j, kk: (i, 0))],
        out_specs=[pl.BlockSpec((tf, tn), lambda i, j, kk: (i, j)),
                   pl.BlockSpec((tf, tn), lambda i, j, kk: (i, j))],
        out_shape=[jax.ShapeDtypeStruct((n, ncol), BF16)] * 2,
        scratch_shapes=[pltpu.VMEM((tf, tn), F32), pltpu.VMEM((tf, tn), F32)],
        compiler_params=_cparams("arbitrary", "arbitrary", "arbitrary"),
        name="dft_fwd",
    )(*factors, z, spec_a, spec_b)


def _dft_inv_kernel(ca_ref, sa_ref, cb_ref, sb_ref, u_ref, v_ref, z_ref, x0_ref, skip_ref, o_ref, acc,
                    *, inv_n):
    kk = pl.program_id(2)

    @pl.when(kk == 0)
    def _():
        acc[...] = jnp.zeros_like(acc)

    cos_t, sin_t = _dft_tile(ca_ref, sa_ref, cb_ref, sb_ref)
    acc[...] += _dot(cos_t, u_ref[...]) + _dot(sin_t, v_ref[...])

    @pl.when(kk == pl.num_programs(2) - 1)
    def _():
        y = acc[...] * inv_n + z_ref[...].astype(F32) * skip_ref[...]
        o_ref[...] = (x0_ref[...].astype(F32) * y).astype(o_ref.dtype)


def _dft_inv(factors, u, v, z, x0, skip_t, tt, tn, tk):
    n, ncol = u.shape
    kern = functools.partial(_dft_inv_kernel, inv_n=1.0 / n)
    return pl.pallas_call(
        kern,
        grid=(n // tt, ncol // tn, n // tk),
        in_specs=_factor_specs(tt, tk, 0, 2) + [
                  pl.BlockSpec((tk, tn), lambda i, j, kk: (kk, j)),
                  pl.BlockSpec((tk, tn), lambda i, j, kk: (kk, j)),
                  pl.BlockSpec((tt, tn), lambda i, j, kk: (i, j)),
                  pl.BlockSpec((tt, tn), lambda i, j, kk: (i, j)),
                  pl.BlockSpec((1, tn), lambda i, j, kk: (0, j))],
        out_specs=pl.BlockSpec((tt, tn), lambda i, j, kk: (i, j)),
        out_shape=jax.ShapeDtypeStruct((n, ncol), BF16),
        scratch_shapes=[pltpu.VMEM((tt, tn), F32)],
        compiler_params=_cparams("arbitrary", "arbitrary", "arbitrary"),
        name="dft_inv",
    )(*factors, u, v, z, x0, skip_t)


def _merge_kernel(ret_ref, hy_ref, gr0_ref, gr1_ref, gh0_ref, gh1_ref, x_ref, gt1_ref, sh2_ref,
                  sc2_ref, g2_ref, wro_ref, who_ref, wout_ref, rwh_ref, rwl_ref, rb_ref,
                  x1_ref, h2_ref, route_ref, routet_ref, cnt_ref, base_scr, *, sub):
    first = jnp.logical_and(pl.program_id(0) == 0, pl.program_id(1) == 0)

    @pl.when(first)
    def _():
        base_scr[...] = jnp.zeros_like(base_scr)

    for s0 in range(0, x_ref.shape[1], sub):
        _merge_rows(slice(s0, s0 + sub), ret_ref, hy_ref, gr0_ref, gr1_ref, gh0_ref, gh1_ref, x_ref,
                    gt1_ref, sh2_ref, sc2_ref, g2_ref, wro_ref, who_ref, wout_ref, rwh_ref, rwl_ref,
                    rb_ref, x1_ref, h2_ref, route_ref, routet_ref, base_scr)
    cnt_ref[...] = base_scr[...]


def _merge_rows(rows, ret_ref, hy_ref, gr0_ref, gr1_ref, gh0_ref, gh1_ref, x_ref, gt1_ref, sh2_ref,
                sc2_ref, g2_ref, wro_ref, who_ref, wout_ref, rwh_ref, rwl_ref, rb_ref,
                x1_ref, h2_ref, route_ref, routet_ref, base_scr):
    tm = rows.stop - rows.start
    gate_r = jnp.concatenate([gr0_ref[0, rows, :], gr1_ref[0, rows, :]], axis=1).astype(F32)
    gate_h = jnp.concatenate([gh0_ref[0, rows, :], gh1_ref[0, rows, :]], axis=1).astype(F32)
    mixed = _sigmoid(gate_r) * _dot(ret_ref[0, rows, :], wro_ref[...]) + \
        _sigmoid(gate_h) * _dot(hy_ref[rows, :], who_ref[...])
    x1 = x_ref[0, rows, :] + gt1_ref[0] * _dot(mixed.astype(BF16), wout_ref[...])
    x1_ref[0, rows, :] = x1
    h2 = x1 * lax.rsqrt(jnp.mean(x1 * x1, axis=-1, keepdims=True) + EPS) * g2_ref[...]
    h2 = h2 * (1.0 + sc2_ref[0]) + sh2_ref[0]
    h2_ref[0, rows, :] = _pack_halves(h2)

    h2_hi, h2_lo = _split_bf16(h2)
    rw_hi = rwh_ref[...]
    logits = _dot(h2_hi, rw_hi) + _dot(h2_lo, rw_hi) + _dot(h2_hi, rwl_ref[...]) + rb_ref[...]
    lane = lax.broadcasted_iota(jnp.int32, logits.shape, 1)
    lane_f = lane.astype(F32)
    big = float(ROUTE_LANES)

    def first_lane(mask):
        return jnp.min(jnp.where(mask, lane_f, big), axis=1, keepdims=True)

    is_group = lane < N_GROUPS
    gl = jnp.where(is_group, logits, NEG)
    ge = jnp.where(is_group, jnp.exp(gl - jnp.max(gl, axis=1, keepdims=True)), 0.0)
    group_p = ge / jnp.sum(ge, axis=1, keepdims=True)
    p_star = jnp.max(group_p, axis=1, keepdims=True)
    g_star = first_lane(jnp.logical_and(is_group, group_p == p_star))
    lo = EXPERT_LANE0 + g_star * EXPERTS_PER_GROUP
    in_group = jnp.logical_and(lane_f >= lo, lane_f < lo + EXPERTS_PER_GROUP)
    el = jnp.where(in_group, logits, NEG)
    ee = jnp.where(in_group, jnp.exp(el - jnp.max(el, axis=1, keepdims=True)), 0.0)
    sp = jnp.where(in_group, ee / jnp.sum(ee, axis=1, keepdims=True), -1.0)
    w_a = jnp.max(sp, axis=1, keepdims=True)
    l_a = first_lane(sp == w_a)
    sp2 = jnp.where(lane_f == l_a, -1.0, sp)
    w_b = jnp.max(sp2, axis=1, keepdims=True)
    l_b = first_lane(sp2 == w_b)
    wsum = w_a + w_b
    wt_a = p_star * w_a / wsum
    wt_b = p_star * w_b / wsum

    hit_a = lane_f == l_a
    hit_b = lane_f == l_b
    onehot = jnp.where(jnp.logical_or(hit_a, hit_b), 1.0, 0.0)
    ri = lax.broadcasted_iota(jnp.int32, (tm, tm), 0)
    ci = lax.broadcasted_iota(jnp.int32, (tm, tm), 1)
    tri = jnp.where(ri > ci, 1.0, 0.0).astype(BF16)
    before = _dot(tri, onehot.astype(BF16)) + base_scr[0:1, :]
    rank_a = jnp.sum(jnp.where(hit_a, before, 0.0), axis=1, keepdims=True)
    rank_b = jnp.sum(jnp.where(hit_b, before, 0.0), axis=1, keepdims=True)
    base_scr[0:1, :] = base_scr[0:1, :] + jnp.sum(onehot, axis=0, keepdims=True)

    vals = (l_a - EXPERT_LANE0, l_b - EXPERT_LANE0, wt_a, wt_b, rank_a, rank_b)
    route = jnp.zeros(logits.shape, F32)
    for idx, val in enumerate(vals):
        route = jnp.where(lane == idx, val, route)
    route_ref[0, rows, :] = route
    routet_ref[:, rows] = route.T[:SUBLANES, :]


def _merge(ret, hy, proj, x, gt1, sh2, sc2, g2, w_ro, w_ho, w_out, rw_hi, rw_lo, rb, tm, sub):
    bsz, n, d = x.shape
    nt = n // tm
    hw = HY_W
    gspec = lambda col: pl.BlockSpec((1, tm, hw), lambda b, i: (b, i, col // hw))
    vec = pl.BlockSpec((1, 1, d), lambda b, i: (b, 0, 0))
    full = lambda shape: pl.BlockSpec(shape, lambda b, i: (0, 0))
    tok = pl.BlockSpec((1, tm, d), lambda b, i: (b, i, 0))
    return pl.pallas_call(
        functools.partial(_merge_kernel, sub=sub),
        grid=(bsz, nt),
        in_specs=[tok,
                  pl.BlockSpec((tm, hw), lambda b, i: (i, b)),
                  gspec(COL_GR), gspec(COL_GR + hw), gspec(COL_GH), gspec(COL_GH + hw),
                  tok, vec, vec, vec, full((1, d)),
                  full((V_W, d)), full((hw, d)), full((d, d)),
                  full((d, ROUTE_LANES)), full((d, ROUTE_LANES)), full((1, ROUTE_LANES))],
        out_specs=[tok, pl.BlockSpec((1, tm, d // 2), lambda b, i: (b, i, 0)),
                   pl.BlockSpec((1, tm, ROUTE_LANES), lambda b, i: (b, i, 0)),
                   pl.BlockSpec((SUBLANES, tm), lambda b, i: (0, b * nt + i)),
                   pl.BlockSpec((8, ROUTE_LANES), lambda b, i: (0, 0))],
        out_shape=[jax.ShapeDtypeStruct((bsz, n, d), F32),
                   jax.ShapeDtypeStruct((bsz, n, d // 2), jnp.int32),
                   jax.ShapeDtypeStruct((bsz, n, ROUTE_LANES), F32),
                   jax.ShapeDtypeStruct((SUBLANES, bsz * n), F32),
                   jax.ShapeDtypeStruct((8, ROUTE_LANES), F32)],
        scratch_shapes=[pltpu.VMEM((8, ROUTE_LANES), F32)],
        compiler_params=_cparams("arbitrary", "arbitrary"),
        name="merge_route",
    )(ret, hy, proj, proj, proj, proj, x, gt1, sh2, sc2, g2, w_ro, w_ho, w_out, rw_hi, rw_lo, rb)


def _dispatch_kernel(da_ref, db_ref, pend_ref, h_ref, xb_ref, zero_scr, sem, zsem, *, tm):
    i = pl.program_id(0)

    def row_copy(g, u, dst_row):
        return pltpu.make_async_copy(h_ref.at[g, pl.ds(u, 1)], xb_ref.at[pl.ds(dst_row, 1)], sem)

    @pl.when(i == 0)
    def _():
        zero_scr[...] = jnp.zeros_like(zero_scr)

        def zcopy(e):
            return pltpu.make_async_copy(
                zero_scr, xb_ref.at[pl.ds(pl.multiple_of(pend_ref[e] - MOE_BLK, MOE_BLK), MOE_BLK)], zsem)

        def nonempty(e):
            prev = jnp.where(e > 0, pend_ref[jnp.maximum(e - 1, 0)], 0)
            return pend_ref[e] > prev

        for e in range(N_EXPERTS):
            @pl.when(nonempty(e))
            def _():
                zcopy(e).start()
        for e in range(N_EXPERTS):
            @pl.when(nonempty(e))
            def _():
                zcopy(e).wait()

        def tail_copy(blk):
            return pltpu.make_async_copy(
                zero_scr, xb_ref.at[pl.ds(pl.multiple_of(blk * MOE_BLK, MOE_BLK), MOE_BLK)], zsem)

        first_unused = pend_ref[N_EXPERTS - 1] // MOE_BLK
        n_blocks = xb_ref.shape[0] // MOE_BLK
        lax.fori_loop(first_unused, n_blocks, lambda blk, c: (tail_copy(blk).start(), c)[1], 0)
        lax.fori_loop(first_unused, n_blocks, lambda blk, c: (tail_copy(blk).wait(), c)[1], 0)

    def issue(g, carry):
        t0 = i * tm + g * SUBLANES
        for u in range(SUBLANES):
            row_copy(g, u, da_ref[t0 + u]).start(priority=0)
            row_copy(g, u, db_ref[t0 + u]).start(priority=1)
        return carry

    lax.fori_loop(0, tm // SUBLANES, issue, 0)

    def drain(g, carry):
        for _ in range(2 * SUBLANES):
            row_copy(0, 0, 0).wait()
        return carry

    lax.fori_loop(0, tm // SUBLANES, drain, 0)


def _dispatch(dest_a, dest_b, pad_end, h2, n_rows, tm):
    t_all, d = h2.shape
    kern = functools.partial(_dispatch_kernel, tm=tm)
    return pl.pallas_call(
        kern,
        grid_spec=pltpu.PrefetchScalarGridSpec(
            num_scalar_prefetch=3,
            grid=(t_all // tm,),
            in_specs=[pl.BlockSpec((tm // SUBLANES, SUBLANES, d), lambda i, *_: (i, 0, 0))],
            out_specs=pl.BlockSpec(memory_space=pl.ANY),
            scratch_shapes=[pltpu.VMEM((MOE_BLK, d), h2.dtype),
                            pltpu.SemaphoreType.DMA(()), pltpu.SemaphoreType.DMA(())]),
        out_shape=jax.ShapeDtypeStruct((n_rows, d), h2.dtype),
        compiler_params=_cparams("arbitrary"),
        name="moe_dispatch",
    )(dest_a, dest_b, pad_end, h2.reshape(t_all // SUBLANES, SUBLANES, d))


def _expert_kernel(be_ref, nu_ref, x_ref, w1_ref, w3_ref, w2_ref, o_ref, w1_scr, w3_scr, w2_scr):
    i = pl.program_id(0)

    @pl.when(jnp.logical_or(i == 0, be_ref[i] != be_ref[jnp.maximum(i - 1, 0)]))
    def _():
        w1_scr[...] = w1_ref[0].astype(BF16)
        w3_scr[...] = w3_ref[0].astype(BF16)
        w2_scr[...] = w2_ref[0].astype(BF16)

    @pl.when(i < nu_ref[0])
    def _():
        x = _unpack_halves(x_ref[...]).astype(BF16)
        a = _dot(x, w1_scr[...])
        b = _dot(x, w3_scr[...])
        o_ref[...] = _pack_halves(_dot((a * _sigmoid(a) * b).astype(BF16), w2_scr[...]))

    @pl.when(i >= nu_ref[0])
    def _():
        o_ref[...] = jnp.zeros_like(o_ref)


def _experts(block_expert, n_used, xb, w1, w3, w2):
    n_rows, dp = xb.shape
    _, d, hid = w1.shape
    row_blk = lambda i, be, nu: (jnp.minimum(i, nu[0] - 1), 0)
    return pl.pallas_call(
        _expert_kernel,
        grid_spec=pltpu.PrefetchScalarGridSpec(
            num_scalar_prefetch=2,
            grid=(n_rows // MOE_BLK,),
            in_specs=[pl.BlockSpec((MOE_BLK, dp), row_blk),
                      pl.BlockSpec((1, d, hid), lambda i, be, nu: (be[i], 0, 0)),
                      pl.BlockSpec((1, d, hid), lambda i, be, nu: (be[i], 0, 0)),
                      pl.BlockSpec((1, hid, d), lambda i, be, nu: (be[i], 0, 0))],
            out_specs=pl.BlockSpec((MOE_BLK, dp), lambda i, be, nu: (i, 0)),
            scratch_shapes=[pltpu.VMEM((d, hid), BF16), pltpu.VMEM((d, hid), BF16),
                            pltpu.VMEM((hid, d), BF16)]),
        out_shape=jax.ShapeDtypeStruct((n_rows, dp), xb.dtype),
        compiler_params=_cparams("arbitrary"),
        name="moe_experts",
    )(block_expert, n_used, xb, w1, w3, w2)


def _combine_kernel(da_ref, db_ref, x1_ref, route_ref, gt2_ref, gf_ref, yb_ref, o_ref,
                    buf, sems, *, tm, tiles_per_batch):
    b = pl.program_id(0)
    i = pl.program_id(1)
    step = b * tiles_per_batch + i
    n_steps = pl.num_programs(0) * tiles_per_batch
    slot = step % 2

    def row_copy(src_row, s, which, g, u):
        return pltpu.make_async_copy(yb_ref.at[pl.ds(src_row, 1)], buf.at[s, which, g, pl.ds(u, 1)],
                                     sems.at[s])

    def issue_tile(tile, s):
        def body(g, carry):
            t0 = tile * tm + g * SUBLANES
            for u in range(SUBLANES):
                row_copy(da_ref[t0 + u], s, 0, g, u).start(priority=0)
                row_copy(db_ref[t0 + u], s, 1, g, u).start(priority=1)
            return carry
        lax.fori_loop(0, tm // SUBLANES, body, 0)

    @pl.when(step == 0)
    def _():
        issue_tile(0, 0)

    @pl.when(step + 1 < n_steps)
    def _():
        issue_tile(step + 1, 1 - slot)

    def drain(g, carry):
        for _ in range(SUBLANES):
            row_copy(0, slot, 0, 0, 0).wait()
            row_copy(0, slot, 1, 0, 0).wait()
        return carry

    lax.fori_loop(0, tm // SUBLANES, drain, 0)

    route = route_ref[0]
    dp = buf.shape[-1]
    y = route[:, 2:3] * _unpack_halves(buf[slot, 0].reshape(tm, dp)) + \
        route[:, 3:4] * _unpack_halves(buf[slot, 1].reshape(tm, dp))
    xo = x1_ref[0] + gt2_ref[0] * y
    o_ref[0] = xo * lax.rsqrt(jnp.mean(xo * xo, axis=-1, keepdims=True) + EPS) * gf_ref[...]


def _combine(dest_a, dest_b, x1, route, gt2, gf, yb, tm):
    bsz, n, d = x1.shape
    nt = n // tm
    kern = functools.partial(_combine_kernel, tm=tm, tiles_per_batch=nt)
    return pl.pallas_call(
        kern,
        grid_spec=pltpu.PrefetchScalarGridSpec(
            num_scalar_prefetch=2,
            grid=(bsz, nt),
            in_specs=[pl.BlockSpec((1, tm, d), lambda b, i, *_: (b, i, 0)),
                      pl.BlockSpec((1, tm, ROUTE_LANES), lambda b, i, *_: (b, i, 0)),
                      pl.BlockSpec((1, 1, d), lambda b, i, *_: (b, 0, 0)),
                      pl.BlockSpec((1, d), lambda b, i, *_: (0, 0)),
                      pl.BlockSpec(memory_space=pl.ANY)],
            out_specs=pl.BlockSpec((1, tm, d), lambda b, i, *_: (b, i, 0)),
            scratch_shapes=[pltpu.VMEM((2, 2, tm // SUBLANES, SUBLANES, yb.shape[1]), yb.dtype),
                            pltpu.SemaphoreType.DMA((2,))]),
        out_shape=jax.ShapeDtypeStruct((bsz, n, d), F32),
        compiler_params=_cparams("arbitrary", "arbitrary"),
        name="moe_combine",
    )(dest_a, dest_b, x1, route, gt2, gf, yb)


def _rope_tables(n):
    rows = n // GRID_W
    r, col = np.meshgrid(np.arange(rows, dtype=np.float64), np.arange(GRID_W, dtype=np.float64), indexing='ij')
    n_freq = RET_QK_DIM // 4
    inv_freq = ROPE_BASE ** (-np.arange(n_freq, dtype=np.float64) / n_freq)
    ang_r = r.reshape(-1)[:, None] * inv_freq
    ang_c = col.reshape(-1)[:, None] * inv_freq
    cos_t = np.concatenate([np.cos(ang_r), np.cos(ang_r), np.cos(ang_c), np.cos(ang_c)], axis=-1)
    sin_t = np.concatenate([-np.sin(ang_r), np.sin(ang_r), -np.sin(ang_c), np.sin(ang_c)], axis=-1)
    return jnp.asarray(cos_t, F32), jnp.asarray(sin_t, F32)


def _hyena_feats(n):
    t = np.arange(n, dtype=np.float64) / n
    bands = np.linspace(1e-4, HY_BANDS - 1, HY_BANDS)
    phase = 2.0 * math.pi * t[:, None] * bands[None, :]
    feats = np.concatenate([t[:, None], np.cos(phase), -np.sin(phase)], axis=-1)
    return jnp.asarray(np.pad(feats, ((0, 0), (0, FFN_LANES - HY_POS_DIM))), F32)


def _layer(x, ctx, mods, norm1_g, norm2_g, w_in, b_in, ret_decay_logit, ret_w_o, hy_conv_w,
           hy_conv_b, hy_ffn_w1, hy_ffn_b1, hy_ffn_freq, hy_ffn_w2, hy_ffn_b2, hy_ffn_w3, hy_skip,
           hy_w_o, w_out, router_group_w, router_group_b, router_expert_w, router_expert_b,
           expert_w1, expert_w3, expert_w2, final_norm_g):
    bsz, n, d = x.shape
    n_ctx = ctx.shape[1]
    mod_lat = mods[:bsz].reshape(bsz, 6, 1, d)
    sh1, sc1, gt1, sh2, sc2, gt2 = (mod_lat[:, s] for s in range(6))
    mod_ctx = mods[bsz].reshape(6, 1, 1, d)
    csh1 = jnp.broadcast_to(mod_ctx[0], (bsz, 1, d))
    csc1 = jnp.broadcast_to(mod_ctx[1], (bsz, 1, d))

    g1 = norm1_g.reshape(1, d)
    b_in2 = b_in.reshape(1, IN_W)
    w_in_b = w_in.astype(BF16)
    proj = _inproj(x, g1, sh1, sc1, w_in_b, b_in2, tm=min(512, n), tn=512)
    proj_c = _inproj(ctx, g1, csh1, csc1, w_in_b[:, COL_K:COL_G], b_in2[:, COL_K:COL_G],
                     tm=n_ctx, tn=512)

    cos_r, sin_r = _rope_tables(n)
    lgt = jnp.broadcast_to(ret_decay_logit.astype(F32).reshape(2 * RET_HEADS, 1),
                           (2 * RET_HEADS, RET_V_DIM))
    ret = _retention(proj, proj_c, cos_r, sin_r, lgt)

    slow = abs(math.log(HY_DECAY_TARGET)) / HY_SLOW_PCT
    fast = abs(math.log(HY_DECAY_TARGET)) / HY_FAST_PCT
    deltas = jnp.tile(jnp.linspace(slow, fast, HY_W, dtype=F32), 2).reshape(1, 2 * HY_W)
    fpad = FFN_LANES - HY_FFN
    row = lambda a: jnp.pad(a.reshape(1, HY_FFN), ((0, 0), (0, fpad)))
    fs, fd = _hyena_filters(n, _hyena_feats(n),
                            jnp.pad(hy_ffn_w1, ((0, FFN_LANES - HY_POS_DIM), (0, fpad))), row(hy_ffn_b1),
                            row(hy_ffn_freq), jnp.pad(hy_ffn_w2, ((0, fpad), (0, fpad))), row(hy_ffn_b2),
                            jnp.pad(hy_ffn_w3, ((0, fpad), (0, 0))), deltas)
    z, x0 = _hyena_pre(proj, hy_conv_w, hy_conv_b)
    fac_fwd = _dft_factors(n, inverse=False)
    fac_inv = _dft_factors(n, inverse=True)
    tile = min(512, n)
    spec_a, spec_b = _dft_filters(fac_fwd, fs, fd, tf=tile, tk=min(2048, n))
    tn = min(2048, bsz * HY_W)
    tk = min(1024, n)
    u, v = _dft_fwd(fac_fwd, z, spec_a, spec_b, tf=tile, tn=tn, tk=tk)
    skip_t = jnp.tile(hy_skip.reshape(1, HY_W), (1, bsz))
    hy = _dft_inv(fac_inv, u, v, z, x0, skip_t, tt=tile, tn=tn, tk=tk)

    rw = jnp.zeros((d, ROUTE_LANES), F32)
    rw = rw.at[:, :N_GROUPS].set(router_group_w).at[:, EXPERT_LANE0:EXPERT_LANE0 + N_EXPERTS].set(router_expert_w)
    rb = jnp.zeros((1, ROUTE_LANES), F32)
    rb = rb.at[0, :N_GROUPS].set(router_group_b).at[0, EXPERT_LANE0:EXPERT_LANE0 + N_EXPERTS].set(router_expert_b)
    rw_hi, rw_lo = _split_bf16(rw)
    sub = min(512, n)
    x1, h2, route, route_t, cnt = _merge(ret, hy, proj, x, gt1, sh2, sc2, norm2_g.reshape(1, d),
                                ret_w_o.astype(BF16), hy_w_o.astype(BF16), w_out.astype(BF16),
                                rw_hi, rw_lo, rb, tm=min(2 * sub, n), sub=sub)

    t_all = bsz * n
    counts = cnt[0, EXPERT_LANE0:EXPERT_LANE0 + N_EXPERTS].astype(jnp.int32)
    padded = (counts + MOE_BLK - 1) // MOE_BLK * MOE_BLK
    pad_end = jnp.cumsum(padded)
    pad_start = pad_end - padded
    def dest_of(expert, rank):
        start = jnp.zeros_like(rank)
        for e in range(N_EXPERTS):
            start = jnp.where(expert == e, pad_start[e], start)
        return start + rank

    ri = route_t.astype(jnp.int32)
    dest_a, dest_b = dest_of(ri[0], ri[4]), dest_of(ri[1], ri[5])
    n_blocks = -(-(2 * t_all + N_EXPERTS * (MOE_BLK - 1)) // MOE_BLK)
    blk0 = jnp.arange(n_blocks, dtype=jnp.int32) * MOE_BLK
    block_expert = jnp.minimum(jnp.sum(blk0[:, None] >= pad_end[None, :], axis=1), N_EXPERTS - 1).astype(jnp.int32)
    n_used = (pad_end[-1:] // MOE_BLK).astype(jnp.int32)

    xb = _dispatch(dest_a, dest_b, pad_end.astype(jnp.int32), h2.reshape(t_all, h2.shape[-1]),
                   n_blocks * MOE_BLK, tm=min(256, n))
    yb = _experts(block_expert, n_used, xb, expert_w1, expert_w3, expert_w2)
    return _combine(dest_a, dest_b, x1, route, gt2, final_norm_g.reshape(1, d), yb, tm=min(256, n))


def kernel(x, c, ctx, c_ctx, ada_w, ada_b, norm1_g, norm2_g, w_in, b_in, ret_decay_logit, ret_w_o, hy_conv_w, hy_conv_b, hy_ffn_w1, hy_ffn_b1, hy_ffn_freq, hy_ffn_w2, hy_ffn_b2, hy_ffn_w3, hy_skip, hy_w_o, w_out, router_group_w, router_group_b, router_expert_w, router_expert_b, expert_w1, expert_w3, expert_w2, final_norm_g):
    depth = ada_w.shape[0]
    assert depth == 1, "single-layer problem: the context stream is only read by the retention states"
    bsz, d = c.shape
    rows = -(-(bsz + 1) // 8) * 8
    cc = jnp.zeros((rows, d), F32).at[:bsz].set(c).at[bsz].set(c_ctx)
    mods = _adaln(cc, ada_w[0], ada_b[0].reshape(1, -1))
    return _layer(x, ctx, mods, norm1_g[0], norm2_g[0], w_in[0], b_in[0], ret_decay_logit[0],
                  ret_w_o[0], hy_conv_w[0], hy_conv_b[0], hy_ffn_w1[0], hy_ffn_b1[0], hy_ffn_freq[0],
                  hy_ffn_w2[0], hy_ffn_b2[0], hy_ffn_w3[0], hy_skip[0], hy_w_o[0], w_out[0],
                  router_group_w[0], router_group_b[0], router_expert_w[0], router_expert_b[0],
                  expert_w1[0], expert_w3[0], expert_w2[0], final_norm_g)
```

```python
import functools
import math

import numpy as np

import jax
import jax.numpy as jnp
from jax import lax
from jax.experimental import pallas as pl
from jax.experimental.pallas import tpu as pltpu

F32 = jnp.float32
BF16 = jnp.bfloat16

D_MODEL = 1024
EPS = 1e-6
GRID_W = 64
ROPE_BASE = 10000.0

RET_HEADS = 4
RET_QK_DIM = 128
RET_V_DIM = 256
RET_CHUNK = 256
RET_UNROLL = 2
QK_W = RET_HEADS * RET_QK_DIM
V_W = RET_HEADS * RET_V_DIM

HY_W = 512
HY_POS_DIM = 33
HY_BANDS = (HY_POS_DIM - 1) // 2
HY_FFN = 64
FFN_LANES = 128
HY_DECAY_TARGET = 1e-2
HY_FAST_PCT = 0.3
HY_SLOW_PCT = 1.5
DFT_GROUP = 64

N_GROUPS = 4
EXPERTS_PER_GROUP = 8
N_EXPERTS = N_GROUPS * EXPERTS_PER_GROUP
EXPERT_HIDDEN = 512
ROUTE_LANES = 128
EXPERT_LANE0 = N_GROUPS
MOE_BLK = 512
SUBLANES = 8

IN_W = 2 * QK_W + 2 * V_W + 3 * HY_W + 2 * D_MODEL
COL_Q, COL_K, COL_V, COL_G = 0, QK_W, 2 * QK_W, 2 * QK_W + V_W
COL_HY = 2 * QK_W + 2 * V_W
COL_GR = COL_HY + 3 * HY_W
COL_GH = COL_GR + D_MODEL

VMEM_LIMIT = 56 * 1024 * 1024
NEG = -1e30


def _cparams(*sem):
    return pltpu.CompilerParams(dimension_semantics=sem, vmem_limit_bytes=VMEM_LIMIT)


def _sigmoid(x):
    return 1.0 / (1.0 + jnp.exp(-x))


def _dot(a, b):
    return jnp.dot(a, b, preferred_element_type=F32)


def _dot_t0(a, b):
    return lax.dot_general(a, b, (((0,), (0,)), ((), ())), preferred_element_type=F32)


def _dot_nt(a, b):
    return lax.dot_general(a, b, (((1,), (1,)), ((), ())), preferred_element_type=F32)


def _split_bf16(a):
    hi = a.astype(BF16)
    lo = (a - hi.astype(F32)).astype(BF16)
    return hi, lo


def _pack_halves(x):
    w = x.shape[1] // 2
    return pltpu.pack_elementwise([x[:, :w], x[:, w:]], packed_dtype=BF16)


def _unpack_halves(p):
    lo = pltpu.unpack_elementwise(p, index=0, packed_dtype=BF16, unpacked_dtype=F32)
    hi = pltpu.unpack_elementwise(p, index=1, packed_dtype=BF16, unpacked_dtype=F32)
    return jnp.concatenate([lo, hi], axis=1)


def _dot3(a, b):
    ah, al = _split_bf16(a)
    bh, bl = _split_bf16(b)
    return _dot(ah, bh) + _dot(al, bh) + _dot(ah, bl)


def _adaln_kernel(c_ref, w_ref, b_ref, o_ref):
    c = c_ref[...]
    o_ref[...] = _dot3(c * _sigmoid(c), w_ref[...]) + b_ref[...]


def _adaln(cc, w, b):
    rows, d = cc.shape
    n = w.shape[1]
    tn = 1536
    return pl.pallas_call(
        _adaln_kernel,
        grid=(n // tn,),
        in_specs=[pl.BlockSpec((rows, d), lambda j: (0, 0)),
                  pl.BlockSpec((d, tn), lambda j: (0, j)),
                  pl.BlockSpec((1, tn), lambda j: (0, j))],
        out_specs=pl.BlockSpec((rows, tn), lambda j: (0, j)),
        out_shape=jax.ShapeDtypeStruct((rows, n), F32),
        compiler_params=_cparams("arbitrary"),
        name="adaln",
    )(cc, w, b)


def _inproj_kernel(x_ref, g_ref, sh_ref, sc_ref, w_ref, b_ref, o_ref, *, tn):
    x = x_ref[0]
    y = x * lax.rsqrt(jnp.mean(x * x, axis=-1, keepdims=True) + EPS) * g_ref[...]
    h = (y * (1.0 + sc_ref[0]) + sh_ref[0]).astype(BF16)
    for j in range(w_ref.shape[1] // tn):
        cols = slice(j * tn, (j + 1) * tn)
        o_ref[0, :, cols] = (_dot(h, w_ref[:, cols]) + b_ref[:, cols]).astype(o_ref.dtype)


def _inproj(x, gain, shift, scale, w, b, tm, tn):
    bsz, n, d = x.shape
    nw = w.shape[1]
    return pl.pallas_call(
        functools.partial(_inproj_kernel, tn=tn),
        grid=(bsz, n // tm),
        in_specs=[pl.BlockSpec((1, tm, d), lambda bi, i: (bi, i, 0)),
                  pl.BlockSpec((1, d), lambda bi, i: (0, 0)),
                  pl.BlockSpec((1, 1, d), lambda bi, i: (bi, 0, 0)),
                  pl.BlockSpec((1, 1, d), lambda bi, i: (bi, 0, 0)),
                  pl.BlockSpec((d, nw), lambda bi, i: (0, 0)),
                  pl.BlockSpec((1, nw), lambda bi, i: (0, 0))],
        out_specs=pl.BlockSpec((1, tm, nw), lambda bi, i: (bi, i, 0)),
        out_shape=jax.ShapeDtypeStruct((bsz, n, nw), BF16),
        compiler_params=_cparams("arbitrary", "arbitrary"),
        name="inproj",
    )(x, gain, shift, scale, w, b)


def _log_sigmoid(x):
    return jnp.minimum(x, 0.0) - jnp.log(1.0 + jnp.exp(-jnp.abs(x)))


def _rope_partner(x):
    lane = lax.broadcasted_iota(jnp.int32, x.shape, 1)
    return jnp.where((lane % 64) < 32, pltpu.roll(x, 96, axis=1), pltpu.roll(x, 32, axis=1))


def _retention_kernel(q_ref, k_ref, v_ref, g_ref, kc_ref, vc_ref, cos_ref, sin_ref, lgt_ref,
                      o_ref, qr_scr, kr_scr, oacc_scr, sf_scr, sb_scr, *, n_tok, n_ctx):
    c_len = RET_CHUNK
    n_chunks = n_tok // c_len
    head = pl.program_id(1)
    k_scale = RET_QK_DIM ** -0.5

    lg_f = _log_sigmoid(lgt_ref[pl.ds(head, 1), :])
    lg_b = _log_sigmoid(lgt_ref[pl.ds(RET_HEADS + head, 1), :])
    lgf_k, lgb_k = lg_f[:, :RET_QK_DIM], lg_b[:, :RET_QK_DIM]

    ii = lax.broadcasted_iota(jnp.int32, (c_len, c_len), 0)
    jj = lax.broadcasted_iota(jnp.int32, (c_len, c_len), 1)
    dif = (ii - jj).astype(F32)
    decay_in = jnp.where(ii >= jj, jnp.exp(lg_f[:, :c_len] * jnp.maximum(dif, 0.0)), 0.0) + \
        jnp.where(jj > ii, jnp.exp(lg_b[:, :c_len] * jnp.maximum(-dif, 0.0)), 0.0)
    pos_k = lax.broadcasted_iota(jnp.int32, (c_len, RET_QK_DIM), 0).astype(F32)
    pos_v = lax.broadcasted_iota(jnp.int32, (c_len, RET_V_DIM), 0).astype(F32)
    dq_f = jnp.exp(lg_f * (pos_v + 1.0))
    dq_b = jnp.exp(lg_b * (c_len - pos_v))
    dk_f = jnp.exp(lgf_k * (c_len - 1.0 - pos_k))
    dk_b = jnp.exp(lgb_k * pos_k)
    dchunk_f = jnp.exp(lg_f * float(c_len))
    dchunk_b = jnp.exp(lg_b * float(c_len))

    pos_c = lax.broadcasted_iota(jnp.int32, (n_ctx, RET_QK_DIM), 0).astype(F32)
    kc = kc_ref[0].astype(F32) * k_scale
    vc = vc_ref[0]
    sf_scr[...] = _dot_t0((kc * jnp.exp(lgf_k * (n_ctx - 1.0 - pos_c))).astype(BF16), vc)
    sb_scr[...] = _dot_t0((kc * jnp.exp(lgb_k * pos_c)).astype(BF16), vc)

    def fwd(c, carry):
        r0 = pl.multiple_of(c * c_len, c_len)
        rows = pl.ds(r0, c_len)
        cos = cos_ref[rows, :]
        sin = sin_ref[rows, :]
        q = q_ref[0, rows, :].astype(F32)
        k = k_ref[0, rows, :].astype(F32)
        qr = q * cos + _rope_partner(q) * sin
        kr = (k * cos + _rope_partner(k) * sin) * k_scale
        qb = qr.astype(BF16)
        kb = kr.astype(BF16)
        qr_scr[rows, :] = qb
        kr_scr[rows, :] = kr
        v = v_ref[0, rows, :]
        scores = _dot_nt(qb, kb) * decay_in
        o = _dot(scores.astype(BF16), v) + _dot(qb, sf_scr[...].astype(BF16)) * dq_f
        oacc_scr[rows, :] = o
        sf_scr[...] = sf_scr[...] * dchunk_f + _dot_t0((kr * dk_f).astype(BF16), v)
        return carry

    lax.fori_loop(0, n_chunks, fwd, 0, unroll=RET_UNROLL)

    def bwd(t, carry):
        c = n_chunks - 1 - t
        r0 = pl.multiple_of(c * c_len, c_len)
        rows = pl.ds(r0, c_len)
        v = v_ref[0, rows, :]
        o = oacc_scr[rows, :] + _dot(qr_scr[rows, :], sb_scr[...].astype(BF16)) * dq_b
        o = o * lax.rsqrt(jnp.mean(o * o, axis=-1, keepdims=True) + EPS)
        g = g_ref[0, rows, :].astype(F32)
        o_ref[0, rows, :] = (g * _sigmoid(g) * o).astype(o_ref.dtype)
        sb_scr[...] = sb_scr[...] * dchunk_b + _dot_t0((kr_scr[rows, :] * dk_b).astype(BF16), v)
        return carry

    lax.fori_loop(0, n_chunks, bwd, 0, unroll=RET_UNROLL)


def _retention(proj, proj_c, cos_t, sin_t, lgt):
    bsz, n, _ = proj.shape
    n_ctx = proj_c.shape[1]
    kq, kv = RET_QK_DIM, RET_V_DIM
    assert n % RET_CHUNK == 0 and RET_CHUNK <= kv
    kern = functools.partial(_retention_kernel, n_tok=n, n_ctx=n_ctx)
    return pl.pallas_call(
        kern,
        grid=(bsz, RET_HEADS),
        in_specs=[pl.BlockSpec((1, n, kq), lambda b, h: (b, 0, COL_Q // kq + h)),
                  pl.BlockSpec((1, n, kq), lambda b, h: (b, 0, COL_K // kq + h)),
                  pl.BlockSpec((1, n, kv), lambda b, h: (b, 0, COL_V // kv + h)),
                  pl.BlockSpec((1, n, kv), lambda b, h: (b, 0, COL_G // kv + h)),
                  pl.BlockSpec((1, n_ctx, kq), lambda b, h: (b, 0, h)),
                  pl.BlockSpec((1, n_ctx, kv), lambda b, h: (b, 0, QK_W // kv + h)),
                  pl.BlockSpec((n, kq), lambda b, h: (0, 0)),
                  pl.BlockSpec((n, kq), lambda b, h: (0, 0)),
                  pl.BlockSpec((2 * RET_HEADS, kv), lambda b, h: (0, 0))],
        out_specs=pl.BlockSpec((1, n, kv), lambda b, h: (b, 0, h)),
        out_shape=jax.ShapeDtypeStruct((bsz, n, V_W), BF16),
        scratch_shapes=[pltpu.VMEM((n, kq), BF16), pltpu.VMEM((n, kq), F32),
                        pltpu.VMEM((n, kv), F32), pltpu.VMEM((kq, kv), F32),
                        pltpu.VMEM((kq, kv), F32)],
        compiler_params=_cparams("arbitrary", "arbitrary"),
        name="retention",
    )(proj, proj, proj, proj, proj_c, proj_c, cos_t, sin_t, lgt)


def _filter_kernel(feats_ref, w1_ref, b1_ref, fr_ref, w2_ref, b2_ref, w3f_ref, w3b_ref,
                   df_ref, db_ref, fs_ref, fd_ref, *, n_tok):
    fr = fr_ref[...]
    hid = jnp.sin(fr * (_dot3(feats_ref[...], w1_ref[...]) + b1_ref[...]))
    hid = jnp.sin(fr * (_dot3(hid, w2_ref[...]) + b2_ref[...]))
    t = lax.broadcasted_iota(jnp.int32, (n_tok, df_ref.shape[1]), 0).astype(F32) / n_tok

    def one(w3_ref, d_ref):
        f = _dot3(hid, w3_ref[...]) * jnp.exp(-t * d_ref[...])
        return f / jnp.sum(jnp.abs(f), axis=0, keepdims=True)

    hf = one(w3f_ref, df_ref)
    hb = one(w3b_ref, db_ref)
    fs_ref[...] = (hf + hb).astype(fs_ref.dtype)
    fd_ref[...] = (hf - hb).astype(fd_ref.dtype)


def _hyena_filters(n, feats, w1, b1, freq, w2, b2, w3, deltas):
    tc = HY_W
    nf = feats.shape[1]
    kern = functools.partial(_filter_kernel, n_tok=n)
    full = lambda shape: pl.BlockSpec(shape, lambda j: (0, 0))
    return pl.pallas_call(
        kern,
        grid=(HY_W // tc,),
        in_specs=[full((n, nf)), full((nf, FFN_LANES)), full((1, FFN_LANES)), full((1, FFN_LANES)),
                  full((FFN_LANES, FFN_LANES)), full((1, FFN_LANES)),
                  pl.BlockSpec((FFN_LANES, tc), lambda j: (0, j)),
                  pl.BlockSpec((FFN_LANES, tc), lambda j: (0, HY_W // tc + j)),
                  pl.BlockSpec((1, tc), lambda j: (0, j)),
                  pl.BlockSpec((1, tc), lambda j: (0, HY_W // tc + j))],
        out_specs=[pl.BlockSpec((n, tc), lambda j: (0, j)),
                   pl.BlockSpec((n, tc), lambda j: (0, j))],
        out_shape=[jax.ShapeDtypeStruct((n, HY_W), BF16)] * 2,
        compiler_params=_cparams("arbitrary"),
        name="hyena_filters",
    )(feats, w1, b1, freq, w2, b2, w3, w3, deltas, deltas)


def _hyena_pre_kernel(u0_ref, u1_ref, u2_ref, w_ref, b_ref, z_ref, x0_ref, *, n_tok, rows):
    n_steps = n_tok // rows
    tc = z_ref.shape[1]
    halo = 16
    rid = lax.broadcasted_iota(jnp.int32, (rows, tc), 0)

    def conv(u_ref, part, r0, has_prev, has_next):
        x = u_ref[0, pl.ds(r0, rows), :].astype(F32)
        prev_g = u_ref[0, pl.ds(pl.multiple_of(jnp.maximum(r0 - halo, 0), halo), halo), :].astype(F32)
        next_g = u_ref[0, pl.ds(pl.multiple_of(jnp.minimum(r0 + rows, n_tok - halo), halo), halo), :].astype(F32)
        prev_row = jnp.where(has_prev, prev_g[halo - 1:halo, :], 0.0)
        next_row = jnp.where(has_next, next_g[0:1, :], 0.0)
        up = jnp.where(rid == 0, prev_row, pltpu.roll(x, 1, axis=0))
        dn = jnp.where(rid == rows - 1, next_row, pltpu.roll(x, rows - 1, axis=0))
        w = w_ref[part]
        return up * w[0:1, :] + x * w[1:2, :] + dn * w[2:3, :] + b_ref[part]

    def body(s, carry):
        r0 = pl.multiple_of(s * rows, rows)
        has_prev = s > 0
        has_next = s < n_steps - 1
        x0 = conv(u0_ref, 0, r0, has_prev, has_next)
        x1 = conv(u1_ref, 1, r0, has_prev, has_next)
        vv = conv(u2_ref, 2, r0, has_prev, has_next)
        z_ref[pl.ds(r0, rows), :] = (x1 * vv).astype(z_ref.dtype)
        x0_ref[pl.ds(r0, rows), :] = x0.astype(x0_ref.dtype)
        return carry

    lax.fori_loop(0, n_steps, body, 0)


def _hyena_pre(proj, conv_w, conv_b):
    bsz, n, _ = proj.shape
    tc = 256
    nj = HY_W // tc
    rows = min(512, n)
    base = COL_HY // tc
    kern = functools.partial(_hyena_pre_kernel, n_tok=n, rows=rows)
    u_spec = lambda part: pl.BlockSpec((1, n, tc), lambda b, j: (b, 0, base + part * nj + j))
    wp = jnp.zeros((3, 8, HY_W), F32).at[:, :3, :].set(conv_w.reshape(3, 3, HY_W).transpose(1, 0, 2))
    bp = conv_b.reshape(3, 1, HY_W)
    return pl.pallas_call(
        kern,
        grid=(bsz, nj),
        in_specs=[u_spec(0), u_spec(1), u_spec(2),
                  pl.BlockSpec((3, 8, tc), lambda b, j: (0, 0, j)),
                  pl.BlockSpec((3, 1, tc), lambda b, j: (0, 0, j))],
        out_specs=[pl.BlockSpec((n, tc), lambda b, j: (0, b * nj + j)),
                   pl.BlockSpec((n, tc), lambda b, j: (0, b * nj + j))],
        out_shape=[jax.ShapeDtypeStruct((n, bsz * HY_W), BF16)] * 2,
        compiler_params=_cparams("arbitrary", "arbitrary"),
        name="hyena_pre",
    )(proj, proj, proj, wp, bp)


def _dft_factors(n, inverse):
    period = 4 * n
    col = np.arange(n, dtype=np.int64)[None, :]
    hi = np.arange(n // DFT_GROUP, dtype=np.int64)[:, None]
    lo = np.arange(DFT_GROUP, dtype=np.int64)[:, None]
    if inverse:
        m_a, m_b = (2 * col + 1) * (DFT_GROUP * hi), (2 * col + 1) * lo
    else:
        m_a, m_b = (2 * DFT_GROUP * hi) * col, (2 * lo + 1) * col
    ang_a = (m_a % period) * (2.0 * math.pi / period)
    ang_b = (m_b % period) * (2.0 * math.pi / period)
    return tuple(jnp.asarray(t, F32) for t in (np.cos(ang_a), np.sin(ang_a), np.cos(ang_b), np.sin(ang_b)))


def _dft_tile(ca_ref, sa_ref, cb_ref, sb_ref):
    cb = cb_ref[...]
    sb = sb_ref[...]
    cos_rows, sin_rows = [], []
    for g in range(ca_ref.shape[0]):
        ca = ca_ref[g:g + 1, :]
        sa = sa_ref[g:g + 1, :]
        cos_rows.append((ca * cb - sa * sb).astype(BF16))
        sin_rows.append((sa * cb + ca * sb).astype(BF16))
    return jnp.concatenate(cos_rows, axis=0), jnp.concatenate(sin_rows, axis=0)


def _factor_specs(tr, tc, row_axis, col_axis):
    a_spec = pl.BlockSpec((tr // DFT_GROUP, tc), lambda *ids: (ids[row_axis], ids[col_axis]))
    b_spec = pl.BlockSpec((DFT_GROUP, tc), lambda *ids: (0, ids[col_axis]))
    return [a_spec, a_spec, b_spec, b_spec]


def _dft_filter_kernel(ca_ref, sa_ref, cb_ref, sb_ref, fs_ref, fd_ref, a_ref, b_ref, acc_a, acc_b):
    kk = pl.program_id(1)

    @pl.when(kk == 0)
    def _():
        acc_a[...] = jnp.zeros_like(acc_a)
        acc_b[...] = jnp.zeros_like(acc_b)

    cos_t, sin_t = _dft_tile(ca_ref, sa_ref, cb_ref, sb_ref)
    acc_a[...] += _dot(cos_t, fs_ref[...])
    acc_b[...] += _dot(sin_t, fd_ref[...])

    @pl.when(kk == pl.num_programs(1) - 1)
    def _():
        a_ref[...] = acc_a[...]
        b_ref[...] = acc_b[...]


def _dft_filters(factors, fs, fd, tf, tk):
    n, w = fs.shape
    return pl.pallas_call(
        _dft_filter_kernel,
        grid=(n // tf, n // tk),
        in_specs=_factor_specs(tf, tk, 0, 1) + [
            pl.BlockSpec((tk, w), lambda i, kk: (kk, 0)),
            pl.BlockSpec((tk, w), lambda i, kk: (kk, 0))],
        out_specs=[pl.BlockSpec((tf, w), lambda i, kk: (i, 0)),
                   pl.BlockSpec((tf, w), lambda i, kk: (i, 0))],
        out_shape=[jax.ShapeDtypeStruct((n, w), F32)] * 2,
        scratch_shapes=[pltpu.VMEM((tf, w), F32), pltpu.VMEM((tf, w), F32)],
        compiler_params=_cparams("arbitrary", "arbitrary"),
        name="dft_filters",
    )(*factors, fs, fd)


def _dft_fwd_kernel(ca_ref, sa_ref, cb_ref, sb_ref, z_ref, a_ref, b_ref, u_ref, v_ref, acc_p, acc_q):
    kk = pl.program_id(2)

    @pl.when(kk == 0)
    def _():
        acc_p[...] = jnp.zeros_like(acc_p)
        acc_q[...] = jnp.zeros_like(acc_q)

    cos_t, sin_t = _dft_tile(ca_ref, sa_ref, cb_ref, sb_ref)
    z = z_ref[...]
    acc_p[...] += _dot(cos_t, z)
    acc_q[...] += _dot(sin_t, z)

    @pl.when(kk == pl.num_programs(2) - 1)
    def _():
        a = a_ref[...]
        b = b_ref[...]
        w = a.shape[1]
        for s in range(u_ref.shape[1] // w):
            cols = slice(s * w, (s + 1) * w)
            p = acc_p[:, cols]
            q = acc_q[:, cols]
            u_ref[:, cols] = (p * a - q * b).astype(u_ref.dtype)
            v_ref[:, cols] = (p * b + q * a).astype(v_ref.dtype)


def _dft_fwd(factors, z, spec_a, spec_b, tf, tn, tk):
    n, ncol = z.shape
    w = spec_a.shape[1]
    return pl.pallas_call(
        _dft_fwd_kernel,
        grid=(n // tf, ncol // tn, n // tk),
        in_specs=_factor_specs(tf, tk, 0, 2) + [
            pl.BlockSpec((tk, tn), lambda i, j, kk: (kk, j)),
            pl.BlockSpec((tf, w), lambda i, j, kk: (i, 0)),
            pl.BlockSpec((tf, w), lambda i, j, kk: (i, 0))],
        out_specs=[pl.BlockSpec((tf, tn), lambda i, j, kk: (i, j)),
                   pl.BlockSpec((tf, tn), lambda i, j, kk: (i, j))],
        out_shape=[jax.ShapeDtypeStruct((n, ncol), BF16)] * 2,
        scratch_shapes=[pltpu.VMEM((tf, tn), F32), pltpu.VMEM((tf, tn), F32)],
        compiler_params=_cparams("arbitrary", "arbitrary", "arbitrary"),
        name="dft_fwd",
    )(*factors, z, spec_a, spec_b)


def _dft_inv_kernel(ca_ref, sa_ref, cb_ref, sb_ref, u_ref, v_ref, z_ref, x0_ref, skip_ref, o_ref, acc,
                    *, inv_n):
    kk = pl.program_id(2)

    @pl.when(kk == 0)
    def _():
        acc[...] = jnp.zeros_like(acc)

    cos_t, sin_t = _dft_tile(ca_ref, sa_ref, cb_ref, sb_ref)
    acc[...] += _dot(cos_t, u_ref[...]) + _dot(sin_t, v_ref[...])

    @pl.when(kk == pl.num_programs(2) - 1)
    def _():
        y = acc[...] * inv_n + z_ref[...].astype(F32) * skip_ref[...]
        o_ref[...] = (x0_ref[...].astype(F32) * y).astype(o_ref.dtype)


def _dft_inv(factors, u, v, z, x0, skip_t, tt, tn, tk):
    n, ncol = u.shape
    kern = functools.partial(_dft_inv_kernel, inv_n=1.0 / n)
    return pl.pallas_call(
        kern,
        grid=(n // tt, ncol // tn, n // tk),
        in_specs=_factor_specs(tt, tk, 0, 2) + [
                  pl.BlockSpec((tk, tn), lambda i, j, kk: (kk, j)),
                  pl.BlockSpec((tk, tn), lambda i, j, kk: (kk, j)),
                  pl.BlockSpec((tt, tn), lambda i, j, kk: (i, j)),
                  pl.BlockSpec((tt, tn), lambda i, j, kk: (i, j)),
                  pl.BlockSpec((1, tn), lambda i, j, kk: (0, j))],
        out_specs=pl.BlockSpec((tt, tn), lambda i, j, kk: (i, j)),
        out_shape=jax.ShapeDtypeStruct((n, ncol), BF16),
        scratch_shapes=[pltpu.VMEM((tt, tn), F32)],
        compiler_params=_cparams("arbitrary", "arbitrary", "arbitrary"),
        name="dft_inv",
    )(*factors, u, v, z, x0, skip_t)


def _merge_kernel(ret_ref, hy_ref, gr0_ref, gr1_ref, gh0_ref, gh1_ref, x_ref, gt1_ref, sh2_ref,
                  sc2_ref, g2_ref, wro_ref, who_ref, wout_ref, rwh_ref, rwl_ref, rb_ref,
                  x1_ref, h2_ref, route_ref, routet_ref, cnt_ref, base_scr, *, sub):
    first = jnp.logical_and(pl.program_id(0) == 0, pl.program_id(1) == 0)

    @pl.when(first)
    def _():
        base_scr[...] = jnp.zeros_like(base_scr)

    for s0 in range(0, x_ref.shape[1], sub):
        _merge_rows(slice(s0, s0 + sub), ret_ref, hy_ref, gr0_ref, gr1_ref, gh0_ref, gh1_ref, x_ref,
                    gt1_ref, sh2_ref, sc2_ref, g2_ref, wro_ref, who_ref, wout_ref, rwh_ref, rwl_ref,
                    rb_ref, x1_ref, h2_ref, route_ref, routet_ref, base_scr)
    cnt_ref[...] = base_scr[...]


def _merge_rows(rows, ret_ref, hy_ref, gr0_ref, gr1_ref, gh0_ref, gh1_ref, x_ref, gt1_ref, sh2_ref,
                sc2_ref, g2_ref, wro_ref, who_ref, wout_ref, rwh_ref, rwl_ref, rb_ref,
                x1_ref, h2_ref, route_ref, routet_ref, base_scr):
    tm = rows.stop - rows.start
    gate_r = jnp.concatenate([gr0_ref[0, rows, :], gr1_ref[0, rows, :]], axis=1).astype(F32)
    gate_h = jnp.concatenate([gh0_ref[0, rows, :], gh1_ref[0, rows, :]], axis=1).astype(F32)
    mixed = _sigmoid(gate_r) * _dot(ret_ref[0, rows, :], wro_ref[...]) + \
        _sigmoid(gate_h) * _dot(hy_ref[rows, :], who_ref[...])
    x1 = x_ref[0, rows, :] + gt1_ref[0] * _dot(mixed.astype(BF16), wout_ref[...])
    x1_ref[0, rows, :] = x1
    h2 = x1 * lax.rsqrt(jnp.mean(x1 * x1, axis=-1, keepdims=True) + EPS) * g2_ref[...]
    h2 = h2 * (1.0 + sc2_ref[0]) + sh2_ref[0]
    h2_ref[0, rows, :] = _pack_halves(h2)

    h2_hi, h2_lo = _split_bf16(h2)
    rw_hi = rwh_ref[...]
    logits = _dot(h2_hi, rw_hi) + _dot(h2_lo, rw_hi) + _dot(h2_hi, rwl_ref[...]) + rb_ref[...]
    lane = lax.broadcasted_iota(jnp.int32, logits.shape, 1)
    lane_f = lane.astype(F32)
    big = float(ROUTE_LANES)

    def first_lane(mask):
        return jnp.min(jnp.where(mask, lane_f, big), axis=1, keepdims=True)

    is_group = lane < N_GROUPS
    gl = jnp.where(is_group, logits, NEG)
    ge = jnp.where(is_group, jnp.exp(gl - jnp.max(gl, axis=1, keepdims=True)), 0.0)
    group_p = ge / jnp.sum(ge, axis=1, keepdims=True)
    p_star = jnp.max(group_p, axis=1, keepdims=True)
    g_star = first_lane(jnp.logical_and(is_group, group_p == p_star))
    lo = EXPERT_LANE0 + g_star * EXPERTS_PER_GROUP
    in_group = jnp.logical_and(lane_f >= lo, lane_f < lo + EXPERTS_PER_GROUP)
    el = jnp.where(in_group, logits, NEG)
    ee = jnp.where(in_group, jnp.exp(el - jnp.max(el, axis=1, keepdims=True)), 0.0)
    sp = jnp.where(in_group, ee / jnp.sum(ee, axis=1, keepdims=True), -1.0)
    w_a = jnp.max(sp, axis=1, keepdims=True)
    l_a = first_lane(sp == w_a)
    sp2 = jnp.where(lane_f == l_a, -1.0, sp)
    w_b = jnp.max(sp2, axis=1, keepdims=True)
    l_b = first_lane(sp2 == w_b)
    wsum = w_a + w_b
    wt_a = p_star * w_a / wsum
    wt_b = p_star * w_b / wsum

    hit_a = lane_f == l_a
    hit_b = lane_f == l_b
    onehot = jnp.where(jnp.logical_or(hit_a, hit_b), 1.0, 0.0)
    ri = lax.broadcasted_iota(jnp.int32, (tm, tm), 0)
    ci = lax.broadcasted_iota(jnp.int32, (tm, tm), 1)
    tri = jnp.where(ri > ci, 1.0, 0.0).astype(BF16)
    before = _dot(tri, onehot.astype(BF16)) + base_scr[0:1, :]
    rank_a = jnp.sum(jnp.where(hit_a, before, 0.0), axis=1, keepdims=True)
    rank_b = jnp.sum(jnp.where(hit_b, before, 0.0), axis=1, keepdims=True)
    base_scr[0:1, :] = base_scr[0:1, :] + jnp.sum(onehot, axis=0, keepdims=True)

    vals = (l_a - EXPERT_LANE0, l_b - EXPERT_LANE0, wt_a, wt_b, rank_a, rank_b)
    route = jnp.zeros(logits.shape, F32)
    for idx, val in enumerate(vals):
        route = jnp.where(lane == idx, val, route)
    route_ref[0, rows, :] = route
    routet_ref[:, rows] = route.T[:SUBLANES, :]


def _merge(ret, hy, proj, x, gt1, sh2, sc2, g2, w_ro, w_ho, w_out, rw_hi, rw_lo, rb, tm, sub):
    bsz, n, d = x.shape
    nt = n // tm
    hw = HY_W
    gspec = lambda col: pl.BlockSpec((1, tm, hw), lambda b, i: (b, i, col // hw))
    vec = pl.BlockSpec((1, 1, d), lambda b, i: (b, 0, 0))
    full = lambda shape: pl.BlockSpec(shape, lambda b, i: (0, 0))
    tok = pl.BlockSpec((1, tm, d), lambda b, i: (b, i, 0))
    return pl.pallas_call(
        functools.partial(_merge_kernel, sub=sub),
        grid=(bsz, nt),
        in_specs=[tok,
                  pl.BlockSpec((tm, hw), lambda b, i: (i, b)),
                  gspec(COL_GR), gspec(COL_GR + hw), gspec(COL_GH), gspec(COL_GH + hw),
                  tok, vec, vec, vec, full((1, d)),
                  full((V_W, d)), full((hw, d)), full((d, d)),
                  full((d, ROUTE_LANES)), full((d, ROUTE_LANES)), full((1, ROUTE_LANES))],
        out_specs=[tok, pl.BlockSpec((1, tm, d // 2), lambda b, i: (b, i, 0)),
                   pl.BlockSpec((1, tm, ROUTE_LANES), lambda b, i: (b, i, 0)),
                   pl.BlockSpec((SUBLANES, tm), lambda b, i: (0, b * nt + i)),
                   pl.BlockSpec((8, ROUTE_LANES), lambda b, i: (0, 0))],
        out_shape=[jax.ShapeDtypeStruct((bsz, n, d), F32),
                   jax.ShapeDtypeStruct((bsz, n, d // 2), jnp.int32),
                   jax.ShapeDtypeStruct((bsz, n, ROUTE_LANES), F32),
                   jax.ShapeDtypeStruct((SUBLANES, bsz * n), F32),
                   jax.ShapeDtypeStruct((8, ROUTE_LANES), F32)],
        scratch_shapes=[pltpu.VMEM((8, ROUTE_LANES), F32)],
        compiler_params=_cparams("arbitrary", "arbitrary"),
        name="merge_route",
    )(ret, hy, proj, proj, proj, proj, x, gt1, sh2, sc2, g2, w_ro, w_ho, w_out, rw_hi, rw_lo, rb)


def _dest_kernel(ps_ref, rt_ref, o_ref):
    x = rt_ref[...]
    start = jnp.zeros(x.shape, jnp.int32)
    for e in range(N_EXPERTS):
        start = jnp.where(x == float(e), ps_ref[e], start)
    o_ref[...] = start + pltpu.roll(x.astype(jnp.int32), SUBLANES // 2, axis=0)


def _dest_rows(pad_start, route_t):
    rows, t_all = route_t.shape
    return pl.pallas_call(
        _dest_kernel,
        grid_spec=pltpu.PrefetchScalarGridSpec(
            num_scalar_prefetch=1,
            grid=(1,),
            in_specs=[pl.BlockSpec((rows, t_all), lambda i, ps: (0, 0))],
            out_specs=pl.BlockSpec((rows, t_all), lambda i, ps: (0, 0))),
        out_shape=jax.ShapeDtypeStruct((rows, t_all), jnp.int32),
        compiler_params=_cparams("arbitrary"),
        name="moe_dest",
    )(pad_start, route_t)


def _dispatch_kernel(da_ref, db_ref, pend_ref, h_ref, xb_ref, zero_scr, sem, zsem, *, tm):
    i = pl.program_id(0)

    def row_copy(g, u, dst_row):
        return pltpu.make_async_copy(h_ref.at[g, pl.ds(u, 1)], xb_ref.at[pl.ds(dst_row, 1)], sem)

    @pl.when(i == 0)
    def _():
        zero_scr[...] = jnp.zeros_like(zero_scr)

        def zcopy(e):
            return pltpu.make_async_copy(
                zero_scr, xb_ref.at[pl.ds(pl.multiple_of(pend_ref[e] - MOE_BLK, MOE_BLK), MOE_BLK)], zsem)

        def nonempty(e):
            prev = jnp.where(e > 0, pend_ref[jnp.maximum(e - 1, 0)], 0)
            return pend_ref[e] > prev

        for e in range(N_EXPERTS):
            @pl.when(nonempty(e))
            def _():
                zcopy(e).start()
        for e in range(N_EXPERTS):
            @pl.when(nonempty(e))
            def _():
                zcopy(e).wait()

        def tail_copy(blk):
            return pltpu.make_async_copy(
                zero_scr, xb_ref.at[pl.ds(pl.multiple_of(blk * MOE_BLK, MOE_BLK), MOE_BLK)], zsem)

        first_unused = pend_ref[N_EXPERTS - 1] // MOE_BLK
        n_blocks = xb_ref.shape[0] // MOE_BLK
        lax.fori_loop(first_unused, n_blocks, lambda blk, c: (tail_copy(blk).start(), c)[1], 0)
        lax.fori_loop(first_unused, n_blocks, lambda blk, c: (tail_copy(blk).wait(), c)[1], 0)

    def issue(g, carry):
        t0 = i * tm + g * SUBLANES
        for u in range(SUBLANES):
            row_copy(g, u, da_ref[t0 + u]).start(priority=0)
            row_copy(g, u, db_ref[t0 + u]).start(priority=1)
        return carry

    lax.fori_loop(0, tm // SUBLANES, issue, 0)

    def drain(g, carry):
        for _ in range(2 * SUBLANES):
            row_copy(0, 0, 0).wait()
        return carry

    lax.fori_loop(0, tm // SUBLANES, drain, 0)


def _dispatch(dest_a, dest_b, pad_end, h2, n_rows, tm):
    t_all, d = h2.shape
    kern = functools.partial(_dispatch_kernel, tm=tm)
    return pl.pallas_call(
        kern,
        grid_spec=pltpu.PrefetchScalarGridSpec(
            num_scalar_prefetch=3,
            grid=(t_all // tm,),
            in_specs=[pl.BlockSpec((tm // SUBLANES, SUBLANES, d), lambda i, *_: (i, 0, 0))],
            out_specs=pl.BlockSpec(memory_space=pl.ANY),
            scratch_shapes=[pltpu.VMEM((MOE_BLK, d), h2.dtype),
                            pltpu.SemaphoreType.DMA(()), pltpu.SemaphoreType.DMA(())]),
        out_shape=jax.ShapeDtypeStruct((n_rows, d), h2.dtype),
        compiler_params=_cparams("arbitrary"),
        name="moe_dispatch",
    )(dest_a, dest_b, pad_end, h2.reshape(t_all // SUBLANES, SUBLANES, d))


def _expert_kernel(be_ref, nu_ref, x_ref, w1_ref, w3_ref, w2_ref, o_ref, w1_scr, w3_scr, w2_scr):
    i = pl.program_id(0)

    @pl.when(jnp.logical_or(i == 0, be_ref[i] != be_ref[jnp.maximum(i - 1, 0)]))
    def _():
        w1_scr[...] = w1_ref[0].astype(BF16)
        w3_scr[...] = w3_ref[0].astype(BF16)
        w2_scr[...] = w2_ref[0].astype(BF16)

    @pl.when(i < nu_ref[0])
    def _():
        x = _unpack_halves(x_ref[...]).astype(BF16)
        a = _dot(x, w1_scr[...])
        b = _dot(x, w3_scr[...])
        o_ref[...] = _pack_halves(_dot((a * _sigmoid(a) * b).astype(BF16), w2_scr[...]))

    @pl.when(i >= nu_ref[0])
    def _():
        o_ref[...] = jnp.zeros_like(o_ref)


def _experts(block_expert, n_used, xb, w1, w3, w2):
    n_rows, dp = xb.shape
    _, d, hid = w1.shape
    row_blk = lambda i, be, nu: (jnp.minimum(i, nu[0] - 1), 0)
    return pl.pallas_call(
        _expert_kernel,
        grid_spec=pltpu.PrefetchScalarGridSpec(
            num_scalar_prefetch=2,
            grid=(n_rows // MOE_BLK,),
            in_specs=[pl.BlockSpec((MOE_BLK, dp), row_blk),
                      pl.BlockSpec((1, d, hid), lambda i, be, nu: (be[i], 0, 0)),
                      pl.BlockSpec((1, d, hid), lambda i, be, nu: (be[i], 0, 0)),
                      pl.BlockSpec((1, hid, d), lambda i, be, nu: (be[i], 0, 0))],
            out_specs=pl.BlockSpec((MOE_BLK, dp), lambda i, be, nu: (i, 0)),
            scratch_shapes=[pltpu.VMEM((d, hid), BF16), pltpu.VMEM((d, hid), BF16),
                            pltpu.VMEM((hid, d), BF16)]),
        out_shape=jax.ShapeDtypeStruct((n_rows, dp), xb.dtype),
        compiler_params=_cparams("arbitrary"),
        name="moe_experts",
    )(block_expert, n_used, xb, w1, w3, w2)


def _combine_kernel(da_ref, db_ref, x1_ref, route_ref, gt2_ref, gf_ref, yb_ref, o_ref,
                    buf, sems, *, tm, tiles_per_batch):
    b = pl.program_id(0)
    i = pl.program_id(1)
    step = b * tiles_per_batch + i
    n_steps = pl.num_programs(0) * tiles_per_batch
    slot = step % 2

    def row_copy(src_row, s, which, g, u):
        return pltpu.make_async_copy(yb_ref.at[pl.ds(src_row, 1)], buf.at[s, which, g, pl.ds(u, 1)],
                                     sems.at[s])

    def issue_tile(tile, s):
        def body(g, carry):
            t0 = tile * tm + g * SUBLANES
            for u in range(SUBLANES):
                row_copy(da_ref[t0 + u], s, 0, g, u).start(priority=0)
                row_copy(db_ref[t0 + u], s, 1, g, u).start(priority=1)
            return carry
        lax.fori_loop(0, tm // SUBLANES, body, 0)

    @pl.when(step == 0)
    def _():
        issue_tile(0, 0)

    @pl.when(step + 1 < n_steps)
    def _():
        issue_tile(step + 1, 1 - slot)

    def drain(g, carry):
        for _ in range(SUBLANES):
            row_copy(0, slot, 0, 0, 0).wait()
            row_copy(0, slot, 1, 0, 0).wait()
        return carry

    lax.fori_loop(0, tm // SUBLANES, drain, 0)

    route = route_ref[0]
    dp = buf.shape[-1]
    y = route[:, 2:3] * _unpack_halves(buf[slot, 0].reshape(tm, dp)) + \
        route[:, 3:4] * _unpack_halves(buf[slot, 1].reshape(tm, dp))
    xo = x1_ref[0] + gt2_ref[0] * y
    o_ref[0] = xo * lax.rsqrt(jnp.mean(xo * xo, axis=-1, keepdims=True) + EPS) * gf_ref[...]


def _combine(dest_a, dest_b, x1, route, gt2, gf, yb, tm):
    bsz, n, d = x1.shape
    nt = n // tm
    kern = functools.partial(_combine_kernel, tm=tm, tiles_per_batch=nt)
    return pl.pallas_call(
        kern,
        grid_spec=pltpu.PrefetchScalarGridSpec(
            num_scalar_prefetch=2,
            grid=(bsz, nt),
            in_specs=[pl.BlockSpec((1, tm, d), lambda b, i, *_: (b, i, 0)),
                      pl.BlockSpec((1, tm, ROUTE_LANES), lambda b, i, *_: (b, i, 0)),
                      pl.BlockSpec((1, 1, d), lambda b, i, *_: (b, 0, 0)),
                      pl.BlockSpec((1, d), lambda b, i, *_: (0, 0)),
                      pl.BlockSpec(memory_space=pl.ANY)],
            out_specs=pl.BlockSpec((1, tm, d), lambda b, i, *_: (b, i, 0)),
            scratch_shapes=[pltpu.VMEM((2, 2, tm // SUBLANES, SUBLANES, yb.shape[1]), yb.dtype),
                            pltpu.SemaphoreType.DMA((2,))]),
        out_shape=jax.ShapeDtypeStruct((bsz, n, d), F32),
        compiler_params=_cparams("arbitrary", "arbitrary"),
        name="moe_combine",
    )(dest_a, dest_b, x1, route, gt2, gf, yb)


def _rope_tables(n):
    rows = n // GRID_W
    r, col = np.meshgrid(np.arange(rows, dtype=np.float64), np.arange(GRID_W, dtype=np.float64), indexing='ij')
    n_freq = RET_QK_DIM // 4
    inv_freq = ROPE_BASE ** (-np.arange(n_freq, dtype=np.float64) / n_freq)
    ang_r = r.reshape(-1)[:, None] * inv_freq
    ang_c = col.reshape(-1)[:, None] * inv_freq
    cos_t = np.concatenate([np.cos(ang_r), np.cos(ang_r), np.cos(ang_c), np.cos(ang_c)], axis=-1)
    sin_t = np.concatenate([-np.sin(ang_r), np.sin(ang_r), -np.sin(ang_c), np.sin(ang_c)], axis=-1)
    return jnp.asarray(cos_t, F32), jnp.asarray(sin_t, F32)


def _hyena_feats(n):
    t = np.arange(n, dtype=np.float64) / n
    bands = np.linspace(1e-4, HY_BANDS - 1, HY_BANDS)
    phase = 2.0 * math.pi * t[:, None] * bands[None, :]
    feats = np.concatenate([t[:, None], np.cos(phase), -np.sin(phase)], axis=-1)
    return jnp.asarray(np.pad(feats, ((0, 0), (0, FFN_LANES - HY_POS_DIM))), F32)


def _layer(x, ctx, mods, norm1_g, norm2_g, w_in, b_in, ret_decay_logit, ret_w_o, hy_conv_w,
           hy_conv_b, hy_ffn_w1, hy_ffn_b1, hy_ffn_freq, hy_ffn_w2, hy_ffn_b2, hy_ffn_w3, hy_skip,
           hy_w_o, w_out, router_group_w, router_group_b, router_expert_w, router_expert_b,
           expert_w1, expert_w3, expert_w2, final_norm_g):
    bsz, n, d = x.shape
    n_ctx = ctx.shape[1]
    mod_lat = mods[:bsz].reshape(bsz, 6, 1, d)
    sh1, sc1, gt1, sh2, sc2, gt2 = (mod_lat[:, s] for s in range(6))
    mod_ctx = mods[bsz].reshape(6, 1, 1, d)
    csh1 = jnp.broadcast_to(mod_ctx[0], (bsz, 1, d))
    csc1 = jnp.broadcast_to(mod_ctx[1], (bsz, 1, d))

    g1 = norm1_g.reshape(1, d)
    b_in2 = b_in.reshape(1, IN_W)
    w_in_b = w_in.astype(BF16)
    proj = _inproj(x, g1, sh1, sc1, w_in_b, b_in2, tm=min(512, n), tn=512)
    proj_c = _inproj(ctx, g1, csh1, csc1, w_in_b[:, COL_K:COL_G], b_in2[:, COL_K:COL_G],
                     tm=n_ctx, tn=512)

    cos_r, sin_r = _rope_tables(n)
    lgt = jnp.broadcast_to(ret_decay_logit.astype(F32).reshape(2 * RET_HEADS, 1),
                           (2 * RET_HEADS, RET_V_DIM))
    ret = _retention(proj, proj_c, cos_r, sin_r, lgt)

    slow = abs(math.log(HY_DECAY_TARGET)) / HY_SLOW_PCT
    fast = abs(math.log(HY_DECAY_TARGET)) / HY_FAST_PCT
    deltas = jnp.tile(jnp.linspace(slow, fast, HY_W, dtype=F32), 2).reshape(1, 2 * HY_W)
    fpad = FFN_LANES - HY_FFN
    row = lambda a: jnp.pad(a.reshape(1, HY_FFN), ((0, 0), (0, fpad)))
    fs, fd = _hyena_filters(n, _hyena_feats(n),
                            jnp.pad(hy_ffn_w1, ((0, FFN_LANES - HY_POS_DIM), (0, fpad))), row(hy_ffn_b1),
                            row(hy_ffn_freq), jnp.pad(hy_ffn_w2, ((0, fpad), (0, fpad))), row(hy_ffn_b2),
                            jnp.pad(hy_ffn_w3, ((0, fpad), (0, 0))), deltas)
    z, x0 = _hyena_pre(proj, hy_conv_w, hy_conv_b)
    fac_fwd = _dft_factors(n, inverse=False)
    fac_inv = _dft_factors(n, inverse=True)
    tile = min(512, n)
    spec_a, spec_b = _dft_filters(fac_fwd, fs, fd, tf=tile, tk=min(2048, n))
    tn = min(2048, bsz * HY_W)
    tk = min(1024, n)
    u, v = _dft_fwd(fac_fwd, z, spec_a, spec_b, tf=tile, tn=tn, tk=tk)
    skip_t = jnp.tile(hy_skip.reshape(1, HY_W), (1, bsz))
    hy = _dft_inv(fac_inv, u, v, z, x0, skip_t, tt=tile, tn=tn, tk=tk)

    rw = jnp.zeros((d, ROUTE_LANES), F32)
    rw = rw.at[:, :N_GROUPS].set(router_group_w).at[:, EXPERT_LANE0:EXPERT_LANE0 + N_EXPERTS].set(router_expert_w)
    rb = jnp.zeros((1, ROUTE_LANES), F32)
    rb = rb.at[0, :N_GROUPS].set(router_group_b).at[0, EXPERT_LANE0:EXPERT_LANE0 + N_EXPERTS].set(router_expert_b)
    rw_hi, rw_lo = _split_bf16(rw)
    sub = min(512, n)
    x1, h2, route, route_t, cnt = _merge(ret, hy, proj, x, gt1, sh2, sc2, norm2_g.reshape(1, d),
                                ret_w_o.astype(BF16), hy_w_o.astype(BF16), w_out.astype(BF16),
                                rw_hi, rw_lo, rb, tm=min(2 * sub, n), sub=sub)

    t_all = bsz * n
    counts = cnt[0, EXPERT_LANE0:EXPERT_LANE0 + N_EXPERTS].astype(jnp.int32)
    padded = (counts + MOE_BLK - 1) // MOE_BLK * MOE_BLK
    pad_end = jnp.cumsum(padded)
    pad_start = pad_end - padded
    dest = _dest_rows(pad_start.astype(jnp.int32), route_t)
    dest_a, dest_b = dest[0], dest[1]
    n_blocks = -(-(2 * t_all + N_EXPERTS * (MOE_BLK - 1)) // MOE_BLK)
    blk0 = jnp.arange(n_blocks, dtype=jnp.int32) * MOE_BLK
    block_expert = jnp.minimum(jnp.sum(blk0[:, None] >= pad_end[None, :], axis=1), N_EXPERTS - 1).astype(jnp.int32)
    n_used = (pad_end[-1:] // MOE_BLK).astype(jnp.int32)

    xb = _dispatch(dest_a, dest_b, pad_end.astype(jnp.int32), h2.reshape(t_all, h2.shape[-1]),
                   n_blocks * MOE_BLK, tm=min(256, n))
    yb = _experts(block_expert, n_used, xb, expert_w1, expert_w3, expert_w2)
    return _combine(dest_a, dest_b, x1, route, gt2, final_norm_g.reshape(1, d), yb, tm=min(256, n))


def kernel(x, c, ctx, c_ctx, ada_w, ada_b, norm1_g, norm2_g, w_in, b_in, ret_decay_logit, ret_w_o, hy_conv_w, hy_conv_b, hy_ffn_w1, hy_ffn_b1, hy_ffn_freq, hy_ffn_w2, hy_ffn_b2, hy_ffn_w3, hy_skip, hy_w_o, w_out, router_group_w, router_group_b, router_expert_w, router_expert_b, expert_w1, expert_w3, expert_w2, final_norm_g):
    depth = ada_w.shape[0]
    assert depth == 1, "single-layer problem: the context stream is only read by the retention states"
    bsz, d = c.shape
    rows = -(-(bsz + 1) // 8) * 8
    cc = jnp.zeros((rows, d), F32).at[:bsz].set(c).at[bsz].set(c_ctx)
    mods = _adaln(cc, ada_w[0], ada_b[0].reshape(1, -1))
    return _layer(x, ctx, mods, norm1_g[0], norm2_g[0], w_in[0], b_in[0], ret_decay_logit[0],
                  ret_w_o[0], hy_conv_w[0], hy_conv_b[0], hy_ffn_w1[0], hy_ffn_b1[0], hy_ffn_freq[0],
                  hy_ffn_w2[0], hy_ffn_b2[0], hy_ffn_w3[0], hy_skip[0], hy_w_o[0], w_out[0],
                  router_group_w[0], router_group_b[0], router_expert_w[0], router_expert_b[0],
                  expert_w1[0], expert_w3[0], expert_w2[0], final_norm_g)
```

```python
import functools
import math

import numpy as np

import jax
import jax.numpy as jnp
from jax import lax
from jax.experimental import pallas as pl
from jax.experimental.pallas import tpu as pltpu

F32 = jnp.float32
BF16 = jnp.bfloat16

D_MODEL = 1024
EPS = 1e-6
GRID_W = 64
ROPE_BASE = 10000.0

RET_HEADS = 4
RET_QK_DIM = 128
RET_V_DIM = 256
RET_CHUNK = 256
RET_UNROLL = 4
QK_W = RET_HEADS * RET_QK_DIM
V_W = RET_HEADS * RET_V_DIM

HY_W = 512
HY_POS_DIM = 33
HY_BANDS = (HY_POS_DIM - 1) // 2
HY_FFN = 64
FFN_LANES = 128
HY_DECAY_TARGET = 1e-2
HY_FAST_PCT = 0.3
HY_SLOW_PCT = 1.5
DFT_GROUP = 64

N_GROUPS = 4
EXPERTS_PER_GROUP = 8
N_EXPERTS = N_GROUPS * EXPERTS_PER_GROUP
EXPERT_HIDDEN = 512
ROUTE_LANES = 128
EXPERT_LANE0 = N_GROUPS
MOE_BLK = 512
SUBLANES = 8

IN_W = 2 * QK_W + 2 * V_W + 3 * HY_W + 2 * D_MODEL
COL_Q, COL_K, COL_V, COL_G = 0, QK_W, 2 * QK_W, 2 * QK_W + V_W
COL_HY = 2 * QK_W + 2 * V_W
COL_GR = COL_HY + 3 * HY_W
COL_GH = COL_GR + D_MODEL

VMEM_LIMIT = 56 * 1024 * 1024
NEG = -1e30


def _cparams(*sem):
    return pltpu.CompilerParams(dimension_semantics=sem, vmem_limit_bytes=VMEM_LIMIT)


def _sigmoid(x):
    return 1.0 / (1.0 + jnp.exp(-x))


def _dot(a, b):
    return jnp.dot(a, b, preferred_element_type=F32)


def _dot_t0(a, b):
    return lax.dot_general(a, b, (((0,), (0,)), ((), ())), preferred_element_type=F32)


def _dot_nt(a, b):
    return lax.dot_general(a, b, (((1,), (1,)), ((), ())), preferred_element_type=F32)


def _split_bf16(a):
    hi = a.astype(BF16)
    lo = (a - hi.astype(F32)).astype(BF16)
    return hi, lo


def _pack_halves(x):
    w = x.shape[1] // 2
    return pltpu.pack_elementwise([x[:, :w], x[:, w:]], packed_dtype=BF16)


def _unpack_halves(p):
    lo = pltpu.unpack_elementwise(p, index=0, packed_dtype=BF16, unpacked_dtype=F32)
    hi = pltpu.unpack_elementwise(p, index=1, packed_dtype=BF16, unpacked_dtype=F32)
    return jnp.concatenate([lo, hi], axis=1)


def _dot3(a, b):
    ah, al = _split_bf16(a)
    bh, bl = _split_bf16(b)
    return _dot(ah, bh) + _dot(al, bh) + _dot(ah, bl)


def _adaln_kernel(c_ref, w_ref, b_ref, o_ref):
    c = c_ref[...]
    o_ref[...] = _dot3(c * _sigmoid(c), w_ref[...]) + b_ref[...]


def _adaln(cc, w, b):
    rows, d = cc.shape
    n = w.shape[1]
    tn = 1536
    return pl.pallas_call(
        _adaln_kernel,
        grid=(n // tn,),
        in_specs=[pl.BlockSpec((rows, d), lambda j: (0, 0)),
                  pl.BlockSpec((d, tn), lambda j: (0, j)),
                  pl.BlockSpec((1, tn), lambda j: (0, j))],
        out_specs=pl.BlockSpec((rows, tn), lambda j: (0, j)),
        out_shape=jax.ShapeDtypeStruct((rows, n), F32),
        compiler_params=_cparams("arbitrary"),
        name="adaln",
    )(cc, w, b)


def _inproj_kernel(x_ref, g_ref, sh_ref, sc_ref, w_ref, b_ref, o_ref, *, tn):
    x = x_ref[0]
    y = x * lax.rsqrt(jnp.mean(x * x, axis=-1, keepdims=True) + EPS) * g_ref[...]
    h = (y * (1.0 + sc_ref[0]) + sh_ref[0]).astype(BF16)
    for j in range(w_ref.shape[1] // tn):
        cols = slice(j * tn, (j + 1) * tn)
        o_ref[0, :, cols] = (_dot(h, w_ref[:, cols]) + b_ref[:, cols]).astype(o_ref.dtype)


def _inproj(x, gain, shift, scale, w, b, tm, tn):
    bsz, n, d = x.shape
    nw = w.shape[1]
    return pl.pallas_call(
        functools.partial(_inproj_kernel, tn=tn),
        grid=(bsz, n // tm),
        in_specs=[pl.BlockSpec((1, tm, d), lambda bi, i: (bi, i, 0)),
                  pl.BlockSpec((1, d), lambda bi, i: (0, 0)),
                  pl.BlockSpec((1, 1, d), lambda bi, i: (bi, 0, 0)),
                  pl.BlockSpec((1, 1, d), lambda bi, i: (bi, 0, 0)),
                  pl.BlockSpec((d, nw), lambda bi, i: (0, 0)),
                  pl.BlockSpec((1, nw), lambda bi, i: (0, 0))],
        out_specs=pl.BlockSpec((1, tm, nw), lambda bi, i: (bi, i, 0)),
        out_shape=jax.ShapeDtypeStruct((bsz, n, nw), BF16),
        compiler_params=_cparams("arbitrary", "arbitrary"),
        name="inproj",
    )(x, gain, shift, scale, w, b)


def _log_sigmoid(x):
    return jnp.minimum(x, 0.0) - jnp.log(1.0 + jnp.exp(-jnp.abs(x)))


def _rope_partner(x):
    lane = lax.broadcasted_iota(jnp.int32, x.shape, 1)
    return jnp.where((lane % 64) < 32, pltpu.roll(x, 96, axis=1), pltpu.roll(x, 32, axis=1))


def _retention_kernel(q_ref, k_ref, v_ref, g_ref, kc_ref, vc_ref, cos_ref, sin_ref, lgt_ref,
                      o_ref, qr_scr, kr_scr, oacc_scr, sf_scr, sb_scr, *, n_tok, n_ctx):
    c_len = RET_CHUNK
    n_chunks = n_tok // c_len
    head = pl.program_id(1)
    k_scale = RET_QK_DIM ** -0.5

    lg_f = _log_sigmoid(lgt_ref[pl.ds(head, 1), :])
    lg_b = _log_sigmoid(lgt_ref[pl.ds(RET_HEADS + head, 1), :])
    lgf_k, lgb_k = lg_f[:, :RET_QK_DIM], lg_b[:, :RET_QK_DIM]

    ii = lax.broadcasted_iota(jnp.int32, (c_len, c_len), 0)
    jj = lax.broadcasted_iota(jnp.int32, (c_len, c_len), 1)
    dif = (ii - jj).astype(F32)
    decay_in = jnp.where(ii >= jj, jnp.exp(lg_f[:, :c_len] * jnp.maximum(dif, 0.0)), 0.0) + \
        jnp.where(jj > ii, jnp.exp(lg_b[:, :c_len] * jnp.maximum(-dif, 0.0)), 0.0)
    pos_k = lax.broadcasted_iota(jnp.int32, (c_len, RET_QK_DIM), 0).astype(F32)
    pos_v = lax.broadcasted_iota(jnp.int32, (c_len, RET_V_DIM), 0).astype(F32)
    dq_f = jnp.exp(lg_f * (pos_v + 1.0))
    dq_b = jnp.exp(lg_b * (c_len - pos_v))
    dk_f = jnp.exp(lgf_k * (c_len - 1.0 - pos_k))
    dk_b = jnp.exp(lgb_k * pos_k)
    dchunk_f = jnp.exp(lg_f * float(c_len))
    dchunk_b = jnp.exp(lg_b * float(c_len))

    pos_c = lax.broadcasted_iota(jnp.int32, (n_ctx, RET_QK_DIM), 0).astype(F32)
    kc = kc_ref[0].astype(F32) * k_scale
    vc = vc_ref[0]
    sf_scr[...] = _dot_t0((kc * jnp.exp(lgf_k * (n_ctx - 1.0 - pos_c))).astype(BF16), vc)
    sb_scr[...] = _dot_t0((kc * jnp.exp(lgb_k * pos_c)).astype(BF16), vc)

    def fwd(c, carry):
        r0 = pl.multiple_of(c * c_len, c_len)
        rows = pl.ds(r0, c_len)
        cos = cos_ref[rows, :]
        sin = sin_ref[rows, :]
        q = q_ref[0, rows, :].astype(F32)
        k = k_ref[0, rows, :].astype(F32)
        qr = q * cos + _rope_partner(q) * sin
        kr = (k * cos + _rope_partner(k) * sin) * k_scale
        qb = qr.astype(BF16)
        kb = kr.astype(BF16)
        qr_scr[rows, :] = qb
        kr_scr[rows, :] = kr
        v = v_ref[0, rows, :]
        scores = _dot_nt(qb, kb) * decay_in
        o = _dot(scores.astype(BF16), v) + _dot(qb, sf_scr[...].astype(BF16)) * dq_f
        oacc_scr[rows, :] = o
        sf_scr[...] = sf_scr[...] * dchunk_f + _dot_t0((kr * dk_f).astype(BF16), v)
        return carry

    lax.fori_loop(0, n_chunks, fwd, 0, unroll=RET_UNROLL)

    def bwd(t, carry):
        c = n_chunks - 1 - t
        r0 = pl.multiple_of(c * c_len, c_len)
        rows = pl.ds(r0, c_len)
        v = v_ref[0, rows, :]
        o = oacc_scr[rows, :] + _dot(qr_scr[rows, :], sb_scr[...].astype(BF16)) * dq_b
        o = o * lax.rsqrt(jnp.mean(o * o, axis=-1, keepdims=True) + EPS)
        g = g_ref[0, rows, :].astype(F32)
        o_ref[0, rows, :] = (g * _sigmoid(g) * o).astype(o_ref.dtype)
        sb_scr[...] = sb_scr[...] * dchunk_b + _dot_t0((kr_scr[rows, :] * dk_b).astype(BF16), v)
        return carry

    lax.fori_loop(0, n_chunks, bwd, 0, unroll=RET_UNROLL)


def _retention(proj, proj_c, cos_t, sin_t, lgt):
    bsz, n, _ = proj.shape
    n_ctx = proj_c.shape[1]
    kq, kv = RET_QK_DIM, RET_V_DIM
    assert n % RET_CHUNK == 0 and RET_CHUNK <= kv
    kern = functools.partial(_retention_kernel, n_tok=n, n_ctx=n_ctx)
    return pl.pallas_call(
        kern,
        grid=(bsz, RET_HEADS),
        in_specs=[pl.BlockSpec((1, n, kq), lambda b, h: (b, 0, COL_Q // kq + h)),
                  pl.BlockSpec((1, n, kq), lambda b, h: (b, 0, COL_K // kq + h)),
                  pl.BlockSpec((1, n, kv), lambda b, h: (b, 0, COL_V // kv + h)),
                  pl.BlockSpec((1, n, kv), lambda b, h: (b, 0, COL_G // kv + h)),
                  pl.BlockSpec((1, n_ctx, kq), lambda b, h: (b, 0, h)),
                  pl.BlockSpec((1, n_ctx, kv), lambda b, h: (b, 0, QK_W // kv + h)),
                  pl.BlockSpec((n, kq), lambda b, h: (0, 0)),
                  pl.BlockSpec((n, kq), lambda b, h: (0, 0)),
                  pl.BlockSpec((2 * RET_HEADS, kv), lambda b, h: (0, 0))],
        out_specs=pl.BlockSpec((1, n, kv), lambda b, h: (b, 0, h)),
        out_shape=jax.ShapeDtypeStruct((bsz, n, V_W), BF16),
        scratch_shapes=[pltpu.VMEM((n, kq), BF16), pltpu.VMEM((n, kq), F32),
                        pltpu.VMEM((n, kv), F32), pltpu.VMEM((kq, kv), F32),
                        pltpu.VMEM((kq, kv), F32)],
        compiler_params=_cparams("arbitrary", "arbitrary"),
        name="retention",
    )(proj, proj, proj, proj, proj_c, proj_c, cos_t, sin_t, lgt)


def _filter_kernel(feats_ref, w1_ref, b1_ref, fr_ref, w2_ref, b2_ref, w3f_ref, w3b_ref,
                   df_ref, db_ref, fs_ref, fd_ref, *, n_tok):
    fr = fr_ref[...]
    hid = jnp.sin(fr * (_dot3(feats_ref[...], w1_ref[...]) + b1_ref[...]))
    hid = jnp.sin(fr * (_dot3(hid, w2_ref[...]) + b2_ref[...]))
    t = lax.broadcasted_iota(jnp.int32, (n_tok, df_ref.shape[1]), 0).astype(F32) / n_tok

    def one(w3_ref, d_ref):
        f = _dot3(hid, w3_ref[...]) * jnp.exp(-t * d_ref[...])
        return f / jnp.sum(jnp.abs(f), axis=0, keepdims=True)

    hf = one(w3f_ref, df_ref)
    hb = one(w3b_ref, db_ref)
    fs_ref[...] = (hf + hb).astype(fs_ref.dtype)
    fd_ref[...] = (hf - hb).astype(fd_ref.dtype)


def _hyena_filters(n, feats, w1, b1, freq, w2, b2, w3, deltas):
    tc = HY_W
    nf = feats.shape[1]
    kern = functools.partial(_filter_kernel, n_tok=n)
    full = lambda shape: pl.BlockSpec(shape, lambda j: (0, 0))
    return pl.pallas_call(
        kern,
        grid=(HY_W // tc,),
        in_specs=[full((n, nf)), full((nf, FFN_LANES)), full((1, FFN_LANES)), full((1, FFN_LANES)),
                  full((FFN_LANES, FFN_LANES)), full((1, FFN_LANES)),
                  pl.BlockSpec((FFN_LANES, tc), lambda j: (0, j)),
                  pl.BlockSpec((FFN_LANES, tc), lambda j: (0, HY_W // tc + j)),
                  pl.BlockSpec((1, tc), lambda j: (0, j)),
                  pl.BlockSpec((1, tc), lambda j: (0, HY_W // tc + j))],
        out_specs=[pl.BlockSpec((n, tc), lambda j: (0, j)),
                   pl.BlockSpec((n, tc), lambda j: (0, j))],
        out_shape=[jax.ShapeDtypeStruct((n, HY_W), BF16)] * 2,
        compiler_params=_cparams("arbitrary"),
        name="hyena_filters",
    )(feats, w1, b1, freq, w2, b2, w3, w3, deltas, deltas)


def _hyena_pre_kernel(u0_ref, u1_ref, u2_ref, w_ref, b_ref, z_ref, x0_ref, *, n_tok, rows):
    n_steps = n_tok // rows
    tc = z_ref.shape[1]
    halo = 16
    rid = lax.broadcasted_iota(jnp.int32, (rows, tc), 0)

    def conv(u_ref, part, r0, has_prev, has_next):
        x = u_ref[0, pl.ds(r0, rows), :].astype(F32)
        prev_g = u_ref[0, pl.ds(pl.multiple_of(jnp.maximum(r0 - halo, 0), halo), halo), :].astype(F32)
        next_g = u_ref[0, pl.ds(pl.multiple_of(jnp.minimum(r0 + rows, n_tok - halo), halo), halo), :].astype(F32)
        prev_row = jnp.where(has_prev, prev_g[halo - 1:halo, :], 0.0)
        next_row = jnp.where(has_next, next_g[0:1, :], 0.0)
        up = jnp.where(rid == 0, prev_row, pltpu.roll(x, 1, axis=0))
        dn = jnp.where(rid == rows - 1, next_row, pltpu.roll(x, rows - 1, axis=0))
        w = w_ref[part]
        return up * w[0:1, :] + x * w[1:2, :] + dn * w[2:3, :] + b_ref[part]

    def body(s, carry):
        r0 = pl.multiple_of(s * rows, rows)
        has_prev = s > 0
        has_next = s < n_steps - 1
        x0 = conv(u0_ref, 0, r0, has_prev, has_next)
        x1 = conv(u1_ref, 1, r0, has_prev, has_next)
        vv = conv(u2_ref, 2, r0, has_prev, has_next)
        z_ref[pl.ds(r0, rows), :] = (x1 * vv).astype(z_ref.dtype)
        x0_ref[pl.ds(r0, rows), :] = x0.astype(x0_ref.dtype)
        return carry

    lax.fori_loop(0, n_steps, body, 0)


def _hyena_pre(proj, conv_w, conv_b):
    bsz, n, _ = proj.shape
    tc = 256
    nj = HY_W // tc
    rows = min(512, n)
    base = COL_HY // tc
    kern = functools.partial(_hyena_pre_kernel, n_tok=n, rows=rows)
    u_spec = lambda part: pl.BlockSpec((1, n, tc), lambda b, j: (b, 0, base + part * nj + j))
    wp = jnp.zeros((3, 8, HY_W), F32).at[:, :3, :].set(conv_w.reshape(3, 3, HY_W).transpose(1, 0, 2))
    bp = conv_b.reshape(3, 1, HY_W)
    return pl.pallas_call(
        kern,
        grid=(bsz, nj),
        in_specs=[u_spec(0), u_spec(1), u_spec(2),
                  pl.BlockSpec((3, 8, tc), lambda b, j: (0, 0, j)),
                  pl.BlockSpec((3, 1, tc), lambda b, j: (0, 0, j))],
        out_specs=[pl.BlockSpec((n, tc), lambda b, j: (0, b * nj + j)),
                   pl.BlockSpec((n, tc), lambda b, j: (0, b * nj + j))],
        out_shape=[jax.ShapeDtypeStruct((n, bsz * HY_W), BF16)] * 2,
        compiler_params=_cparams("arbitrary", "arbitrary"),
        name="hyena_pre",
    )(proj, proj, proj, wp, bp)


def _dft_factors(n, inverse):
    period = 4 * n
    col = np.arange(n, dtype=np.int64)[None, :]
    hi = np.arange(n // DFT_GROUP, dtype=np.int64)[:, None]
    lo = np.arange(DFT_GROUP, dtype=np.int64)[:, None]
    if inverse:
        m_a, m_b = (2 * col + 1) * (DFT_GROUP * hi), (2 * col + 1) * lo
    else:
        m_a, m_b = (2 * DFT_GROUP * hi) * col, (2 * lo + 1) * col
    ang_a = (m_a % period) * (2.0 * math.pi / period)
    ang_b = (m_b % period) * (2.0 * math.pi / period)
    return tuple(jnp.asarray(t, F32) for t in (np.cos(ang_a), np.sin(ang_a), np.cos(ang_b), np.sin(ang_b)))


def _dft_tile(ca_ref, sa_ref, cb_ref, sb_ref):
    cb = cb_ref[...]
    sb = sb_ref[...]
    cos_rows, sin_rows = [], []
    for g in range(ca_ref.shape[0]):
        ca = ca_ref[g:g + 1, :]
        sa = sa_ref[g:g + 1, :]
        cos_rows.append((ca * cb - sa * sb).astype(BF16))
        sin_rows.append((sa * cb + ca * sb).astype(BF16))
    return jnp.concatenate(cos_rows, axis=0), jnp.concatenate(sin_rows, axis=0)


def _factor_specs(tr, tc, row_axis, col_axis):
    a_spec = pl.BlockSpec((tr // DFT_GROUP, tc), lambda *ids: (ids[row_axis], ids[col_axis]))
    b_spec = pl.BlockSpec((DFT_GROUP, tc), lambda *ids: (0, ids[col_axis]))
    return [a_spec, a_spec, b_spec, b_spec]


def _dft_filter_kernel(ca_ref, sa_ref, cb_ref, sb_ref, fs_ref, fd_ref, a_ref, b_ref, acc_a, acc_b):
    kk = pl.program_id(1)

    @pl.when(kk == 0)
    def _():
        acc_a[...] = jnp.zeros_like(acc_a)
        acc_b[...] = jnp.zeros_like(acc_b)

    cos_t, sin_t = _dft_tile(ca_ref, sa_ref, cb_ref, sb_ref)
    acc_a[...] += _dot(cos_t, fs_ref[...])
    acc_b[...] += _dot(sin_t, fd_ref[...])

    @pl.when(kk == pl.num_programs(1) - 1)
    def _():
        a_ref[...] = acc_a[...]
        b_ref[...] = acc_b[...]


def _dft_filters(factors, fs, fd, tf, tk):
    n, w = fs.shape
    return pl.pallas_call(
        _dft_filter_kernel,
        grid=(n // tf, n // tk),
        in_specs=_factor_specs(tf, tk, 0, 1) + [
            pl.BlockSpec((tk, w), lambda i, kk: (kk, 0)),
            pl.BlockSpec((tk, w), lambda i, kk: (kk, 0))],
        out_specs=[pl.BlockSpec((tf, w), lambda i, kk: (i, 0)),
                   pl.BlockSpec((tf, w), lambda i, kk: (i, 0))],
        out_shape=[jax.ShapeDtypeStruct((n, w), F32)] * 2,
        scratch_shapes=[pltpu.VMEM((tf, w), F32), pltpu.VMEM((tf, w), F32)],
        compiler_params=_cparams("arbitrary", "arbitrary"),
        name="dft_filters",
    )(*factors, fs, fd)


def _dft_fwd_kernel(ca_ref, sa_ref, cb_ref, sb_ref, z_ref, a_ref, b_ref, u_ref, v_ref, acc_p, acc_q):
    kk = pl.program_id(2)

    @pl.when(kk == 0)
    def _():
        acc_p[...] = jnp.zeros_like(acc_p)
        acc_q[...] = jnp.zeros_like(acc_q)

    cos_t, sin_t = _dft_tile(ca_ref, sa_ref, cb_ref, sb_ref)
    z = z_ref[...]
    acc_p[...] += _dot(cos_t, z)
    acc_q[...] += _dot(sin_t, z)

    @pl.when(kk == pl.num_programs(2) - 1)
    def _():
        a = a_ref[...]
        b = b_ref[...]
        w = a.shape[1]
        for s in range(u_ref.shape[1] // w):
            cols = slice(s * w, (s + 1) * w)
            p = acc_p[:, cols]
            q = acc_q[:, cols]
            u_ref[:, cols] = (p * a - q * b).astype(u_ref.dtype)
            v_ref[:, cols] = (p * b + q * a).astype(v_ref.dtype)


def _dft_fwd(factors, z, spec_a, spec_b, tf, tn, tk):
    n, ncol = z.shape
    w = spec_a.shape[1]
    return pl.pallas_call(
        _dft_fwd_kernel,
        grid=(n // tf, ncol // tn, n // tk),
        in_specs=_factor_specs(tf, tk, 0, 2) + [
            pl.BlockSpec((tk, tn), lambda i, j, kk: (kk, j)),
            pl.BlockSpec((tf, w), lambda i, j, kk: (i, 0)),
            pl.BlockSpec((tf, w), lambda i, j, kk: (i, 0))],
        out_specs=[pl.BlockSpec((tf, tn), lambda i, j, kk: (i, j)),
                   pl.BlockSpec((tf, tn), lambda i, j, kk: (i, j))],
        out_shape=[jax.ShapeDtypeStruct((n, ncol), BF16)] * 2,
        scratch_shapes=[pltpu.VMEM((tf, tn), F32), pltpu.VMEM((tf, tn), F32)],
        compiler_params=_cparams("arbitrary", "arbitrary", "arbitrary"),
        name="dft_fwd",
    )(*factors, z, spec_a, spec_b)


def _dft_inv_kernel(ca_ref, sa_ref, cb_ref, sb_ref, u_ref, v_ref, z_ref, x0_ref, skip_ref, o_ref, acc,
                    *, inv_n):
    kk = pl.program_id(2)

    @pl.when(kk == 0)
    def _():
        acc[...] = jnp.zeros_like(acc)

    cos_t, sin_t = _dft_tile(ca_ref, sa_ref, cb_ref, sb_ref)
    acc[...] += _dot(cos_t, u_ref[...]) + _dot(sin_t, v_ref[...])

    @pl.when(kk == pl.num_programs(2) - 1)
    def _():
        y = acc[...] * inv_n + z_ref[...].astype(F32) * skip_ref[...]
        o_ref[...] = (x0_ref[...].astype(F32) * y).astype(o_ref.dtype)


def _dft_inv(factors, u, v, z, x0, skip_t, tt, tn, tk):
    n, ncol = u.shape
    kern = functools.partial(_dft_inv_kernel, inv_n=1.0 / n)
    return pl.pallas_call(
        kern,
        grid=(n // tt, ncol // tn, n // tk),
        in_specs=_factor_specs(tt, tk, 0, 2) + [
                  pl.BlockSpec((tk, tn), lambda i, j, kk: (kk, j)),
                  pl.BlockSpec((tk, tn), lambda i, j, kk: (kk, j)),
                  pl.BlockSpec((tt, tn), lambda i, j, kk: (i, j)),
                  pl.BlockSpec((tt, tn), lambda i, j, kk: (i, j)),
                  pl.BlockSpec((1, tn), lambda i, j, kk: (0, j))],
        out_specs=pl.BlockSpec((tt, tn), lambda i, j, kk: (i, j)),
        out_shape=jax.ShapeDtypeStruct((n, ncol), BF16),
        scratch_shapes=[pltpu.VMEM((tt, tn), F32)],
        compiler_params=_cparams("arbitrary", "arbitrary", "arbitrary"),
        name="dft_inv",
    )(*factors, u, v, z, x0, skip_t)


def _merge_kernel(ret_ref, hy_ref, gr0_ref, gr1_ref, gh0_ref, gh1_ref, x_ref, gt1_ref, sh2_ref,
                  sc2_ref, g2_ref, wro_ref, who_ref, wout_ref, rwh_ref, rwl_ref, rb_ref,
                  x1_ref, h2_ref, route_ref, routet_ref, cnt_ref, base_scr, *, sub):
    first = jnp.logical_and(pl.program_id(0) == 0, pl.program_id(1) == 0)

    @pl.when(first)
    def _():
        base_scr[...] = jnp.zeros_like(base_scr)

    for s0 in range(0, x_ref.shape[1], sub):
        _merge_rows(slice(s0, s0 + sub), ret_ref, hy_ref, gr0_ref, gr1_ref, gh0_ref, gh1_ref, x_ref,
                    gt1_ref, sh2_ref, sc2_ref, g2_ref, wro_ref, who_ref, wout_ref, rwh_ref, rwl_ref,
                    rb_ref, x1_ref, h2_ref, route_ref, routet_ref, base_scr)
    cnt_ref[...] = base_scr[...]


def _merge_rows(rows, ret_ref, hy_ref, gr0_ref, gr1_ref, gh0_ref, gh1_ref, x_ref, gt1_ref, sh2_ref,
                sc2_ref, g2_ref, wro_ref, who_ref, wout_ref, rwh_ref, rwl_ref, rb_ref,
                x1_ref, h2_ref, route_ref, routet_ref, base_scr):
    tm = rows.stop - rows.start
    gate_r = jnp.concatenate([gr0_ref[0, rows, :], gr1_ref[0, rows, :]], axis=1).astype(F32)
    gate_h = jnp.concatenate([gh0_ref[0, rows, :], gh1_ref[0, rows, :]], axis=1).astype(F32)
    mixed = _sigmoid(gate_r) * _dot(ret_ref[0, rows, :], wro_ref[...]) + \
        _sigmoid(gate_h) * _dot(hy_ref[rows, :], who_ref[...])
    x1 = x_ref[0, rows, :] + gt1_ref[0] * _dot(mixed.astype(BF16), wout_ref[...])
    x1_ref[0, rows, :] = x1
    h2 = x1 * lax.rsqrt(jnp.mean(x1 * x1, axis=-1, keepdims=True) + EPS) * g2_ref[...]
    h2 = h2 * (1.0 + sc2_ref[0]) + sh2_ref[0]
    h2_ref[0, rows, :] = _pack_halves(h2)

    h2_hi, h2_lo = _split_bf16(h2)
    rw_hi = rwh_ref[...]
    logits = _dot(h2_hi, rw_hi) + _dot(h2_lo, rw_hi) + _dot(h2_hi, rwl_ref[...]) + rb_ref[...]
    lane = lax.broadcasted_iota(jnp.int32, logits.shape, 1)
    lane_f = lane.astype(F32)
    big = float(ROUTE_LANES)

    def first_lane(mask):
        return jnp.min(jnp.where(mask, lane_f, big), axis=1, keepdims=True)

    is_group = lane < N_GROUPS
    gl = jnp.where(is_group, logits, NEG)
    ge = jnp.where(is_group, jnp.exp(gl - jnp.max(gl, axis=1, keepdims=True)), 0.0)
    group_p = ge / jnp.sum(ge, axis=1, keepdims=True)
    p_star = jnp.max(group_p, axis=1, keepdims=True)
    g_star = first_lane(jnp.logical_and(is_group, group_p == p_star))
    lo = EXPERT_LANE0 + g_star * EXPERTS_PER_GROUP
    in_group = jnp.logical_and(lane_f >= lo, lane_f < lo + EXPERTS_PER_GROUP)
    el = jnp.where(in_group, logits, NEG)
    ee = jnp.where(in_group, jnp.exp(el - jnp.max(el, axis=1, keepdims=True)), 0.0)
    sp = jnp.where(in_group, ee / jnp.sum(ee, axis=1, keepdims=True), -1.0)
    w_a = jnp.max(sp, axis=1, keepdims=True)
    l_a = first_lane(sp == w_a)
    sp2 = jnp.where(lane_f == l_a, -1.0, sp)
    w_b = jnp.max(sp2, axis=1, keepdims=True)
    l_b = first_lane(sp2 == w_b)
    wsum = w_a + w_b
    wt_a = p_star * w_a / wsum
    wt_b = p_star * w_b / wsum

    hit_a = lane_f == l_a
    hit_b = lane_f == l_b
    onehot = jnp.where(jnp.logical_or(hit_a, hit_b), 1.0, 0.0)
    ri = lax.broadcasted_iota(jnp.int32, (tm, tm), 0)
    ci = lax.broadcasted_iota(jnp.int32, (tm, tm), 1)
    tri = jnp.where(ri > ci, 1.0, 0.0).astype(BF16)
    before = _dot(tri, onehot.astype(BF16)) + base_scr[0:1, :]
    rank_a = jnp.sum(jnp.where(hit_a, before, 0.0), axis=1, keepdims=True)
    rank_b = jnp.sum(jnp.where(hit_b, before, 0.0), axis=1, keepdims=True)
    base_scr[0:1, :] = base_scr[0:1, :] + jnp.sum(onehot, axis=0, keepdims=True)

    vals = (l_a - EXPERT_LANE0, l_b - EXPERT_LANE0, wt_a, wt_b, rank_a, rank_b)
    route = jnp.zeros(logits.shape, F32)
    for idx, val in enumerate(vals):
        route = jnp.where(lane == idx, val, route)
    route_ref[0, rows, :] = route
    routet_ref[:, rows] = route.T[:SUBLANES, :]


def _merge(ret, hy, proj, x, gt1, sh2, sc2, g2, w_ro, w_ho, w_out, rw_hi, rw_lo, rb, tm, sub):
    bsz, n, d = x.shape
    nt = n // tm
    hw = HY_W
    gspec = lambda col: pl.BlockSpec((1, tm, hw), lambda b, i: (b, i, col // hw))
    vec = pl.BlockSpec((1, 1, d), lambda b, i: (b, 0, 0))
    full = lambda shape: pl.BlockSpec(shape, lambda b, i: (0, 0))
    tok = pl.BlockSpec((1, tm, d), lambda b, i: (b, i, 0))
    return pl.pallas_call(
        functools.partial(_merge_kernel, sub=sub),
        grid=(bsz, nt),
        in_specs=[tok,
                  pl.BlockSpec((tm, hw), lambda b, i: (i, b)),
                  gspec(COL_GR), gspec(COL_GR + hw), gspec(COL_GH), gspec(COL_GH + hw),
                  tok, vec, vec, vec, full((1, d)),
                  full((V_W, d)), full((hw, d)), full((d, d)),
                  full((d, ROUTE_LANES)), full((d, ROUTE_LANES)), full((1, ROUTE_LANES))],
        out_specs=[tok, pl.BlockSpec((1, tm, d // 2), lambda b, i: (b, i, 0)),
                   pl.BlockSpec((1, tm, ROUTE_LANES), lambda b, i: (b, i, 0)),
                   pl.BlockSpec((SUBLANES, tm), lambda b, i: (0, b * nt + i)),
                   pl.BlockSpec((8, ROUTE_LANES), lambda b, i: (0, 0))],
        out_shape=[jax.ShapeDtypeStruct((bsz, n, d), F32),
                   jax.ShapeDtypeStruct((bsz, n, d // 2), jnp.int32),
                   jax.ShapeDtypeStruct((bsz, n, ROUTE_LANES), F32),
                   jax.ShapeDtypeStruct((SUBLANES, bsz * n), F32),
                   jax.ShapeDtypeStruct((8, ROUTE_LANES), F32)],
        scratch_shapes=[pltpu.VMEM((8, ROUTE_LANES), F32)],
        compiler_params=_cparams("arbitrary", "arbitrary"),
        name="merge_route",
    )(ret, hy, proj, proj, proj, proj, x, gt1, sh2, sc2, g2, w_ro, w_ho, w_out, rw_hi, rw_lo, rb)


def _dest_kernel(ps_ref, rt_ref, o_ref):
    x = rt_ref[...]
    start = jnp.zeros(x.shape, jnp.int32)
    for e in range(N_EXPERTS):
        start = jnp.where(x == float(e), ps_ref[e], start)
    o_ref[...] = start + pltpu.roll(x.astype(jnp.int32), SUBLANES // 2, axis=0)


def _dest_rows(pad_start, route_t):
    rows, t_all = route_t.shape
    return pl.pallas_call(
        _dest_kernel,
        grid_spec=pltpu.PrefetchScalarGridSpec(
            num_scalar_prefetch=1,
            grid=(1,),
            in_specs=[pl.BlockSpec((rows, t_all), lambda i, ps: (0, 0))],
            out_specs=pl.BlockSpec((rows, t_all), lambda i, ps: (0, 0))),
        out_shape=jax.ShapeDtypeStruct((rows, t_all), jnp.int32),
        compiler_params=_cparams("arbitrary"),
        name="moe_dest",
    )(pad_start, route_t)


def _dispatch_kernel(da_ref, db_ref, pend_ref, h_ref, xb_ref, zero_scr, sem, zsem, *, tm):
    i = pl.program_id(0)

    def row_copy(g, u, dst_row):
        return pltpu.make_async_copy(h_ref.at[g, pl.ds(u, 1)], xb_ref.at[pl.ds(dst_row, 1)], sem)

    @pl.when(i == 0)
    def _():
        zero_scr[...] = jnp.zeros_like(zero_scr)

        def zcopy(e):
            return pltpu.make_async_copy(
                zero_scr, xb_ref.at[pl.ds(pl.multiple_of(pend_ref[e] - MOE_BLK, MOE_BLK), MOE_BLK)], zsem)

        def nonempty(e):
            prev = jnp.where(e > 0, pend_ref[jnp.maximum(e - 1, 0)], 0)
            return pend_ref[e] > prev

        for e in range(N_EXPERTS):
            @pl.when(nonempty(e))
            def _():
                zcopy(e).start()
        for e in range(N_EXPERTS):
            @pl.when(nonempty(e))
            def _():
                zcopy(e).wait()

        def tail_copy(blk):
            return pltpu.make_async_copy(
                zero_scr, xb_ref.at[pl.ds(pl.multiple_of(blk * MOE_BLK, MOE_BLK), MOE_BLK)], zsem)

        first_unused = pend_ref[N_EXPERTS - 1] // MOE_BLK
        n_blocks = xb_ref.shape[0] // MOE_BLK
        lax.fori_loop(first_unused, n_blocks, lambda blk, c: (tail_copy(blk).start(), c)[1], 0)
        lax.fori_loop(first_unused, n_blocks, lambda blk, c: (tail_copy(blk).wait(), c)[1], 0)

    def issue(g, carry):
        t0 = i * tm + g * SUBLANES
        for u in range(SUBLANES):
            row_copy(g, u, da_ref[t0 + u]).start(priority=0)
            row_copy(g, u, db_ref[t0 + u]).start(priority=1)
        return carry

    lax.fori_loop(0, tm // SUBLANES, issue, 0)

    def drain(g, carry):
        for _ in range(2 * SUBLANES):
            row_copy(0, 0, 0).wait()
        return carry

    lax.fori_loop(0, tm // SUBLANES, drain, 0)


def _dispatch(dest_a, dest_b, pad_end, h2, n_rows, tm):
    t_all, d = h2.shape
    kern = functools.partial(_dispatch_kernel, tm=tm)
    return pl.pallas_call(
        kern,
        grid_spec=pltpu.PrefetchScalarGridSpec(
            num_scalar_prefetch=3,
            grid=(t_all // tm,),
            in_specs=[pl.BlockSpec((tm // SUBLANES, SUBLANES, d), lambda i, *_: (i, 0, 0))],
            out_specs=pl.BlockSpec(memory_space=pl.ANY),
            scratch_shapes=[pltpu.VMEM((MOE_BLK, d), h2.dtype),
                            pltpu.SemaphoreType.DMA(()), pltpu.SemaphoreType.DMA(())]),
        out_shape=jax.ShapeDtypeStruct((n_rows, d), h2.dtype),
        compiler_params=_cparams("arbitrary"),
        name="moe_dispatch",
    )(dest_a, dest_b, pad_end, h2.reshape(t_all // SUBLANES, SUBLANES, d))


def _expert_kernel(be_ref, nu_ref, x_ref, w1_ref, w3_ref, w2_ref, o_ref, w1_scr, w3_scr, w2_scr):
    i = pl.program_id(0)

    @pl.when(jnp.logical_or(i == 0, be_ref[i] != be_ref[jnp.maximum(i - 1, 0)]))
    def _():
        w1_scr[...] = w1_ref[0].astype(BF16)
        w3_scr[...] = w3_ref[0].astype(BF16)
        w2_scr[...] = w2_ref[0].astype(BF16)

    @pl.when(i < nu_ref[0])
    def _():
        x = _unpack_halves(x_ref[...]).astype(BF16)
        a = _dot(x, w1_scr[...])
        b = _dot(x, w3_scr[...])
        o_ref[...] = _pack_halves(_dot((a * _sigmoid(a) * b).astype(BF16), w2_scr[...]))

    @pl.when(i >= nu_ref[0])
    def _():
        o_ref[...] = jnp.zeros_like(o_ref)


def _experts(block_expert, n_used, xb, w1, w3, w2):
    n_rows, dp = xb.shape
    _, d, hid = w1.shape
    row_blk = lambda i, be, nu: (jnp.minimum(i, nu[0] - 1), 0)
    return pl.pallas_call(
        _expert_kernel,
        grid_spec=pltpu.PrefetchScalarGridSpec(
            num_scalar_prefetch=2,
            grid=(n_rows // MOE_BLK,),
            in_specs=[pl.BlockSpec((MOE_BLK, dp), row_blk),
                      pl.BlockSpec((1, d, hid), lambda i, be, nu: (be[i], 0, 0)),
                      pl.BlockSpec((1, d, hid), lambda i, be, nu: (be[i], 0, 0)),
                      pl.BlockSpec((1, hid, d), lambda i, be, nu: (be[i], 0, 0))],
            out_specs=pl.BlockSpec((MOE_BLK, dp), lambda i, be, nu: (i, 0)),
            scratch_shapes=[pltpu.VMEM((d, hid), BF16), pltpu.VMEM((d, hid), BF16),
                            pltpu.VMEM((hid, d), BF16)]),
        out_shape=jax.ShapeDtypeStruct((n_rows, dp), xb.dtype),
        compiler_params=_cparams("arbitrary"),
        name="moe_experts",
    )(block_expert, n_used, xb, w1, w3, w2)


def _combine_kernel(da_ref, db_ref, x1_ref, route_ref, gt2_ref, gf_ref, yb_ref, o_ref,
                    buf, sems, *, tm, tiles_per_batch):
    b = pl.program_id(0)
    i = pl.program_id(1)
    step = b * tiles_per_batch + i
    n_steps = pl.num_programs(0) * tiles_per_batch
    slot = step % 2

    def row_copy(src_row, s, which, g, u):
        return pltpu.make_async_copy(yb_ref.at[pl.ds(src_row, 1)], buf.at[s, which, g, pl.ds(u, 1)],
                                     sems.at[s])

    def issue_tile(tile, s):
        def body(g, carry):
            t0 = tile * tm + g * SUBLANES
            for u in range(SUBLANES):
                row_copy(da_ref[t0 + u], s, 0, g, u).start(priority=0)
                row_copy(db_ref[t0 + u], s, 1, g, u).start(priority=1)
            return carry
        lax.fori_loop(0, tm // SUBLANES, body, 0)

    @pl.when(step == 0)
    def _():
        issue_tile(0, 0)

    @pl.when(step + 1 < n_steps)
    def _():
        issue_tile(step + 1, 1 - slot)

    def drain(g, carry):
        for _ in range(SUBLANES):
            row_copy(0, slot, 0, 0, 0).wait()
            row_copy(0, slot, 1, 0, 0).wait()
        return carry

    lax.fori_loop(0, tm // SUBLANES, drain, 0)

    route = route_ref[0]
    dp = buf.shape[-1]
    y = route[:, 2:3] * _unpack_halves(buf[slot, 0].reshape(tm, dp)) + \
        route[:, 3:4] * _unpack_halves(buf[slot, 1].reshape(tm, dp))
    xo = x1_ref[0] + gt2_ref[0] * y
    o_ref[0] = xo * lax.rsqrt(jnp.mean(xo * xo, axis=-1, keepdims=True) + EPS) * gf_ref[...]


def _combine(dest_a, dest_b, x1, route, gt2, gf, yb, tm):
    bsz, n, d = x1.shape
    nt = n // tm
    kern = functools.partial(_combine_kernel, tm=tm, tiles_per_batch=nt)
    return pl.pallas_call(
        kern,
        grid_spec=pltpu.PrefetchScalarGridSpec(
            num_scalar_prefetch=2,
            grid=(bsz, nt),
            in_specs=[pl.BlockSpec((1, tm, d), lambda b, i, *_: (b, i, 0)),
                      pl.BlockSpec((1, tm, ROUTE_LANES), lambda b, i, *_: (b, i, 0)),
                      pl.BlockSpec((1, 1, d), lambda b, i, *_: (b, 0, 0)),
                      pl.BlockSpec((1, d), lambda b, i, *_: (0, 0)),
                      pl.BlockSpec(memory_space=pl.ANY)],
            out_specs=pl.BlockSpec((1, tm, d), lambda b, i, *_: (b, i, 0)),
            scratch_shapes=[pltpu.VMEM((2, 2, tm // SUBLANES, SUBLANES, yb.shape[1]), yb.dtype),
                            pltpu.SemaphoreType.DMA((2,))]),
        out_shape=jax.ShapeDtypeStruct((bsz, n, d), F32),
        compiler_params=_cparams("arbitrary", "arbitrary"),
        name="moe_combine",
    )(dest_a, dest_b, x1, route, gt2, gf, yb)


def _rope_tables(n):
    rows = n // GRID_W
    r, col = np.meshgrid(np.arange(rows, dtype=np.float64), np.arange(GRID_W, dtype=np.float64), indexing='ij')
    n_freq = RET_QK_DIM // 4
    inv_freq = ROPE_BASE ** (-np.arange(n_freq, dtype=np.float64) / n_freq)
    ang_r = r.reshape(-1)[:, None] * inv_freq
    ang_c = col.reshape(-1)[:, None] * inv_freq
    cos_t = np.concatenate([np.cos(ang_r), np.cos(ang_r), np.cos(ang_c), np.cos(ang_c)], axis=-1)
    sin_t = np.concatenate([-np.sin(ang_r), np.sin(ang_r), -np.sin(ang_c), np.sin(ang_c)], axis=-1)
    return jnp.asarray(cos_t, F32), jnp.asarray(sin_t, F32)


def _hyena_feats(n):
    t = np.arange(n, dtype=np.float64) / n
    bands = np.linspace(1e-4, HY_BANDS - 1, HY_BANDS)
    phase = 2.0 * math.pi * t[:, None] * bands[None, :]
    feats = np.concatenate([t[:, None], np.cos(phase), -np.sin(phase)], axis=-1)
    return jnp.asarray(np.pad(feats, ((0, 0), (0, FFN_LANES - HY_POS_DIM))), F32)


def _layer(x, ctx, mods, norm1_g, norm2_g, w_in, b_in, ret_decay_logit, ret_w_o, hy_conv_w,
           hy_conv_b, hy_ffn_w1, hy_ffn_b1, hy_ffn_freq, hy_ffn_w2, hy_ffn_b2, hy_ffn_w3, hy_skip,
           hy_w_o, w_out, router_group_w, router_group_b, router_expert_w, router_expert_b,
           expert_w1, expert_w3, expert_w2, final_norm_g):
    bsz, n, d = x.shape
    n_ctx = ctx.shape[1]
    mod_lat = mods[:bsz].reshape(bsz, 6, 1, d)
    sh1, sc1, gt1, sh2, sc2, gt2 = (mod_lat[:, s] for s in range(6))
    mod_ctx = mods[bsz].reshape(6, 1, 1, d)
    csh1 = jnp.broadcast_to(mod_ctx[0], (bsz, 1, d))
    csc1 = jnp.broadcast_to(mod_ctx[1], (bsz, 1, d))

    g1 = norm1_g.reshape(1, d)
    b_in2 = b_in.reshape(1, IN_W)
    w_in_b = w_in.astype(BF16)
    proj = _inproj(x, g1, sh1, sc1, w_in_b, b_in2, tm=min(512, n), tn=512)
    proj_c = _inproj(ctx, g1, csh1, csc1, w_in_b[:, COL_K:COL_G], b_in2[:, COL_K:COL_G],
                     tm=n_ctx, tn=512)

    cos_r, sin_r = _rope_tables(n)
    lgt = jnp.broadcast_to(ret_decay_logit.astype(F32).reshape(2 * RET_HEADS, 1),
                           (2 * RET_HEADS, RET_V_DIM))
    ret = _retention(proj, proj_c, cos_r, sin_r, lgt)

    slow = abs(math.log(HY_DECAY_TARGET)) / HY_SLOW_PCT
    fast = abs(math.log(HY_DECAY_TARGET)) / HY_FAST_PCT
    deltas = jnp.tile(jnp.linspace(slow, fast, HY_W, dtype=F32), 2).reshape(1, 2 * HY_W)
    fpad = FFN_LANES - HY_FFN
    row = lambda a: jnp.pad(a.reshape(1, HY_FFN), ((0, 0), (0, fpad)))
    fs, fd = _hyena_filters(n, _hyena_feats(n),
                            jnp.pad(hy_ffn_w1, ((0, FFN_LANES - HY_POS_DIM), (0, fpad))), row(hy_ffn_b1),
                            row(hy_ffn_freq), jnp.pad(hy_ffn_w2, ((0, fpad), (0, fpad))), row(hy_ffn_b2),
                            jnp.pad(hy_ffn_w3, ((0, fpad), (0, 0))), deltas)
    z, x0 = _hyena_pre(proj, hy_conv_w, hy_conv_b)
    fac_fwd = _dft_factors(n, inverse=False)
    fac_inv = _dft_factors(n, inverse=True)
    tile = min(512, n)
    spec_a, spec_b = _dft_filters(fac_fwd, fs, fd, tf=tile, tk=min(2048, n))
    tn = min(2048, bsz * HY_W)
    tk = min(1024, n)
    u, v = _dft_fwd(fac_fwd, z, spec_a, spec_b, tf=tile, tn=tn, tk=tk)
    skip_t = jnp.tile(hy_skip.reshape(1, HY_W), (1, bsz))
    hy = _dft_inv(fac_inv, u, v, z, x0, skip_t, tt=tile, tn=tn, tk=tk)

    rw = jnp.zeros((d, ROUTE_LANES), F32)
    rw = rw.at[:, :N_GROUPS].set(router_group_w).at[:, EXPERT_LANE0:EXPERT_LANE0 + N_EXPERTS].set(router_expert_w)
    rb = jnp.zeros((1, ROUTE_LANES), F32)
    rb = rb.at[0, :N_GROUPS].set(router_group_b).at[0, EXPERT_LANE0:EXPERT_LANE0 + N_EXPERTS].set(router_expert_b)
    rw_hi, rw_lo = _split_bf16(rw)
    sub = min(512, n)
    x1, h2, route, route_t, cnt = _merge(ret, hy, proj, x, gt1, sh2, sc2, norm2_g.reshape(1, d),
                                ret_w_o.astype(BF16), hy_w_o.astype(BF16), w_out.astype(BF16),
                                rw_hi, rw_lo, rb, tm=min(2 * sub, n), sub=sub)

    t_all = bsz * n
    counts = cnt[0, EXPERT_LANE0:EXPERT_LANE0 + N_EXPERTS].astype(jnp.int32)
    padded = (counts + MOE_BLK - 1) // MOE_BLK * MOE_BLK
    pad_end = jnp.cumsum(padded)
    pad_start = pad_end - padded
    dest = _dest_rows(pad_start.astype(jnp.int32), route_t)
    dest_a, dest_b = dest[0], dest[1]
    n_blocks = -(-(2 * t_all + N_EXPERTS * (MOE_BLK - 1)) // MOE_BLK)
    blk0 = jnp.arange(n_blocks, dtype=jnp.int32) * MOE_BLK
    block_expert = jnp.minimum(jnp.sum(blk0[:, None] >= pad_end[None, :], axis=1), N_EXPERTS - 1).astype(jnp.int32)
    n_used = (pad_end[-1:] // MOE_BLK).astype(jnp.int32)

    xb = _dispatch(dest_a, dest_b, pad_end.astype(jnp.int32), h2.reshape(t_all, h2.shape[-1]),
                   n_blocks * MOE_BLK, tm=min(512, n))
    yb = _experts(block_expert, n_used, xb, expert_w1, expert_w3, expert_w2)
    return _combine(dest_a, dest_b, x1, route, gt2, final_norm_g.reshape(1, d), yb, tm=min(512, n))


def kernel(x, c, ctx, c_ctx, ada_w, ada_b, norm1_g, norm2_g, w_in, b_in, ret_decay_logit, ret_w_o, hy_conv_w, hy_conv_b, hy_ffn_w1, hy_ffn_b1, hy_ffn_freq, hy_ffn_w2, hy_ffn_b2, hy_ffn_w3, hy_skip, hy_w_o, w_out, router_group_w, router_group_b, router_expert_w, router_expert_b, expert_w1, expert_w3, expert_w2, final_norm_g):
    depth = ada_w.shape[0]
    assert depth == 1, "single-layer problem: the context stream is only read by the retention states"
    bsz, d = c.shape
    rows = -(-(bsz + 1) // 8) * 8
    cc = jnp.zeros((rows, d), F32).at[:bsz].set(c).at[bsz].set(c_ctx)
    mods = _adaln(cc, ada_w[0], ada_b[0].reshape(1, -1))
    return _layer(x, ctx, mods, norm1_g[0], norm2_g[0], w_in[0], b_in[0], ret_decay_logit[0],
                  ret_w_o[0], hy_conv_w[0], hy_conv_b[0], hy_ffn_w1[0], hy_ffn_b1[0], hy_ffn_freq[0],
                  hy_ffn_w2[0], hy_ffn_b2[0], hy_ffn_w3[0], hy_skip[0], hy_w_o[0], w_out[0],
                  router_group_w[0], router_group_b[0], router_expert_w[0], router_expert_b[0],
                  expert_w1[0], expert_w3[0], expert_w2[0], final_norm_g)
```

```python
import functools
import math

import numpy as np

import jax
import jax.numpy as jnp
from jax import lax
from jax.experimental import pallas as pl
from jax.experimental.pallas import tpu as pltpu

F32 = jnp.float32
BF16 = jnp.bfloat16

D_MODEL = 1024
EPS = 1e-6
GRID_W = 64
ROPE_BASE = 10000.0

RET_HEADS = 4
RET_QK_DIM = 128
RET_V_DIM = 256
RET_CHUNK = 256
RET_UNROLL = 4
QK_W = RET_HEADS * RET_QK_DIM
V_W = RET_HEADS * RET_V_DIM

HY_W = 512
HY_POS_DIM = 33
HY_BANDS = (HY_POS_DIM - 1) // 2
HY_FFN = 64
FFN_LANES = 128
HY_DECAY_TARGET = 1e-2
HY_FAST_PCT = 0.3
HY_SLOW_PCT = 1.5
DFT_GROUP = 64

N_GROUPS = 4
EXPERTS_PER_GROUP = 8
N_EXPERTS = N_GROUPS * EXPERTS_PER_GROUP
EXPERT_HIDDEN = 512
ROUTE_LANES = 128
EXPERT_LANE0 = N_GROUPS
MOE_BLK = 512
SUBLANES = 8

IN_W = 2 * QK_W + 2 * V_W + 3 * HY_W + 2 * D_MODEL
COL_Q, COL_K, COL_V, COL_G = 0, QK_W, 2 * QK_W, 2 * QK_W + V_W
COL_HY = 2 * QK_W + 2 * V_W
COL_GR = COL_HY + 3 * HY_W
COL_GH = COL_GR + D_MODEL

VMEM_LIMIT = 56 * 1024 * 1024
NEG = -1e30


def _cparams(*sem):
    return pltpu.CompilerParams(dimension_semantics=sem, vmem_limit_bytes=VMEM_LIMIT)


def _sigmoid(x):
    return 1.0 / (1.0 + jnp.exp(-x))


def _dot(a, b):
    return jnp.dot(a, b, preferred_element_type=F32)


def _dot_t0(a, b):
    return lax.dot_general(a, b, (((0,), (0,)), ((), ())), preferred_element_type=F32)


def _dot_nt(a, b):
    return lax.dot_general(a, b, (((1,), (1,)), ((), ())), preferred_element_type=F32)


def _split_bf16(a):
    hi = a.astype(BF16)
    lo = (a - hi.astype(F32)).astype(BF16)
    return hi, lo


def _pack_halves(x):
    w = x.shape[1] // 2
    return pltpu.pack_elementwise([x[:, :w], x[:, w:]], packed_dtype=BF16)


def _unpack_halves(p):
    lo = pltpu.unpack_elementwise(p, index=0, packed_dtype=BF16, unpacked_dtype=F32)
    hi = pltpu.unpack_elementwise(p, index=1, packed_dtype=BF16, unpacked_dtype=F32)
    return jnp.concatenate([lo, hi], axis=1)


def _dot3(a, b):
    ah, al = _split_bf16(a)
    bh, bl = _split_bf16(b)
    return _dot(ah, bh) + _dot(al, bh) + _dot(ah, bl)


def _adaln_kernel(c_ref, w_ref, b_ref, o_ref):
    c = c_ref[...]
    o_ref[...] = _dot3(c * _sigmoid(c), w_ref[...]) + b_ref[...]


def _adaln(cc, w, b):
    rows, d = cc.shape
    n = w.shape[1]
    tn = 1536
    return pl.pallas_call(
        _adaln_kernel,
        grid=(n // tn,),
        in_specs=[pl.BlockSpec((rows, d), lambda j: (0, 0)),
                  pl.BlockSpec((d, tn), lambda j: (0, j)),
                  pl.BlockSpec((1, tn), lambda j: (0, j))],
        out_specs=pl.BlockSpec((rows, tn), lambda j: (0, j)),
        out_shape=jax.ShapeDtypeStruct((rows, n), F32),
        compiler_params=_cparams("arbitrary"),
        name="adaln",
    )(cc, w, b)


def _inproj_kernel(x_ref, g_ref, sh_ref, sc_ref, w_ref, b_ref, o_ref, *, tn):
    x = x_ref[0]
    y = x * lax.rsqrt(jnp.mean(x * x, axis=-1, keepdims=True) + EPS) * g_ref[...]
    h = (y * (1.0 + sc_ref[0]) + sh_ref[0]).astype(BF16)
    for j in range(w_ref.shape[1] // tn):
        cols = slice(j * tn, (j + 1) * tn)
        o_ref[0, :, cols] = (_dot(h, w_ref[:, cols]) + b_ref[:, cols]).astype(o_ref.dtype)


def _inproj(x, gain, shift, scale, w, b, tm, tn):
    bsz, n, d = x.shape
    nw = w.shape[1]
    return pl.pallas_call(
        functools.partial(_inproj_kernel, tn=tn),
        grid=(bsz, n // tm),
        in_specs=[pl.BlockSpec((1, tm, d), lambda bi, i: (bi, i, 0)),
                  pl.BlockSpec((1, d), lambda bi, i: (0, 0)),
                  pl.BlockSpec((1, 1, d), lambda bi, i: (bi, 0, 0)),
                  pl.BlockSpec((1, 1, d), lambda bi, i: (bi, 0, 0)),
                  pl.BlockSpec((d, nw), lambda bi, i: (0, 0)),
                  pl.BlockSpec((1, nw), lambda bi, i: (0, 0))],
        out_specs=pl.BlockSpec((1, tm, nw), lambda bi, i: (bi, i, 0)),
        out_shape=jax.ShapeDtypeStruct((bsz, n, nw), BF16),
        compiler_params=_cparams("arbitrary", "arbitrary"),
        name="inproj",
    )(x, gain, shift, scale, w, b)


def _log_sigmoid(x):
    return jnp.minimum(x, 0.0) - jnp.log(1.0 + jnp.exp(-jnp.abs(x)))


def _rope_partner(x):
    lane = lax.broadcasted_iota(jnp.int32, x.shape, 1)
    return jnp.where((lane % 64) < 32, pltpu.roll(x, 96, axis=1), pltpu.roll(x, 32, axis=1))


def _retention_kernel(q_ref, k_ref, v_ref, g_ref, kc_ref, vc_ref, cos_ref, sin_ref, lgt_ref,
                      o_ref, qr_scr, kr_scr, oacc_scr, sf_scr, sb_scr, *, n_tok, n_ctx):
    c_len = RET_CHUNK
    n_chunks = n_tok // c_len
    head = pl.program_id(1)
    k_scale = RET_QK_DIM ** -0.5

    lg_f = _log_sigmoid(lgt_ref[pl.ds(head, 1), :])
    lg_b = _log_sigmoid(lgt_ref[pl.ds(RET_HEADS + head, 1), :])
    lgf_k, lgb_k = lg_f[:, :RET_QK_DIM], lg_b[:, :RET_QK_DIM]

    ii = lax.broadcasted_iota(jnp.int32, (c_len, c_len), 0)
    jj = lax.broadcasted_iota(jnp.int32, (c_len, c_len), 1)
    dif = (ii - jj).astype(F32)
    decay_in = jnp.where(ii >= jj, jnp.exp(lg_f[:, :c_len] * jnp.maximum(dif, 0.0)), 0.0) + \
        jnp.where(jj > ii, jnp.exp(lg_b[:, :c_len] * jnp.maximum(-dif, 0.0)), 0.0)
    pos_k = lax.broadcasted_iota(jnp.int32, (c_len, RET_QK_DIM), 0).astype(F32)
    pos_v = lax.broadcasted_iota(jnp.int32, (c_len, RET_V_DIM), 0).astype(F32)
    dq_f = jnp.exp(lg_f * (pos_v + 1.0))
    dq_b = jnp.exp(lg_b * (c_len - pos_v))
    dk_f = jnp.exp(lgf_k * (c_len - 1.0 - pos_k))
    dk_b = jnp.exp(lgb_k * pos_k)
    dchunk_f = jnp.exp(lg_f * float(c_len))
    dchunk_b = jnp.exp(lg_b * float(c_len))

    pos_c = lax.broadcasted_iota(jnp.int32, (n_ctx, RET_QK_DIM), 0).astype(F32)
    kc = kc_ref[0].astype(F32) * k_scale
    vc = vc_ref[0]
    sf_scr[...] = _dot_t0((kc * jnp.exp(lgf_k * (n_ctx - 1.0 - pos_c))).astype(BF16), vc)
    sb_scr[...] = _dot_t0((kc * jnp.exp(lgb_k * pos_c)).astype(BF16), vc)

    def fwd(c, carry):
        r0 = pl.multiple_of(c * c_len, c_len)
        rows = pl.ds(r0, c_len)
        cos = cos_ref[rows, :]
        sin = sin_ref[rows, :]
        q = q_ref[0, rows, :].astype(F32)
        k = k_ref[0, rows, :].astype(F32)
        qr = q * cos + _rope_partner(q) * sin
        kr = (k * cos + _rope_partner(k) * sin) * k_scale
        qb = qr.astype(BF16)
        kb = kr.astype(BF16)
        qr_scr[rows, :] = qb
        kr_scr[rows, :] = kr
        v = v_ref[0, rows, :]
        scores = _dot_nt(qb, kb) * decay_in
        o = _dot(scores.astype(BF16), v) + _dot(qb, sf_scr[...].astype(BF16)) * dq_f
        oacc_scr[rows, :] = o
        sf_scr[...] = sf_scr[...] * dchunk_f + _dot_t0((kr * dk_f).astype(BF16), v)
        return carry

    lax.fori_loop(0, n_chunks, fwd, 0, unroll=RET_UNROLL)

    def bwd(t, carry):
        c = n_chunks - 1 - t
        r0 = pl.multiple_of(c * c_len, c_len)
        rows = pl.ds(r0, c_len)
        v = v_ref[0, rows, :]
        o = oacc_scr[rows, :] + _dot(qr_scr[rows, :], sb_scr[...].astype(BF16)) * dq_b
        o = o * lax.rsqrt(jnp.mean(o * o, axis=-1, keepdims=True) + EPS)
        g = g_ref[0, rows, :].astype(F32)
        o_ref[0, rows, :] = (g * _sigmoid(g) * o).astype(o_ref.dtype)
        sb_scr[...] = sb_scr[...] * dchunk_b + _dot_t0((kr_scr[rows, :] * dk_b).astype(BF16), v)
        return carry

    lax.fori_loop(0, n_chunks, bwd, 0, unroll=RET_UNROLL)


def _retention(proj, proj_c, cos_t, sin_t, lgt):
    bsz, n, _ = proj.shape
    n_ctx = proj_c.shape[1]
    kq, kv = RET_QK_DIM, RET_V_DIM
    assert n % RET_CHUNK == 0 and RET_CHUNK <= kv
    kern = functools.partial(_retention_kernel, n_tok=n, n_ctx=n_ctx)
    return pl.pallas_call(
        kern,
        grid=(bsz, RET_HEADS),
        in_specs=[pl.BlockSpec((1, n, kq), lambda b, h: (b, 0, COL_Q // kq + h)),
                  pl.BlockSpec((1, n, kq), lambda b, h: (b, 0, COL_K // kq + h)),
                  pl.BlockSpec((1, n, kv), lambda b, h: (b, 0, COL_V // kv + h)),
                  pl.BlockSpec((1, n, kv), lambda b, h: (b, 0, COL_G // kv + h)),
                  pl.BlockSpec((1, n_ctx, kq), lambda b, h: (b, 0, h)),
                  pl.BlockSpec((1, n_ctx, kv), lambda b, h: (b, 0, QK_W // kv + h)),
                  pl.BlockSpec((n, kq), lambda b, h: (0, 0)),
                  pl.BlockSpec((n, kq), lambda b, h: (0, 0)),
                  pl.BlockSpec((2 * RET_HEADS, kv), lambda b, h: (0, 0))],
        out_specs=pl.BlockSpec((1, n, kv), lambda b, h: (b, 0, h)),
        out_shape=jax.ShapeDtypeStruct((bsz, n, V_W), BF16),
        scratch_shapes=[pltpu.VMEM((n, kq), BF16), pltpu.VMEM((n, kq), F32),
                        pltpu.VMEM((n, kv), F32), pltpu.VMEM((kq, kv), F32),
                        pltpu.VMEM((kq, kv), F32)],
        compiler_params=_cparams("arbitrary", "arbitrary"),
        name="retention",
    )(proj, proj, proj, proj, proj_c, proj_c, cos_t, sin_t, lgt)


def _filter_kernel(feats_ref, w1_ref, b1_ref, fr_ref, w2_ref, b2_ref, w3f_ref, w3b_ref,
                   df_ref, db_ref, fs_ref, fd_ref, *, n_tok):
    fr = fr_ref[...]
    hid = jnp.sin(fr * (_dot3(feats_ref[...], w1_ref[...]) + b1_ref[...]))
    hid = jnp.sin(fr * (_dot3(hid, w2_ref[...]) + b2_ref[...]))
    t = lax.broadcasted_iota(jnp.int32, (n_tok, df_ref.shape[1]), 0).astype(F32) / n_tok

    def one(w3_ref, d_ref):
        f = _dot3(hid, w3_ref[...]) * jnp.exp(-t * d_ref[...])
        return f / jnp.sum(jnp.abs(f), axis=0, keepdims=True)

    hf = one(w3f_ref, df_ref)
    hb = one(w3b_ref, db_ref)
    fs_ref[...] = (hf + hb).astype(fs_ref.dtype)
    fd_ref[...] = (hf - hb).astype(fd_ref.dtype)


def _hyena_filters(n, feats, w1, b1, freq, w2, b2, w3, deltas):
    tc = HY_W
    nf = feats.shape[1]
    kern = functools.partial(_filter_kernel, n_tok=n)
    full = lambda shape: pl.BlockSpec(shape, lambda j: (0, 0))
    return pl.pallas_call(
        kern,
        grid=(HY_W // tc,),
        in_specs=[full((n, nf)), full((nf, FFN_LANES)), full((1, FFN_LANES)), full((1, FFN_LANES)),
                  full((FFN_LANES, FFN_LANES)), full((1, FFN_LANES)),
                  pl.BlockSpec((FFN_LANES, tc), lambda j: (0, j)),
                  pl.BlockSpec((FFN_LANES, tc), lambda j: (0, HY_W // tc + j)),
                  pl.BlockSpec((1, tc), lambda j: (0, j)),
                  pl.BlockSpec((1, tc), lambda j: (0, HY_W // tc + j))],
        out_specs=[pl.BlockSpec((n, tc), lambda j: (0, j)),
                   pl.BlockSpec((n, tc), lambda j: (0, j))],
        out_shape=[jax.ShapeDtypeStruct((n, HY_W), BF16)] * 2,
        compiler_params=_cparams("arbitrary"),
        name="hyena_filters",
    )(feats, w1, b1, freq, w2, b2, w3, w3, deltas, deltas)


def _hyena_pre_kernel(u0_ref, u1_ref, u2_ref, w_ref, b_ref, z_ref, x0_ref, *, n_tok, rows):
    n_steps = n_tok // rows
    tc = z_ref.shape[1]
    halo = 16
    rid = lax.broadcasted_iota(jnp.int32, (rows, tc), 0)

    def conv(u_ref, part, r0, has_prev, has_next):
        x = u_ref[0, pl.ds(r0, rows), :].astype(F32)
        prev_g = u_ref[0, pl.ds(pl.multiple_of(jnp.maximum(r0 - halo, 0), halo), halo), :].astype(F32)
        next_g = u_ref[0, pl.ds(pl.multiple_of(jnp.minimum(r0 + rows, n_tok - halo), halo), halo), :].astype(F32)
        prev_row = jnp.where(has_prev, prev_g[halo - 1:halo, :], 0.0)
        next_row = jnp.where(has_next, next_g[0:1, :], 0.0)
        up = jnp.where(rid == 0, prev_row, pltpu.roll(x, 1, axis=0))
        dn = jnp.where(rid == rows - 1, next_row, pltpu.roll(x, rows - 1, axis=0))
        w = w_ref[part]
        return up * w[0:1, :] + x * w[1:2, :] + dn * w[2:3, :] + b_ref[part]

    def body(s, carry):
        r0 = pl.multiple_of(s * rows, rows)
        has_prev = s > 0
        has_next = s < n_steps - 1
        x0 = conv(u0_ref, 0, r0, has_prev, has_next)
        x1 = conv(u1_ref, 1, r0, has_prev, has_next)
        vv = conv(u2_ref, 2, r0, has_prev, has_next)
        z_ref[pl.ds(r0, rows), :] = (x1 * vv).astype(z_ref.dtype)
        x0_ref[pl.ds(r0, rows), :] = x0.astype(x0_ref.dtype)
        return carry

    lax.fori_loop(0, n_steps, body, 0)


def _hyena_pre(proj, conv_w, conv_b):
    bsz, n, _ = proj.shape
    tc = 256
    nj = HY_W // tc
    rows = min(512, n)
    base = COL_HY // tc
    kern = functools.partial(_hyena_pre_kernel, n_tok=n, rows=rows)
    u_spec = lambda part: pl.BlockSpec((1, n, tc), lambda b, j: (b, 0, base + part * nj + j))
    wp = jnp.zeros((3, 8, HY_W), F32).at[:, :3, :].set(conv_w.reshape(3, 3, HY_W).transpose(1, 0, 2))
    bp = conv_b.reshape(3, 1, HY_W)
    return pl.pallas_call(
        kern,
        grid=(bsz, nj),
        in_specs=[u_spec(0), u_spec(1), u_spec(2),
                  pl.BlockSpec((3, 8, tc), lambda b, j: (0, 0, j)),
                  pl.BlockSpec((3, 1, tc), lambda b, j: (0, 0, j))],
        out_specs=[pl.BlockSpec((n, tc), lambda b, j: (0, b * nj + j)),
                   pl.BlockSpec((n, tc), lambda b, j: (0, b * nj + j))],
        out_shape=[jax.ShapeDtypeStruct((n, bsz * HY_W), BF16)] * 2,
        compiler_params=_cparams("arbitrary", "arbitrary"),
        name="hyena_pre",
    )(proj, proj, proj, wp, bp)


def _dft_factors(n, inverse):
    period = 4 * n
    col = np.arange(n, dtype=np.int64)[None, :]
    hi = np.arange(n // DFT_GROUP, dtype=np.int64)[:, None]
    lo = np.arange(DFT_GROUP, dtype=np.int64)[:, None]
    if inverse:
        m_a, m_b = (2 * col + 1) * (DFT_GROUP * hi), (2 * col + 1) * lo
    else:
        m_a, m_b = (2 * DFT_GROUP * hi) * col, (2 * lo + 1) * col
    ang_a = (m_a % period) * (2.0 * math.pi / period)
    ang_b = (m_b % period) * (2.0 * math.pi / period)
    return tuple(jnp.asarray(t, F32) for t in (np.cos(ang_a), np.sin(ang_a), np.cos(ang_b), np.sin(ang_b)))


def _dft_tile(ca_ref, sa_ref, cb_ref, sb_ref):
    cb = cb_ref[...]
    sb = sb_ref[...]
    cos_rows, sin_rows = [], []
    for g in range(ca_ref.shape[0]):
        ca = ca_ref[g:g + 1, :]
        sa = sa_ref[g:g + 1, :]
        cos_rows.append((ca * cb - sa * sb).astype(BF16))
        sin_rows.append((sa * cb + ca * sb).astype(BF16))
    return jnp.concatenate(cos_rows, axis=0), jnp.concatenate(sin_rows, axis=0)


def _factor_specs(tr, tc, row_axis, col_axis):
    a_spec = pl.BlockSpec((tr // DFT_GROUP, tc), lambda *ids: (ids[row_axis], ids[col_axis]))
    b_spec = pl.BlockSpec((DFT_GROUP, tc), lambda *ids: (0, ids[col_axis]))
    return [a_spec, a_spec, b_spec, b_spec]


def _dft_filter_kernel(ca_ref, sa_ref, cb_ref, sb_ref, fs_ref, fd_ref, a_ref, b_ref, acc_a, acc_b):
    kk = pl.program_id(1)

    @pl.when(kk == 0)
    def _():
        acc_a[...] = jnp.zeros_like(acc_a)
        acc_b[...] = jnp.zeros_like(acc_b)

    cos_t, sin_t = _dft_tile(ca_ref, sa_ref, cb_ref, sb_ref)
    acc_a[...] += _dot(cos_t, fs_ref[...])
    acc_b[...] += _dot(sin_t, fd_ref[...])

    @pl.when(kk == pl.num_programs(1) - 1)
    def _():
        a_ref[...] = acc_a[...]
        b_ref[...] = acc_b[...]


def _dft_filters(factors, fs, fd, tf, tk):
    n, w = fs.shape
    return pl.pallas_call(
        _dft_filter_kernel,
        grid=(n // tf, n // tk),
        in_specs=_factor_specs(tf, tk, 0, 1) + [
            pl.BlockSpec((tk, w), lambda i, kk: (kk, 0)),
            pl.BlockSpec((tk, w), lambda i, kk: (kk, 0))],
        out_specs=[pl.BlockSpec((tf, w), lambda i, kk: (i, 0)),
                   pl.BlockSpec((tf, w), lambda i, kk: (i, 0))],
        out_shape=[jax.ShapeDtypeStruct((n, w), F32)] * 2,
        scratch_shapes=[pltpu.VMEM((tf, w), F32), pltpu.VMEM((tf, w), F32)],
        compiler_params=_cparams("arbitrary", "arbitrary"),
        name="dft_filters",
    )(*factors, fs, fd)


def _dft_fwd_kernel(ca_ref, sa_ref, cb_ref, sb_ref, z_ref, a_ref, b_ref, u_ref, v_ref, acc_p, acc_q):
    kk = pl.program_id(2)

    @pl.when(kk == 0)
    def _():
        acc_p[...] = jnp.zeros_like(acc_p)
        acc_q[...] = jnp.zeros_like(acc_q)

    cos_t, sin_t = _dft_tile(ca_ref, sa_ref, cb_ref, sb_ref)
    z = z_ref[...]
    acc_p[...] += _dot(cos_t, z)
    acc_q[...] += _dot(sin_t, z)

    @pl.when(kk == pl.num_programs(2) - 1)
    def _():
        a = a_ref[...]
        b = b_ref[...]
        w = a.shape[1]
        for s in range(u_ref.shape[1] // w):
            cols = slice(s * w, (s + 1) * w)
            p = acc_p[:, cols]
            q = acc_q[:, cols]
            u_ref[:, cols] = (p * a - q * b).astype(u_ref.dtype)
            v_ref[:, cols] = (p * b + q * a).astype(v_ref.dtype)


def _dft_fwd(factors, z, spec_a, spec_b, tf, tn, tk):
    n, ncol = z.shape
    w = spec_a.shape[1]
    return pl.pallas_call(
        _dft_fwd_kernel,
        grid=(n // tf, ncol // tn, n // tk),
        in_specs=_factor_specs(tf, tk, 0, 2) + [
            pl.BlockSpec((tk, tn), lambda i, j, kk: (kk, j)),
            pl.BlockSpec((tf, w), lambda i, j, kk: (i, 0)),
            pl.BlockSpec((tf, w), lambda i, j, kk: (i, 0))],
        out_specs=[pl.BlockSpec((tf, tn), lambda i, j, kk: (i, j)),
                   pl.BlockSpec((tf, tn), lambda i, j, kk: (i, j))],
        out_shape=[jax.ShapeDtypeStruct((n, ncol), BF16)] * 2,
        scratch_shapes=[pltpu.VMEM((tf, tn), F32), pltpu.VMEM((tf, tn), F32)],
        compiler_params=_cparams("arbitrary", "arbitrary", "arbitrary"),
        name="dft_fwd",
    )(*factors, z, spec_a, spec_b)


def _dft_inv_kernel(ca_ref, sa_ref, cb_ref, sb_ref, u_ref, v_ref, z_ref, x0_ref, skip_ref, o_ref, acc,
                    *, inv_n):
    kk = pl.program_id(2)

    @pl.when(kk == 0)
    def _():
        acc[...] = jnp.zeros_like(acc)

    cos_t, sin_t = _dft_tile(ca_ref, sa_ref, cb_ref, sb_ref)
    acc[...] += _dot(cos_t, u_ref[...]) + _dot(sin_t, v_ref[...])

    @pl.when(kk == pl.num_programs(2) - 1)
    def _():
        y = acc[...] * inv_n + z_ref[...].astype(F32) * skip_ref[...]
        o_ref[...] = (x0_ref[...].astype(F32) * y).astype(o_ref.dtype)


def _dft_inv(factors, u, v, z, x0, skip_t, tt, tn, tk):
    n, ncol = u.shape
    kern = functools.partial(_dft_inv_kernel, inv_n=1.0 / n)
    return pl.pallas_call(
        kern,
        grid=(n // tt, ncol // tn, n // tk),
        in_specs=_factor_specs(tt, tk, 0, 2) + [
                  pl.BlockSpec((tk, tn), lambda i, j, kk: (kk, j)),
                  pl.BlockSpec((tk, tn), lambda i, j, kk: (kk, j)),
                  pl.BlockSpec((tt, tn), lambda i, j, kk: (i, j)),
                  pl.BlockSpec((tt, tn), lambda i, j, kk: (i, j)),
                  pl.BlockSpec((1, tn), lambda i, j, kk: (0, j))],
        out_specs=pl.BlockSpec((tt, tn), lambda i, j, kk: (i, j)),
        out_shape=jax.ShapeDtypeStruct((n, ncol), BF16),
        scratch_shapes=[pltpu.VMEM((tt, tn), F32)],
        compiler_params=_cparams("arbitrary", "arbitrary", "arbitrary"),
        name="dft_inv",
    )(*factors, u, v, z, x0, skip_t)


def _merge_kernel(ret_ref, hy_ref, gr0_ref, gr1_ref, gh0_ref, gh1_ref, x_ref, gt1_ref, sh2_ref,
                  sc2_ref, g2_ref, wro_ref, who_ref, wout_ref, rwh_ref, rwl_ref, rb_ref,
                  x1_ref, h2_ref, route_ref, routet_ref, cnt_ref, base_scr, *, sub):
    first = jnp.logical_and(pl.program_id(0) == 0, pl.program_id(1) == 0)

    @pl.when(first)
    def _():
        base_scr[...] = jnp.zeros_like(base_scr)

    for s0 in range(0, x_ref.shape[1], sub):
        _merge_rows(slice(s0, s0 + sub), ret_ref, hy_ref, gr0_ref, gr1_ref, gh0_ref, gh1_ref, x_ref,
                    gt1_ref, sh2_ref, sc2_ref, g2_ref, wro_ref, who_ref, wout_ref, rwh_ref, rwl_ref,
                    rb_ref, x1_ref, h2_ref, route_ref, routet_ref, base_scr)
    cnt_ref[...] = base_scr[...]


def _merge_rows(rows, ret_ref, hy_ref, gr0_ref, gr1_ref, gh0_ref, gh1_ref, x_ref, gt1_ref, sh2_ref,
                sc2_ref, g2_ref, wro_ref, who_ref, wout_ref, rwh_ref, rwl_ref, rb_ref,
                x1_ref, h2_ref, route_ref, routet_ref, base_scr):
    tm = rows.stop - rows.start
    gate_r = jnp.concatenate([gr0_ref[0, rows, :], gr1_ref[0, rows, :]], axis=1).astype(F32)
    gate_h = jnp.concatenate([gh0_ref[0, rows, :], gh1_ref[0, rows, :]], axis=1).astype(F32)
    mixed = _sigmoid(gate_r) * _dot(ret_ref[0, rows, :], wro_ref[...]) + \
        _sigmoid(gate_h) * _dot(hy_ref[rows, :], who_ref[...])
    x1 = x_ref[0, rows, :] + gt1_ref[0] * _dot(mixed.astype(BF16), wout_ref[...])
    x1_ref[0, rows, :] = x1
    h2 = x1 * lax.rsqrt(jnp.mean(x1 * x1, axis=-1, keepdims=True) + EPS) * g2_ref[...]
    h2 = h2 * (1.0 + sc2_ref[0]) + sh2_ref[0]
    h2_ref[0, rows, :] = _pack_halves(h2)

    h2_hi, h2_lo = _split_bf16(h2)
    rw_hi = rwh_ref[...]
    logits = _dot(h2_hi, rw_hi) + _dot(h2_lo, rw_hi) + _dot(h2_hi, rwl_ref[...]) + rb_ref[...]
    lane = lax.broadcasted_iota(jnp.int32, logits.shape, 1)
    lane_f = lane.astype(F32)
    big = float(ROUTE_LANES)

    def first_lane(mask):
        return jnp.min(jnp.where(mask, lane_f, big), axis=1, keepdims=True)

    is_group = lane < N_GROUPS
    gl = jnp.where(is_group, logits, NEG)
    ge = jnp.where(is_group, jnp.exp(gl - jnp.max(gl, axis=1, keepdims=True)), 0.0)
    group_p = ge / jnp.sum(ge, axis=1, keepdims=True)
    p_star = jnp.max(group_p, axis=1, keepdims=True)
    g_star = first_lane(jnp.logical_and(is_group, group_p == p_star))
    lo = EXPERT_LANE0 + g_star * EXPERTS_PER_GROUP
    in_group = jnp.logical_and(lane_f >= lo, lane_f < lo + EXPERTS_PER_GROUP)
    el = jnp.where(in_group, logits, NEG)
    ee = jnp.where(in_group, jnp.exp(el - jnp.max(el, axis=1, keepdims=True)), 0.0)
    sp = jnp.where(in_group, ee / jnp.sum(ee, axis=1, keepdims=True), -1.0)
    w_a = jnp.max(sp, axis=1, keepdims=True)
    l_a = first_lane(sp == w_a)
    sp2 = jnp.where(lane_f == l_a, -1.0, sp)
    w_b = jnp.max(sp2, axis=1, keepdims=True)
    l_b = first_lane(sp2 == w_b)
    wsum = w_a + w_b
    wt_a = p_star * w_a / wsum
    wt_b = p_star * w_b / wsum

    hit_a = lane_f == l_a
    hit_b = lane_f == l_b
    onehot = jnp.where(jnp.logical_or(hit_a, hit_b), 1.0, 0.0)
    ri = lax.broadcasted_iota(jnp.int32, (tm, tm), 0)
    ci = lax.broadcasted_iota(jnp.int32, (tm, tm), 1)
    tri = jnp.where(ri > ci, 1.0, 0.0).astype(BF16)
    before = _dot(tri, onehot.astype(BF16)) + base_scr[0:1, :]
    rank_a = jnp.sum(jnp.where(hit_a, before, 0.0), axis=1, keepdims=True)
    rank_b = jnp.sum(jnp.where(hit_b, before, 0.0), axis=1, keepdims=True)
    base_scr[0:1, :] = base_scr[0:1, :] + jnp.sum(onehot, axis=0, keepdims=True)

    vals = (l_a - EXPERT_LANE0, l_b - EXPERT_LANE0, wt_a, wt_b, rank_a, rank_b)
    route = jnp.zeros(logits.shape, F32)
    for idx, val in enumerate(vals):
        route = jnp.where(lane == idx, val, route)
    route_ref[0, rows, :] = route
    routet_ref[:, rows] = route.T[:SUBLANES, :]


def _merge(ret, hy, proj, x, gt1, sh2, sc2, g2, w_ro, w_ho, w_out, rw_hi, rw_lo, rb, tm, sub):
    bsz, n, d = x.shape
    nt = n // tm
    hw = HY_W
    gspec = lambda col: pl.BlockSpec((1, tm, hw), lambda b, i: (b, i, col // hw))
    vec = pl.BlockSpec((1, 1, d), lambda b, i: (b, 0, 0))
    full = lambda shape: pl.BlockSpec(shape, lambda b, i: (0, 0))
    tok = pl.BlockSpec((1, tm, d), lambda b, i: (b, i, 0))
    return pl.pallas_call(
        functools.partial(_merge_kernel, sub=sub),
        grid=(bsz, nt),
        in_specs=[tok,
                  pl.BlockSpec((tm, hw), lambda b, i: (i, b)),
                  gspec(COL_GR), gspec(COL_GR + hw), gspec(COL_GH), gspec(COL_GH + hw),
                  tok, vec, vec, vec, full((1, d)),
                  full((V_W, d)), full((hw, d)), full((d, d)),
                  full((d, ROUTE_LANES)), full((d, ROUTE_LANES)), full((1, ROUTE_LANES))],
        out_specs=[tok, pl.BlockSpec((1, tm, d // 2), lambda b, i: (b, i, 0)),
                   pl.BlockSpec((1, tm, ROUTE_LANES), lambda b, i: (b, i, 0)),
                   pl.BlockSpec((SUBLANES, tm), lambda b, i: (0, b * nt + i)),
                   pl.BlockSpec((8, ROUTE_LANES), lambda b, i: (0, 0))],
        out_shape=[jax.ShapeDtypeStruct((bsz, n, d), F32),
                   jax.ShapeDtypeStruct((bsz, n, d // 2), jnp.int32),
                   jax.ShapeDtypeStruct((bsz, n, ROUTE_LANES), F32),
                   jax.ShapeDtypeStruct((SUBLANES, bsz * n), F32),
                   jax.ShapeDtypeStruct((8, ROUTE_LANES), F32)],
        scratch_shapes=[pltpu.VMEM((8, ROUTE_LANES), F32)],
        compiler_params=_cparams("arbitrary", "arbitrary"),
        name="merge_route",
    )(ret, hy, proj, proj, proj, proj, x, gt1, sh2, sc2, g2, w_ro, w_ho, w_out, rw_hi, rw_lo, rb)


def _dest_kernel(ps_ref, rt_ref, o_ref):
    x = rt_ref[...]
    start = jnp.zeros(x.shape, jnp.int32)
    for e in range(N_EXPERTS):
        start = jnp.where(x == float(e), ps_ref[e], start)
    o_ref[...] = start + pltpu.roll(x.astype(jnp.int32), SUBLANES // 2, axis=0)


def _dest_rows(pad_start, route_t):
    rows, t_all = route_t.shape
    return pl.pallas_call(
        _dest_kernel,
        grid_spec=pltpu.PrefetchScalarGridSpec(
            num_scalar_prefetch=1,
            grid=(1,),
            in_specs=[pl.BlockSpec((rows, t_all), lambda i, ps: (0, 0))],
            out_specs=pl.BlockSpec((rows, t_all), lambda i, ps: (0, 0))),
        out_shape=jax.ShapeDtypeStruct((rows, t_all), jnp.int32),
        compiler_params=_cparams("arbitrary"),
        name="moe_dest",
    )(pad_start, route_t)


def _dispatch_kernel(da_ref, db_ref, pend_ref, h_ref, xb_ref, zero_scr, sem, zsem, *, tm):
    i = pl.program_id(0)

    def row_copy(g, u, dst_row):
        return pltpu.make_async_copy(h_ref.at[g, pl.ds(u, 1)], xb_ref.at[pl.ds(dst_row, 1)], sem)

    @pl.when(i == 0)
    def _():
        zero_scr[...] = jnp.zeros_like(zero_scr)

        def zcopy(e):
            return pltpu.make_async_copy(
                zero_scr, xb_ref.at[pl.ds(pl.multiple_of(pend_ref[e] - MOE_BLK, MOE_BLK), MOE_BLK)], zsem)

        def nonempty(e):
            prev = jnp.where(e > 0, pend_ref[jnp.maximum(e - 1, 0)], 0)
            return pend_ref[e] > prev

        for e in range(N_EXPERTS):
            @pl.when(nonempty(e))
            def _():
                zcopy(e).start()
        for e in range(N_EXPERTS):
            @pl.when(nonempty(e))
            def _():
                zcopy(e).wait()

        def tail_copy(blk):
            return pltpu.make_async_copy(
                zero_scr, xb_ref.at[pl.ds(pl.multiple_of(blk * MOE_BLK, MOE_BLK), MOE_BLK)], zsem)

        first_unused = pend_ref[N_EXPERTS - 1] // MOE_BLK
        n_blocks = xb_ref.shape[0] // MOE_BLK
        lax.fori_loop(first_unused, n_blocks, lambda blk, c: (tail_copy(blk).start(), c)[1], 0)
        lax.fori_loop(first_unused, n_blocks, lambda blk, c: (tail_copy(blk).wait(), c)[1], 0)

    def issue(g, carry):
        t0 = i * tm + g * SUBLANES
        for u in range(SUBLANES):
            row_copy(g, u, da_ref[t0 + u]).start(priority=0)
            row_copy(g, u, db_ref[t0 + u]).start(priority=1)
        return carry

    lax.fori_loop(0, tm // SUBLANES, issue, 0)

    def drain(g, carry):
        for _ in range(2 * SUBLANES):
            row_copy(0, 0, 0).wait()
        return carry

    lax.fori_loop(0, tm // SUBLANES, drain, 0)


def _dispatch(dest_a, dest_b, pad_end, h2, n_rows, tm):
    t_all, d = h2.shape
    kern = functools.partial(_dispatch_kernel, tm=tm)
    return pl.pallas_call(
        kern,
        grid_spec=pltpu.PrefetchScalarGridSpec(
            num_scalar_prefetch=3,
            grid=(t_all // tm,),
            in_specs=[pl.BlockSpec((tm // SUBLANES, SUBLANES, d), lambda i, *_: (i, 0, 0))],
            out_specs=pl.BlockSpec(memory_space=pl.ANY),
            scratch_shapes=[pltpu.VMEM((MOE_BLK, d), h2.dtype),
                            pltpu.SemaphoreType.DMA(()), pltpu.SemaphoreType.DMA(())]),
        out_shape=jax.ShapeDtypeStruct((n_rows, d), h2.dtype),
        compiler_params=_cparams("arbitrary"),
        name="moe_dispatch",
    )(dest_a, dest_b, pad_end, h2.reshape(t_all // SUBLANES, SUBLANES, d))


def _expert_kernel(be_ref, nu_ref, x_ref, w1_ref, w3_ref, w2_ref, o_ref, w1_scr, w3_scr, w2_scr):
    i = pl.program_id(0)

    @pl.when(jnp.logical_or(i == 0, be_ref[i] != be_ref[jnp.maximum(i - 1, 0)]))
    def _():
        w1_scr[...] = w1_ref[0].astype(BF16)
        w3_scr[...] = w3_ref[0].astype(BF16)
        w2_scr[...] = w2_ref[0].astype(BF16)

    @pl.when(i < nu_ref[0])
    def _():
        x = _unpack_halves(x_ref[...]).astype(BF16)
        a = _dot(x, w1_scr[...])
        b = _dot(x, w3_scr[...])
        o_ref[...] = _pack_halves(_dot((a * _sigmoid(a) * b).astype(BF16), w2_scr[...]))

    @pl.when(i >= nu_ref[0])
    def _():
        o_ref[...] = jnp.zeros_like(o_ref)


def _experts(block_expert, n_used, xb, w1, w3, w2):
    n_rows, dp = xb.shape
    _, d, hid = w1.shape
    row_blk = lambda i, be, nu: (jnp.minimum(i, nu[0] - 1), 0)
    return pl.pallas_call(
        _expert_kernel,
        grid_spec=pltpu.PrefetchScalarGridSpec(
            num_scalar_prefetch=2,
            grid=(n_rows // MOE_BLK,),
            in_specs=[pl.BlockSpec((MOE_BLK, dp), row_blk),
                      pl.BlockSpec((1, d, hid), lambda i, be, nu: (be[i], 0, 0)),
                      pl.BlockSpec((1, d, hid), lambda i, be, nu: (be[i], 0, 0)),
                      pl.BlockSpec((1, hid, d), lambda i, be, nu: (be[i], 0, 0))],
            out_specs=pl.BlockSpec((MOE_BLK, dp), lambda i, be, nu: (i, 0)),
            scratch_shapes=[pltpu.VMEM((d, hid), BF16), pltpu.VMEM((d, hid), BF16),
                            pltpu.VMEM((hid, d), BF16)]),
        out_shape=jax.ShapeDtypeStruct((n_rows, dp), xb.dtype),
        compiler_params=_cparams("arbitrary"),
        name="moe_experts",
    )(block_expert, n_used, xb, w1, w3, w2)


def _combine_kernel(da_ref, db_ref, x1_ref, route_ref, gt2_ref, gf_ref, yb_ref, o_ref,
                    buf, sems, *, tm, tiles_per_batch):
    b = pl.program_id(0)
    i = pl.program_id(1)
    step = b * tiles_per_batch + i
    n_steps = pl.num_programs(0) * tiles_per_batch
    slot = step % 2

    def row_copy(src_row, s, which, g, u):
        return pltpu.make_async_copy(yb_ref.at[pl.ds(src_row, 1)], buf.at[s, which, g, pl.ds(u, 1)],
                                     sems.at[s])

    def issue_tile(tile, s):
        def body(g, carry):
            t0 = tile * tm + g * SUBLANES
            for u in range(SUBLANES):
                row_copy(da_ref[t0 + u], s, 0, g, u).start(priority=0)
                row_copy(db_ref[t0 + u], s, 1, g, u).start(priority=1)
            return carry
        lax.fori_loop(0, tm // SUBLANES, body, 0)

    @pl.when(step == 0)
    def _():
        issue_tile(0, 0)

    @pl.when(step + 1 < n_steps)
    def _():
        issue_tile(step + 1, 1 - slot)

    def drain(g, carry):
        for _ in range(SUBLANES):
            row_copy(0, slot, 0, 0, 0).wait()
            row_copy(0, slot, 1, 0, 0).wait()
        return carry

    lax.fori_loop(0, tm // SUBLANES, drain, 0)

    route = route_ref[0]
    dp = buf.shape[-1]
    y = route[:, 2:3] * _unpack_halves(buf[slot, 0].reshape(tm, dp)) + \
        route[:, 3:4] * _unpack_halves(buf[slot, 1].reshape(tm, dp))
    xo = x1_ref[0] + gt2_ref[0] * y
    o_ref[0] = xo * lax.rsqrt(jnp.mean(xo * xo, axis=-1, keepdims=True) + EPS) * gf_ref[...]


def _combine(dest_a, dest_b, x1, route, gt2, gf, yb, tm):
    bsz, n, d = x1.shape
    nt = n // tm
    kern = functools.partial(_combine_kernel, tm=tm, tiles_per_batch=nt)
    return pl.pallas_call(
        kern,
        grid_spec=pltpu.PrefetchScalarGridSpec(
            num_scalar_prefetch=2,
            grid=(bsz, nt),
            in_specs=[pl.BlockSpec((1, tm, d), lambda b, i, *_: (b, i, 0)),
                      pl.BlockSpec((1, tm, ROUTE_LANES), lambda b, i, *_: (b, i, 0)),
                      pl.BlockSpec((1, 1, d), lambda b, i, *_: (b, 0, 0)),
                      pl.BlockSpec((1, d), lambda b, i, *_: (0, 0)),
                      pl.BlockSpec(memory_space=pl.ANY)],
            out_specs=pl.BlockSpec((1, tm, d), lambda b, i, *_: (b, i, 0)),
            scratch_shapes=[pltpu.VMEM((2, 2, tm // SUBLANES, SUBLANES, yb.shape[1]), yb.dtype),
                            pltpu.SemaphoreType.DMA((2,))]),
        out_shape=jax.ShapeDtypeStruct((bsz, n, d), F32),
        compiler_params=_cparams("arbitrary", "arbitrary"),
        name="moe_combine",
    )(dest_a, dest_b, x1, route, gt2, gf, yb)


def _rope_tables(n):
    rows = n // GRID_W
    r, col = np.meshgrid(np.arange(rows, dtype=np.float64), np.arange(GRID_W, dtype=np.float64), indexing='ij')
    n_freq = RET_QK_DIM // 4
    inv_freq = ROPE_BASE ** (-np.arange(n_freq, dtype=np.float64) / n_freq)
    ang_r = r.reshape(-1)[:, None] * inv_freq
    ang_c = col.reshape(-1)[:, None] * inv_freq
    cos_t = np.concatenate([np.cos(ang_r), np.cos(ang_r), np.cos(ang_c), np.cos(ang_c)], axis=-1)
    sin_t = np.concatenate([-np.sin(ang_r), np.sin(ang_r), -np.sin(ang_c), np.sin(ang_c)], axis=-1)
    return jnp.asarray(cos_t, F32), jnp.asarray(sin_t, F32)


def _hyena_feats(n):
    t = np.arange(n, dtype=np.float64) / n
    bands = np.linspace(1e-4, HY_BANDS - 1, HY_BANDS)
    phase = 2.0 * math.pi * t[:, None] * bands[None, :]
    feats = np.concatenate([t[:, None], np.cos(phase), -np.sin(phase)], axis=-1)
    return jnp.asarray(np.pad(feats, ((0, 0), (0, FFN_LANES - HY_POS_DIM))), F32)


def _layer(x, ctx, mods, norm1_g, norm2_g, w_in, b_in, ret_decay_logit, ret_w_o, hy_conv_w,
           hy_conv_b, hy_ffn_w1, hy_ffn_b1, hy_ffn_freq, hy_ffn_w2, hy_ffn_b2, hy_ffn_w3, hy_skip,
           hy_w_o, w_out, router_group_w, router_group_b, router_expert_w, router_expert_b,
           expert_w1, expert_w3, expert_w2, final_norm_g):
    bsz, n, d = x.shape
    n_ctx = ctx.shape[1]
    mod_lat = mods[:bsz].reshape(bsz, 6, 1, d)
    sh1, sc1, gt1, sh2, sc2, gt2 = (mod_lat[:, s] for s in range(6))
    mod_ctx = mods[bsz].reshape(6, 1, 1, d)
    csh1 = jnp.broadcast_to(mod_ctx[0], (bsz, 1, d))
    csc1 = jnp.broadcast_to(mod_ctx[1], (bsz, 1, d))

    g1 = norm1_g.reshape(1, d)
    b_in2 = b_in.reshape(1, IN_W)
    w_in_b = w_in.astype(BF16)
    proj = _inproj(x, g1, sh1, sc1, w_in_b, b_in2, tm=min(512, n), tn=512)
    proj_c = _inproj(ctx, g1, csh1, csc1, w_in_b[:, COL_K:COL_G], b_in2[:, COL_K:COL_G],
                     tm=n_ctx, tn=512)

    cos_r, sin_r = _rope_tables(n)
    lgt = jnp.broadcast_to(ret_decay_logit.astype(F32).reshape(2 * RET_HEADS, 1),
                           (2 * RET_HEADS, RET_V_DIM))
    ret = _retention(proj, proj_c, cos_r, sin_r, lgt)

    slow = abs(math.log(HY_DECAY_TARGET)) / HY_SLOW_PCT
    fast = abs(math.log(HY_DECAY_TARGET)) / HY_FAST_PCT
    deltas = jnp.tile(jnp.linspace(slow, fast, HY_W, dtype=F32), 2).reshape(1, 2 * HY_W)
    fpad = FFN_LANES - HY_FFN
    row = lambda a: jnp.pad(a.reshape(1, HY_FFN), ((0, 0), (0, fpad)))
    fs, fd = _hyena_filters(n, _hyena_feats(n),
                            jnp.pad(hy_ffn_w1, ((0, FFN_LANES - HY_POS_DIM), (0, fpad))), row(hy_ffn_b1),
                            row(hy_ffn_freq), jnp.pad(hy_ffn_w2, ((0, fpad), (0, fpad))), row(hy_ffn_b2),
                            jnp.pad(hy_ffn_w3, ((0, fpad), (0, 0))), deltas)
    z, x0 = _hyena_pre(proj, hy_conv_w, hy_conv_b)
    fac_fwd = _dft_factors(n, inverse=False)
    fac_inv = _dft_factors(n, inverse=True)
    tile = min(512, n)
    spec_a, spec_b = _dft_filters(fac_fwd, fs, fd, tf=tile, tk=min(2048, n))
    tn = min(2048, bsz * HY_W)
    tk = min(1024, n)
    u, v = _dft_fwd(fac_fwd, z, spec_a, spec_b, tf=tile, tn=tn, tk=min(2048, n))
    skip_t = jnp.tile(hy_skip.reshape(1, HY_W), (1, bsz))
    hy = _dft_inv(fac_inv, u, v, z, x0, skip_t, tt=tile, tn=tn, tk=tk)

    rw = jnp.zeros((d, ROUTE_LANES), F32)
    rw = rw.at[:, :N_GROUPS].set(router_group_w).at[:, EXPERT_LANE0:EXPERT_LANE0 + N_EXPERTS].set(router_expert_w)
    rb = jnp.zeros((1, ROUTE_LANES), F32)
    rb = rb.at[0, :N_GROUPS].set(router_group_b).at[0, EXPERT_LANE0:EXPERT_LANE0 + N_EXPERTS].set(router_expert_b)
    rw_hi, rw_lo = _split_bf16(rw)
    sub = min(512, n)
    x1, h2, route, route_t, cnt = _merge(ret, hy, proj, x, gt1, sh2, sc2, norm2_g.reshape(1, d),
                                ret_w_o.astype(BF16), hy_w_o.astype(BF16), w_out.astype(BF16),
                                rw_hi, rw_lo, rb, tm=min(2 * sub, n), sub=sub)

    t_all = bsz * n
    counts = cnt[0, EXPERT_LANE0:EXPERT_LANE0 + N_EXPERTS].astype(jnp.int32)
    padded = (counts + MOE_BLK - 1) // MOE_BLK * MOE_BLK
    pad_end = jnp.cumsum(padded)
    pad_start = pad_end - padded
    dest = _dest_rows(pad_start.astype(jnp.int32), route_t)
    dest_a, dest_b = dest[0], dest[1]
    n_blocks = -(-(2 * t_all + N_EXPERTS * (MOE_BLK - 1)) // MOE_BLK)
    blk0 = jnp.arange(n_blocks, dtype=jnp.int32) * MOE_BLK
    block_expert = jnp.minimum(jnp.sum(blk0[:, None] >= pad_end[None, :], axis=1), N_EXPERTS - 1).astype(jnp.int32)
    n_used = (pad_end[-1:] // MOE_BLK).astype(jnp.int32)

    xb = _dispatch(dest_a, dest_b, pad_end.astype(jnp.int32), h2.reshape(t_all, h2.shape[-1]),
                   n_blocks * MOE_BLK, tm=min(1024, n))
    yb = _experts(block_expert, n_used, xb, expert_w1, expert_w3, expert_w2)
    return _combine(dest_a, dest_b, x1, route, gt2, final_norm_g.reshape(1, d), yb, tm=min(1024, n))


def kernel(x, c, ctx, c_ctx, ada_w, ada_b, norm1_g, norm2_g, w_in, b_in, ret_decay_logit, ret_w_o, hy_conv_w, hy_conv_b, hy_ffn_w1, hy_ffn_b1, hy_ffn_freq, hy_ffn_w2, hy_ffn_b2, hy_ffn_w3, hy_skip, hy_w_o, w_out, router_group_w, router_group_b, router_expert_w, router_expert_b, expert_w1, expert_w3, expert_w2, final_norm_g):
    depth = ada_w.shape[0]
    assert depth == 1, "single-layer problem: the context stream is only read by the retention states"
    bsz, d = c.shape
    rows = -(-(bsz + 1) // 8) * 8
    cc = jnp.zeros((rows, d), F32).at[:bsz].set(c).at[bsz].set(c_ctx)
    mods = _adaln(cc, ada_w[0], ada_b[0].reshape(1, -1))
    return _layer(x, ctx, mods, norm1_g[0], norm2_g[0], w_in[0], b_in[0], ret_decay_logit[0],
                  ret_w_o[0], hy_conv_w[0], hy_conv_b[0], hy_ffn_w1[0], hy_ffn_b1[0], hy_ffn_freq[0],
                  hy_ffn_w2[0], hy_ffn_b2[0], hy_ffn_w3[0], hy_skip[0], hy_w_o[0], w_out[0],
                  router_group_w[0], router_group_b[0], router_expert_w[0], router_expert_b[0],
                  expert_w1[0], expert_w3[0], expert_w2[0], final_norm_g)
```
